```python
import math
import jax
import jax.numpy as jnp
from jax import lax
import numpy as np

D_MODEL = 2048
BATCH = 2
SEQ = 8192
DEPTH = 1

N_META = 16
POOL_WIDTH = D_MODEL // 2
POOL_WINDOWS = (2, 4, 8, 16)
POOL_GROUPS = len(POOL_WINDOWS)
POOL_GROUP_WIDTH = POOL_WIDTH // POOL_GROUPS
SSM_WIDTH = D_MODEL // 2
SSM_GROUP_CH = 16
SSM_GROUPS = SSM_WIDTH // SSM_GROUP_CH
SSM_STATE = 64
DT_MIN = 1e-3
DT_MAX = 1e-1
GATE_OFFSET = POOL_WIDTH + SSM_WIDTH
IN_WIDTH = GATE_OFFSET + 2 * D_MODEL
MOE_GROUPS = 8
EXPERTS_PER_GROUP = 8
N_EXPERTS = MOE_GROUPS * EXPERTS_PER_GROUP
TOP_K_WITHIN = 2
EXPERT_FF = D_MODEL // 4
MOE_BLOCK = 128
RMS_EPS = 1e-6

kernel_name = 'hybrid_pool_s5_hmoe_block'


def rms_norm(x, g):
    x32 = x.astype(jnp.float32)
    y = x32 * lax.rsqrt(jnp.mean(x32 * x32, axis=-1, keepdims=True) + RMS_EPS)
    return (y * g.astype(jnp.float32)).astype(x.dtype)


def pool_mixer(u, pool_w, pool_scale):
    bn, L, C = u.shape
    u32 = u.astype(jnp.float32)
    cs = jnp.cumsum(u32, axis=1)
    pos = jnp.arange(L)
    parts = []
    for g, w in enumerate(POOL_WINDOWS):
        sl = slice(g * POOL_GROUP_WIDTH, (g + 1) * POOL_GROUP_WIDTH)
        csg = cs[..., sl]
        shifted = jnp.pad(csg, ((0, 0), (w, 0), (0, 0)))[:, :L]
        cnt = jnp.minimum(pos + 1, w).astype(jnp.float32)[None, :, None]
        parts.append((csg - shifted) / cnt - u32[..., sl])
    d = jnp.stack(parts, axis=2).astype(u.dtype)
    y = jnp.einsum('blgc,gcd->blgd', d, pool_w).reshape(bn, L, C)
    return y * pool_scale


def s5_ssm(u, a_re, a_im, log_dt, b_re, b_im, c_re, c_im, d):
    bn, L, Q = u.shape
    u32 = u.astype(jnp.float32)
    ug = u32.reshape(bn, L, SSM_GROUPS, SSM_GROUP_CH)
    ar = a_re.astype(jnp.float32)
    ai = a_im.astype(jnp.float32)
    dt = jnp.exp(log_dt.astype(jnp.float32))[:, None]
    mag = jnp.exp(ar * dt)
    lam_re = mag * jnp.cos(ai * dt)
    lam_im = mag * jnp.sin(ai * dt)
    den = ar * ar + ai * ai
    nr = lam_re - 1.0
    coef_re = (nr * ar + lam_im * ai) / den
    coef_im = (lam_im * ar - nr * ai) / den
    br = b_re.astype(jnp.float32)
    bi = b_im.astype(jnp.float32)
    bb_re = coef_re[..., None] * br - coef_im[..., None] * bi
    bb_im = coef_re[..., None] * bi + coef_im[..., None] * br
    bu_re = jnp.einsum('blgh,gnh->blgn', ug, bb_re)
    bu_im = jnp.einsum('blgh,gnh->blgn', ug, bb_im)
    la_re = jnp.broadcast_to(lam_re, bu_re.shape)
    la_im = jnp.broadcast_to(lam_im, bu_im.shape)

    def combine(e1, e2):
        a1r, a1i, b1r, b1i = e1
        a2r, a2i, b2r, b2i = e2
        return (a2r * a1r - a2i * a1i,
                a2r * a1i + a2i * a1r,
                a2r * b1r - a2i * b1i + b2r,
                a2r * b1i + a2i * b1r + b2i)

    _, _, xr, xi = lax.associative_scan(combine, (la_re, la_im, bu_re, bu_im), axis=1)
    y = (jnp.einsum('blgn,ghn->blgh', xr, c_re.astype(jnp.float32))
         - jnp.einsum('blgn,ghn->blgh', xi, c_im.astype(jnp.float32)))
    y = y.reshape(bn, L, Q) + d.astype(jnp.float32) * u32
    return y.astype(u.dtype)


def hier_moe(v, w_rg, b_rg, w_re, b_re, w_gate, w_up, w_down):
    bn, L, D = v.shape
    N = bn * L
    vf = v.reshape(N, D)
    gl = (vf @ w_rg).astype(jnp.float32) + b_rg.astype(jnp.float32)
    p_group = jax.nn.softmax(gl, axis=-1)
    _, g_star = lax.top_k(gl, 1)
    pg = jnp.take_along_axis(p_group, g_star, axis=1)
    el = ((vf @ w_re).astype(jnp.float32) + b_re.astype(jnp.float32)).reshape(N, MOE_GROUPS, EXPERTS_PER_GROUP)
    g_idx = jnp.broadcast_to(g_star[:, :, None], (N, 1, EXPERTS_PER_GROUP))
    el_sel = jnp.take_along_axis(el, g_idx, axis=1)[:, 0]
    ev, ei = lax.top_k(el_sel, TOP_K_WITHIN)
    gate = pg * jax.nn.softmax(ev, axis=-1)
    eid = g_star * EXPERTS_PER_GROUP + ei

    M = N * TOP_K_WITHIN
    e_flat = eid.reshape(M)
    w_flat = gate.reshape(M)
    tok_flat = jnp.arange(M, dtype=jnp.int32) // TOP_K_WITHIN
    order = jnp.argsort(e_flat)
    se = e_flat[order]
    stok = tok_flat[order]
    sw = w_flat[order]
    counts = jnp.zeros((N_EXPERTS,), jnp.int32).at[e_flat].add(1)
    padded = ((counts + MOE_BLOCK - 1) // MOE_BLOCK) * MOE_BLOCK
    pend = jnp.cumsum(padded)
    pstart = pend - padded
    start = jnp.cumsum(counts) - counts
    dest = pstart[se] + (jnp.arange(M, dtype=jnp.int32) - start[se])
    T = ((M + N_EXPERTS * (MOE_BLOCK - 1) + MOE_BLOCK - 1) // MOE_BLOCK) * MOE_BLOCK
    NB = T // MOE_BLOCK
    buf_tok = jnp.full((T,), N, jnp.int32).at[dest].set(stok)
    buf_w = jnp.zeros((T,), jnp.float32).at[dest].set(sw)
    block_e = jnp.clip(jnp.searchsorted(pend, jnp.arange(NB, dtype=jnp.int32) * MOE_BLOCK, side='right'),
                       0, N_EXPERTS - 1)
    vpad = jnp.concatenate([vf, jnp.zeros((1, D), vf.dtype)], axis=0)
    xb = vpad[buf_tok].reshape(NB, MOE_BLOCK, D)

    def expert_block(args):
        xblk, e = args
        hg = xblk @ w_gate[e]
        hu = xblk @ w_up[e]
        return (jax.nn.silu(hg) * hu) @ w_down[e]

    yb = lax.map(expert_block, (xb, block_e)).reshape(T, D)
    yb = yb * buf_w[:, None].astype(yb.dtype)
    out = jnp.zeros((N + 1, D), yb.dtype).at[buf_tok].add(yb)[:N]
    return out.reshape(bn, L, D)


def setup_inputs(seed: int = 0) -> dict:
    key = jax.random.key(seed)
    ks = jax.random.split(key, 32)
    f32 = jnp.float32

    def nrm(k, shape, scale):
        return jax.random.normal(k, shape, f32) * scale

    D = D_MODEL
    G, N, H = SSM_GROUPS, SSM_STATE, SSM_GROUP_CH
    n_idx = jnp.arange(SSM_STATE, dtype=f32)
    return {
        'x': nrm(ks[0], (BATCH, SEQ, D), 1.0),
        'meta': nrm(ks[1], (N_META, D), 1.0),
        'norm_mix': 1.0 + nrm(ks[2], (DEPTH, D), 0.02),
        'w_in': nrm(ks[3], (DEPTH, D, IN_WIDTH), D ** -0.5),
        'pool_w': nrm(ks[4], (DEPTH, POOL_GROUPS, POOL_GROUP_WIDTH, POOL_GROUP_WIDTH), POOL_GROUP_WIDTH ** -0.5),
        'pool_scale': 1.0 + nrm(ks[5], (DEPTH, POOL_WIDTH), 0.1),
        'ssm_a_re': -0.5 + nrm(ks[6], (DEPTH, G, N), 0.01),
        'ssm_a_im': math.pi * n_idx + nrm(ks[7], (DEPTH, G, N), 0.01),
        'ssm_log_dt': jax.random.uniform(ks[8], (DEPTH, G), f32, math.log(DT_MIN), math.log(DT_MAX)),
        'ssm_b_re': nrm(ks[9], (DEPTH, G, N, H), (2 * H) ** -0.5),
        'ssm_b_im': nrm(ks[10], (DEPTH, G, N, H), (2 * H) ** -0.5),
        'ssm_c_re': nrm(ks[11], (DEPTH, G, H, N), N ** -0.5),
        'ssm_c_im': nrm(ks[12], (DEPTH, G, H, N), N ** -0.5),
        'ssm_d': nrm(ks[13], (DEPTH, SSM_WIDTH), 1.0),
        'w_glu': nrm(ks[14], (DEPTH, SSM_WIDTH, SSM_WIDTH), SSM_WIDTH ** -0.5),
        'b_glu': nrm(ks[15], (DEPTH, SSM_WIDTH), 0.01),
        'w_branch_pool': nrm(ks[16], (DEPTH, POOL_WIDTH, D), POOL_WIDTH ** -0.5),
        'w_branch_ssm': nrm(ks[17], (DEPTH, SSM_WIDTH, D), SSM_WIDTH ** -0.5),
        'w_out': nrm(ks[18], (DEPTH, D, D), D ** -0.5),
        'norm_ffn': 1.0 + nrm(ks[19], (DEPTH, D), 0.02),
        'w_router_group': nrm(ks[20], (DEPTH, D, MOE_GROUPS), D ** -0.5),
        'b_router_group': nrm(ks[21], (DEPTH, MOE_GROUPS), 0.01),
        'w_router_expert': nrm(ks[22], (DEPTH, D, N_EXPERTS), D ** -0.5),
        'b_router_expert': nrm(ks[23], (DEPTH, N_EXPERTS), 0.01),
        'w_gate': nrm(ks[24], (DEPTH, N_EXPERTS, D, EXPERT_FF), D ** -0.5),
        'w_up': nrm(ks[25], (DEPTH, N_EXPERTS, D, EXPERT_FF), D ** -0.5),
        'w_down': nrm(ks[26], (DEPTH, N_EXPERTS, EXPERT_FF, D), EXPERT_FF ** -0.5),
        'norm_final': 1.0 + nrm(ks[27], (D,), 0.02),
    }


def reference(x, meta, norm_mix, w_in, pool_w, pool_scale, ssm_a_re, ssm_a_im, ssm_log_dt,
              ssm_b_re, ssm_b_im, ssm_c_re, ssm_c_im, ssm_d, w_glu, b_glu, w_branch_pool,
              w_branch_ssm, w_out, norm_ffn, w_router_group, b_router_group, w_router_expert,
              b_router_expert, w_gate, w_up, w_down, norm_final):
    bn = x.shape[0]
    meta_b = jnp.broadcast_to(meta[None].astype(x.dtype), (bn, N_META, D_MODEL))
    h = jnp.concatenate([meta_b, x], axis=1)
    for l in range(DEPTH):
        u = rms_norm(h, norm_mix[l])
        z = u @ w_in[l]
        z_pool = z[..., :POOL_WIDTH]
        z_ssm = z[..., POOL_WIDTH:GATE_OFFSET]
        gate_pool = jax.nn.sigmoid(z[..., GATE_OFFSET:GATE_OFFSET + D_MODEL])
        gate_ssm = jax.nn.sigmoid(z[..., GATE_OFFSET + D_MODEL:])
        y_pool = pool_mixer(z_pool, pool_w[l], pool_scale[l]) @ w_branch_pool[l]
        s = s5_ssm(z_ssm, ssm_a_re[l], ssm_a_im[l], ssm_log_dt[l], ssm_b_re[l], ssm_b_im[l],
                   ssm_c_re[l], ssm_c_im[l], ssm_d[l])
        s = jax.nn.gelu(s)
        s = s * jax.nn.sigmoid(s @ w_glu[l] + b_glu[l])
        y_ssm = s @ w_branch_ssm[l]
        merged = gate_pool * y_pool + gate_ssm * y_ssm
        h = h + merged @ w_out[l]
        v = rms_norm(h, norm_ffn[l])
        h = h + hier_moe(v, w_router_group[l], b_router_group[l], w_router_expert[l],
                         b_router_expert[l], w_gate[l], w_up[l], w_down[l])
    return rms_norm(h, norm_final)[:, N_META:]
```

```python
import functools
import math

import jax
import jax.numpy as jnp
from jax import lax
from jax.experimental import pallas as pl
from jax.experimental.pallas import tpu as pltpu

F32 = jnp.float32
BF16 = jnp.bfloat16

RMS_EPS = 1e-6
POOL_WINDOWS = (2, 4, 8, 16)
N_META = 16
SSM_CHUNK = 16
SSM_GROUP_CH = 16
SSM_STATE = 64
MOE_GROUPS = 8
EXPERTS_PER_GROUP = 8
ROUTER_LANES = 128
EXPERT_ROWS = 256
VMEM_LIMIT = 56 * 1024 * 1024


def _cparams(sem, vmem=VMEM_LIMIT):
    return pltpu.CompilerParams(dimension_semantics=sem, vmem_limit_bytes=vmem)


def _sigmoid(x):
    return 1.0 / (1.0 + jnp.exp(-x))


def _gelu_tanh(x):
    c = math.sqrt(2.0 / math.pi)
    return 0.5 * x * (1.0 + jnp.tanh(c * (x + 0.044715 * (x * x * x))))


def _rms(x, g):
    ms = jnp.mean(x * x, axis=-1, keepdims=True)
    return x * lax.rsqrt(ms + RMS_EPS) * g


def _rms_matmul_kernel(x_ref, g_ref, w_ref, o_ref, u_ref, *, act):
    @pl.when(pl.program_id(1) == 0)
    def _():
        u_ref[...] = _rms(x_ref[...], g_ref[...]).astype(BF16)

    acc = jnp.dot(u_ref[...], w_ref[...], preferred_element_type=F32)
    if act:
        acc = _sigmoid(acc)
    o_ref[...] = acc.astype(o_ref.dtype)


def rms_matmul(x, g, w, *, act, out_dtype, tm=512, tn=512):
    n, d = x.shape
    cols = w.shape[1]
    tm = min(tm, n)
    return pl.pallas_call(
        functools.partial(_rms_matmul_kernel, act=act),
        grid=(n // tm, cols // tn),
        in_specs=[
            pl.BlockSpec((tm, d), lambda i, j: (i, 0)),
            pl.BlockSpec((1, d), lambda i, j: (0, 0)),
            pl.BlockSpec((d, tn), lambda i, j: (0, j)),
        ],
        out_specs=pl.BlockSpec((tm, tn), lambda i, j: (i, j)),
        out_shape=jax.ShapeDtypeStruct((n, cols), out_dtype),
        scratch_shapes=[pltpu.VMEM((tm, d), BF16)],
        compiler_params=_cparams(("parallel", "arbitrary")),
        name="rms_matmul",
    )(x, g, w)


def _pool_kernel(z_ref, zm_ref, w_ref, sc_ref, o_ref, ext_ref, *, tl, gw):
    h = N_META

    @pl.when(pl.program_id(1) == 0)
    def _():
        ext_ref[0:h, :] = zm_ref[...]

    ext_ref[h:h + tl, :] = z_ref[...]
    for g, w in enumerate(POOL_WINDOWS):
        cols = slice(g * gw, (g + 1) * gw)
        u = ext_ref[h:h + tl, cols]
        s = u
        for k in range(1, w):
            s = s + ext_ref[h - k:h - k + tl, cols]
        d = s * (1.0 / w) - u
        y = jnp.dot(d.astype(BF16), w_ref[g], preferred_element_type=F32)
        o_ref[:, cols] = (y * sc_ref[:, cols]).astype(o_ref.dtype)
    ext_ref[0:h, :] = ext_ref[tl:tl + h, :]


def pool_mixer(z, zmeta, pool_w, pool_scale, *, tl=512):
    b, l, _ = z.shape
    ng, gw, _ = pool_w.shape
    c = ng * gw
    tl = min(tl, l)
    return pl.pallas_call(
        functools.partial(_pool_kernel, tl=tl, gw=gw),
        grid=(b, l // tl),
        in_specs=[
            pl.BlockSpec((None, tl, c), lambda i, t: (i, t, 0)),
            pl.BlockSpec((N_META, c), lambda i, t: (0, 0)),
            pl.BlockSpec((ng, gw, gw), lambda i, t: (0, 0, 0)),
            pl.BlockSpec((1, c), lambda i, t: (0, 0)),
        ],
        out_specs=pl.BlockSpec((None, tl, c), lambda i, t: (i, t, 0)),
        out_shape=jax.ShapeDtypeStruct((b, l, c), BF16),
        scratch_shapes=[pltpu.VMEM((tl + N_META, c), F32)],
        compiler_params=_cparams(("arbitrary", "arbitrary")),
        name="pool_mixer",
    )(z, zmeta, pool_w, pool_scale)


def _ssm_kernel(u_ref, um_ref, m_ref, p_ref, q_ref, ar_ref, ai_ref, o_ref, *, nc, nsteps):
    u = u_ref[...]
    p = p_ref[...]
    y_intra = jnp.dot(u, m_ref[...], preferred_element_type=F32)
    s = jnp.dot(u, p, preferred_element_type=F32)
    x0 = jnp.dot(um_ref[...], p, preferred_element_type=F32)[0:1]
    r = s.shape[0]
    c = lax.broadcasted_iota(jnp.int32, (r, 2 * SSM_STATE), 0) & (nc - 1)
    x = jnp.where(c == 0, x0, pltpu.roll(s, 1, axis=0))
    d = 1
    for k in range(nsteps):
        z = jnp.where(c >= d, pltpu.roll(x, d, axis=0), 0.0)
        x = x + ar_ref[k:k + 1, :] * z + ai_ref[k:k + 1, :] * pltpu.roll(z, SSM_STATE, axis=1)
        d *= 2
    y = y_intra + jnp.dot(x.astype(BF16), q_ref[...], preferred_element_type=F32)
    o_ref[...] = y.astype(o_ref.dtype)


def ssm_chunked(u, um, m, p, q, ar, ai, *, nc):
    g, r, w = u.shape
    nsteps = ar.shape[1]
    ns2 = p.shape[2]
    return pl.pallas_call(
        functools.partial(_ssm_kernel, nc=nc, nsteps=int(math.log2(nc))),
        grid=(g,),
        in_specs=[
            pl.BlockSpec((None, r, w), lambda i: (i, 0, 0)),
            pl.BlockSpec((None, um.shape[1], w), lambda i: (i, 0, 0)),
            pl.BlockSpec((None, w, w), lambda i: (i, 0, 0)),
            pl.BlockSpec((None, w, ns2), lambda i: (i, 0, 0)),
            pl.BlockSpec((None, ns2, w), lambda i: (i, 0, 0)),
            pl.BlockSpec((None, nsteps, ns2), lambda i: (i, 0, 0)),
            pl.BlockSpec((None, nsteps, ns2), lambda i: (i, 0, 0)),
        ],
        out_specs=pl.BlockSpec((None, r, w), lambda i: (i, 0, 0)),
        out_shape=jax.ShapeDtypeStruct((g, r, w), F32),
        compiler_params=_cparams(("parallel",)),
        name="ssm_chunked",
    )(u, um, m, p, q, ar, ai)


def _ssm_operators(a_re, a_im, log_dt, b_re, b_im, c_re, c_im, nc):
    t = SSM_CHUNK
    hp = lax.Precision.HIGHEST
    ar = a_re.astype(F32)
    ai = a_im.astype(F32)
    dt = jnp.exp(log_dt.astype(F32))[:, None]
    mag = jnp.exp(ar * dt)
    lam_re = mag * jnp.cos(ai * dt)
    lam_im = mag * jnp.sin(ai * dt)
    den = ar * ar + ai * ai
    nr = lam_re - 1.0
    coef_re = (nr * ar + lam_im * ai) / den
    coef_im = (lam_im * ar - nr * ai) / den
    br = b_re.astype(F32)
    bi = b_im.astype(F32)
    bb_re = coef_re[..., None] * br - coef_im[..., None] * bi
    bb_im = coef_re[..., None] * bi + coef_im[..., None] * br
    cr = c_re.astype(F32)
    ci = c_im.astype(F32)

    def lam_pow(k):
        k = k.astype(F32)
        m = jnp.exp(ar[..., None] * dt[..., None] * k)
        ang = ai[..., None] * dt[..., None] * k
        return m * jnp.cos(ang), m * jnp.sin(ang)

    ks = jnp.arange(t + 1)
    pr, pi = lam_pow(ks)
    lb_re = pr[..., None] * bb_re[:, :, None, :] - pi[..., None] * bb_im[:, :, None, :]
    lb_im = pr[..., None] * bb_im[:, :, None, :] + pi[..., None] * bb_re[:, :, None, :]
    kk = (jnp.einsum('gjn,gnkh->gkjh', cr, lb_re, precision=hp)
          - jnp.einsum('gjn,gnkh->gkjh', ci, lb_im, precision=hp))
    s_idx = jnp.arange(t)[:, None]
    t_idx = jnp.arange(t)[None, :]
    lag = t_idx - s_idx
    m5 = kk[:, jnp.clip(lag, 0, t)]
    m5 = jnp.where((lag >= 0)[None, :, :, None, None], m5, 0.0)
    g = ar.shape[0]
    hh = SSM_GROUP_CH
    m_op = m5.transpose(0, 1, 4, 2, 3).reshape(g, t * hh, t * hh)
    rev = t - 1 - jnp.arange(t)
    p_re = lb_re[:, :, rev, :].transpose(0, 2, 3, 1).reshape(g, t * hh, -1)
    p_im = lb_im[:, :, rev, :].transpose(0, 2, 3, 1).reshape(g, t * hh, -1)
    p_op = jnp.concatenate([p_re, p_im], axis=-1)
    qr = pr[:, :, 1:]
    qi = pi[:, :, 1:]
    crt = cr.transpose(0, 2, 1)
    cit = ci.transpose(0, 2, 1)
    q_re = (qr[..., None] * crt[:, :, None, :] - qi[..., None] * cit[:, :, None, :])
    q_im = (qr[..., None] * cit[:, :, None, :] + qi[..., None] * crt[:, :, None, :])
    q_op = jnp.concatenate([q_re.reshape(g, -1, t * hh), -q_im.reshape(g, -1, t * hh)], axis=1)
    nsteps = int(math.log2(nc))
    sr, si = lam_pow(t * (2 ** jnp.arange(nsteps)))
    sr = sr.transpose(0, 2, 1)
    si = si.transpose(0, 2, 1)
    ar_op = jnp.concatenate([sr, sr], axis=-1)
    ai_op = jnp.concatenate([-si, si], axis=-1)
    return m_op.astype(BF16), p_op.astype(BF16), q_op.astype(BF16), ar_op, ai_op


def _merge_kernel(p_ref, y_ref, us_ref, gp_ref, gs_ref, h_ref, dv_ref, wglu_ref, bglu_ref,
                  wbp_ref, wbs_ref, wout_ref, nf_ref, wr_ref, br_ref, h1_ref, lg_ref):
    s = _gelu_tanh(y_ref[...] + dv_ref[...] * us_ref[...])
    gl = jnp.dot(s.astype(BF16), wglu_ref[...], preferred_element_type=F32) + bglu_ref[...]
    s = s * _sigmoid(gl)
    y_ssm = jnp.dot(s.astype(BF16), wbs_ref[...], preferred_element_type=F32)
    y_pool = jnp.dot(p_ref[...], wbp_ref[...], preferred_element_type=F32)
    merged = gp_ref[...].astype(F32) * y_pool + gs_ref[...].astype(F32) * y_ssm
    h1 = h_ref[...] + jnp.dot(merged.astype(BF16), wout_ref[...], preferred_element_type=F32)
    h1_ref[...] = h1
    v = _rms(h1, nf_ref[...])
    lg_ref[...] = jnp.dot(v.astype(BF16), wr_ref[...], preferred_element_type=F32) + br_ref[...]


def merge(pool_y, ssm_y, zps, gates, h0, dvec, w_glu, b_glu, w_bp, w_bs, w_out, nf, w_r, b_r,
          *, tm=256):
    n, d = h0.shape
    q = pool_y.shape[1]
    tm = min(tm, n)
    const = lambda shape: pl.BlockSpec(shape, lambda i: (0,) * len(shape),
                                       pipeline_mode=pl.Buffered(1))
    return pl.pallas_call(
        _merge_kernel,
        grid=(n // tm,),
        in_specs=[
            pl.BlockSpec((tm, q), lambda i: (i, 0)),
            pl.BlockSpec((tm, q), lambda i: (i, 0)),
            pl.BlockSpec((tm, q), lambda i: (i, 1)),
            pl.BlockSpec((tm, d), lambda i: (i, 0)),
            pl.BlockSpec((tm, d), lambda i: (i, 1)),
            pl.BlockSpec((tm, d), lambda i: (i, 0)),
            const((1, q)), const((q, q)), const((1, q)),
            const((q, d)), const((q, d)), const((d, d)),
            const((1, d)), const((d, ROUTER_LANES)), const((1, ROUTER_LANES)),
        ],
        out_specs=[
            pl.BlockSpec((tm, d), lambda i: (i, 0)),
            pl.BlockSpec((tm, ROUTER_LANES), lambda i: (i, 0)),
        ],
        out_shape=[
            jax.ShapeDtypeStruct((n, d), F32),
            jax.ShapeDtypeStruct((n, ROUTER_LANES), F32),
        ],
        compiler_params=_cparams(("parallel",)),
        name="merge",
    )(pool_y, ssm_y, zps, gates, gates, h0, dvec, w_glu, b_glu, w_bp, w_bs, w_out, nf, w_r, b_r)


def _route_kernel(lg_ref, gate_ref, eid_ref):
    l = lg_ref[...]
    lane = lax.broadcasted_iota(jnp.int32, l.shape, 1).astype(F32)
    neg = -jnp.inf
    big = float(ROUTER_LANES)
    is_g = lane < MOE_GROUPS
    gl = jnp.where(is_g, l, neg)
    gmax = jnp.max(gl, axis=1, keepdims=True)
    gstar = jnp.min(jnp.where(gl == gmax, lane, big), axis=1, keepdims=True)
    pg = 1.0 / jnp.sum(jnp.where(is_g, jnp.exp(gl - gmax), 0.0), axis=1, keepdims=True)
    lo = MOE_GROUPS + EXPERTS_PER_GROUP * gstar
    is_e = (lane >= lo) & (lane < lo + EXPERTS_PER_GROUP)
    el = jnp.where(is_e, l, neg)
    m1 = jnp.max(el, axis=1, keepdims=True)
    i1 = jnp.min(jnp.where(is_e & (el == m1), lane, big), axis=1, keepdims=True)
    is_e2 = is_e & (lane != i1)
    el2 = jnp.where(is_e2, l, neg)
    m2 = jnp.max(el2, axis=1, keepdims=True)
    i2 = jnp.min(jnp.where(is_e2 & (el2 == m2), lane, big), axis=1, keepdims=True)
    t = jnp.exp(m2 - m1)
    w1 = pg / (1.0 + t)
    w2 = pg * t / (1.0 + t)
    gate_ref[...] = jnp.where(lane == 0.0, w1, jnp.where(lane == 1.0, w2, 0.0))
    eid = jnp.where(lane == 0.0, i1, jnp.where(lane == 1.0, i2, float(MOE_GROUPS)))
    eid_ref[...] = (eid - MOE_GROUPS).astype(jnp.int32)


def route(logits, *, tm=1024):
    n, w = logits.shape
    tm = min(tm, n)
    return pl.pallas_call(
        _route_kernel,
        grid=(n // tm,),
        in_specs=[pl.BlockSpec((tm, w), lambda i: (i, 0))],
        out_specs=[pl.BlockSpec((tm, w), lambda i: (i, 0)),
                   pl.BlockSpec((tm, w), lambda i: (i, 0))],
        out_shape=[jax.ShapeDtypeStruct((n, w), F32), jax.ShapeDtypeStruct((n, w), jnp.int32)],
        compiler_params=_cparams(("parallel",)),
        name="route",
    )(logits)


def _expert_kernel(be_ref, nv_ref, nu_ref, code_ref, h_hbm, nf_ref, wg_ref, wu_ref, wd_ref, y_hbm,
                   xbuf, ybuf, wgb, wub, wdb, gsem, ssem, *, rows, ntok):
    i = pl.program_id(0)
    nused = nu_ref[0]
    slot = lax.rem(i, 2)

    def gather(blk, sl):
        def body(r, _):
            tok = jnp.maximum(code_ref[blk * rows + r], 0) >> 1
            pltpu.make_async_copy(h_hbm.at[pl.ds(tok, 1), :], xbuf.at[sl, pl.ds(r, 1), :],
                                  gsem.at[sl]).start()
            return 0
        lax.fori_loop(0, rows, body, 0)

    def scatter(blk):
        def body(r, _):
            code = code_ref[blk * rows + r]
            orow = (code & 1) * ntok + (code >> 1)
            pltpu.make_async_copy(ybuf.at[pl.ds(r, 1), :], y_hbm.at[pl.ds(orow, 1), :],
                                  ssem).start()
            return 0
        lax.fori_loop(0, nv_ref[blk], body, 0)

    def wait_scatter(blk):
        nv = nv_ref[blk]
        p = 1
        while p <= rows:
            @pl.when((nv & p) != 0)
            def _(p=p):
                pltpu.make_async_copy(ybuf.at[pl.ds(0, p), :], y_hbm.at[pl.ds(0, p), :],
                                      ssem).wait()
            p *= 2

    @pl.when(i == 0)
    def _():
        gather(0, 0)

    @pl.when(i < nused)
    def _():
        e = be_ref[i]
        changed = jnp.logical_or(i == 0, e != be_ref[jnp.maximum(i - 1, 0)])

        @pl.when(changed)
        def _():
            wgb[...] = wg_ref[...].astype(BF16)
            wub[...] = wu_ref[...].astype(BF16)
            wdb[...] = wd_ref[...].astype(BF16)

        pltpu.make_async_copy(h_hbm.at[pl.ds(0, rows), :], xbuf.at[slot], gsem.at[slot]).wait()

        @pl.when(i + 1 < nused)
        def _():
            gather(i + 1, 1 - slot)

        v = _rms(xbuf[slot], nf_ref[...]).astype(BF16)
        hg = jnp.dot(v, wgb[...], preferred_element_type=F32)
        hu = jnp.dot(v, wub[...], preferred_element_type=F32)
        a = (hg * _sigmoid(hg) * hu).astype(BF16)
        y = jnp.dot(a, wdb[...], preferred_element_type=F32)

        @pl.when(i > 0)
        def _():
            wait_scatter(i - 1)

        ybuf[...] = y
        scatter(i)

        @pl.when(i == nused - 1)
        def _():
            wait_scatter(i)


def experts(block_e, nvalid, nused, code, h1, nf, w_gate, w_up, w_down, *, rows=EXPERT_ROWS):
    n, d = h1.shape
    ne, _, ff = w_gate.shape
    nblk = block_e.shape[0]
    grid_spec = pltpu.PrefetchScalarGridSpec(
        num_scalar_prefetch=4,
        grid=(nblk,),
        in_specs=[
            pl.BlockSpec(memory_space=pl.ANY),
            pl.BlockSpec((1, d), lambda i, *_: (0, 0)),
            pl.BlockSpec((None, d, ff), lambda i, be, *_: (be[i], 0, 0)),
            pl.BlockSpec((None, d, ff), lambda i, be, *_: (be[i], 0, 0)),
            pl.BlockSpec((None, ff, d), lambda i, be, *_: (be[i], 0, 0)),
        ],
        out_specs=pl.BlockSpec(memory_space=pl.ANY),
        scratch_shapes=[
            pltpu.VMEM((2, rows, d), F32),
            pltpu.VMEM((rows, d), F32),
            pltpu.VMEM((d, ff), BF16),
            pltpu.VMEM((d, ff), BF16),
            pltpu.VMEM((ff, d), BF16),
            pltpu.SemaphoreType.DMA((2,)),
            pltpu.SemaphoreType.DMA,
        ],
    )
    return pl.pallas_call(
        functools.partial(_expert_kernel, rows=rows, ntok=n),
        grid_spec=grid_spec,
        out_shape=jax.ShapeDtypeStruct((2 * n, d), F32),
        compiler_params=_cparams(("arbitrary",)),
        name="experts",
    )(block_e, nvalid, nused, code, h1, nf, w_gate, w_up, w_down)


def _dispatch_plan(eid, rows, n_experts):
    n = eid.shape[0]
    m = 2 * n
    e_flat = eid.reshape(m)
    order = jnp.argsort(e_flat).astype(jnp.int32)
    se = e_flat[order]
    counts = jnp.zeros((n_experts,), jnp.int32).at[e_flat].add(1)
    padded = ((counts + rows - 1) // rows) * rows
    pend = jnp.cumsum(padded)
    pstart = pend - padded
    start = jnp.cumsum(counts) - counts
    dest = pstart[se] + (jnp.arange(m, dtype=jnp.int32) - start[se])
    nblk = (m + n_experts * (rows - 1) + rows - 1) // rows
    code = jnp.full((nblk * rows,), -1, jnp.int32).at[dest].set(order)
    block_e = jnp.clip(jnp.searchsorted(pend, jnp.arange(nblk, dtype=jnp.int32) * rows,
                                        side='right'), 0, n_experts - 1).astype(jnp.int32)
    nused = (pend[-1:] // rows).astype(jnp.int32)
    nvalid = jnp.sum((code.reshape(nblk, rows) >= 0).astype(jnp.int32), axis=1)
    return block_e, nvalid, nused, code


def _combine_kernel(h_ref, y0_ref, y1_ref, gate_ref, g_ref, o_ref):
    gate = gate_ref[...]
    h = h_ref[...] + gate[:, 0:1] * y0_ref[...] + gate[:, 1:2] * y1_ref[...]
    o_ref[...] = _rms(h, g_ref[...])


def combine(h1, y, gate, g, *, tm=512):
    n, d = h1.shape
    tm = min(tm, n)
    return pl.pallas_call(
        _combine_kernel,
        grid=(n // tm,),
        in_specs=[
            pl.BlockSpec((tm, d), lambda i: (i, 0)),
            pl.BlockSpec((None, tm, d), lambda i: (0, i, 0)),
            pl.BlockSpec((None, tm, d), lambda i: (1, i, 0)),
            pl.BlockSpec((tm, gate.shape[1]), lambda i: (i, 0)),
            pl.BlockSpec((1, d), lambda i: (0, 0)),
        ],
        out_specs=pl.BlockSpec((tm, d), lambda i: (i, 0)),
        out_shape=jax.ShapeDtypeStruct((n, d), F32),
        compiler_params=_cparams(("parallel",)),
        name="combine",
    )(h1, y, y, gate, g)


def kernel(x, meta, norm_mix, w_in, pool_w, pool_scale, ssm_a_re, ssm_a_im, ssm_log_dt, ssm_b_re,
           ssm_b_im, ssm_c_re, ssm_c_im, ssm_d, w_glu, b_glu, w_branch_pool, w_branch_ssm, w_out,
           norm_ffn, w_router_group, b_router_group, w_router_expert, b_router_expert, w_gate,
           w_up, w_down, norm_final):
    assert w_in.shape[0] == 1, "single layer"
    bsz, seq, d = x.shape
    n = bsz * seq
    q = pool_scale.shape[1]
    assert meta.shape[0] == N_META == SSM_CHUNK and N_META >= max(POOL_WINDOWS)
    nc = seq // SSM_CHUNK
    groups = q // SSM_GROUP_CH
    cw = SSM_CHUNK * SSM_GROUP_CH

    xf = x.reshape(n, d)
    g_mix = norm_mix[0][None, :]
    w_in_b = w_in[0].astype(BF16)
    w_ps = w_in_b[:, :2 * q]
    w_gt = w_in_b[:, 2 * q:]

    zps = rms_matmul(xf, g_mix, w_ps, act=False, out_dtype=F32)
    zps_meta = rms_matmul(meta.astype(F32), g_mix, w_ps, act=False, out_dtype=F32)
    gates = rms_matmul(xf, g_mix, w_gt, act=True, out_dtype=BF16)

    pool_y = pool_mixer(zps.reshape(bsz, seq, 2 * q), zps_meta, pool_w[0].astype(BF16),
                        pool_scale[0][None, :]).reshape(n, q)

    m_op, p_op, q_op, ar_op, ai_op = _ssm_operators(
        ssm_a_re[0], ssm_a_im[0], ssm_log_dt[0], ssm_b_re[0], ssm_b_im[0], ssm_c_re[0],
        ssm_c_im[0], nc)
    zs = zps[:, q:].reshape(bsz * nc, SSM_CHUNK, groups, SSM_GROUP_CH)
    u_g = zs.transpose(2, 0, 1, 3).reshape(groups, bsz * nc, cw).astype(BF16)
    um = zps_meta[:, q:].reshape(SSM_CHUNK, groups, SSM_GROUP_CH).transpose(1, 0, 2)
    um = jnp.pad(um.reshape(groups, 1, cw), ((0, 0), (0, 7), (0, 0))).astype(BF16)
    y_g = ssm_chunked(u_g, um, m_op, p_op, q_op, ar_op, ai_op, nc=nc)
    ssm_y = (y_g.reshape(groups, bsz * nc, SSM_CHUNK, SSM_GROUP_CH)
             .transpose(1, 2, 0, 3).reshape(n, q))

    n_groups = w_router_group.shape[2]
    n_exp = w_router_expert.shape[2]
    w_r = jnp.concatenate([w_router_group[0], w_router_expert[0]], axis=1)
    w_r = jnp.pad(w_r, ((0, 0), (0, ROUTER_LANES - n_groups - n_exp))).astype(BF16)
    b_r = jnp.concatenate([b_router_group[0], b_router_expert[0]])
    b_r = jnp.pad(b_r, (0, ROUTER_LANES - n_groups - n_exp))[None, :].astype(F32)
    h1, logits = merge(pool_y, ssm_y, zps, gates, xf, ssm_d[0][None, :], w_glu[0].astype(BF16),
                       b_glu[0][None, :], w_branch_pool[0].astype(BF16),
                       w_branch_ssm[0].astype(BF16), w_out[0].astype(BF16),
                       norm_ffn[0][None, :], w_r, b_r)

    gate, eid = route(logits)
    block_e, nvalid, nused, code = _dispatch_plan(eid[:, :2], EXPERT_ROWS, n_exp)
    y = experts(block_e, nvalid, nused, code, h1, norm_ffn[0][None, :], w_gate[0], w_up[0],
                w_down[0])
    out = combine(h1, y.reshape(2, n, d), gate, norm_final[None, :])
    return out.reshape(bsz, seq, d)
```

```python
import functools
import math

import jax
import jax.numpy as jnp
from jax import lax
from jax.experimental import pallas as pl
from jax.experimental.pallas import tpu as pltpu

F32 = jnp.float32
BF16 = jnp.bfloat16

RMS_EPS = 1e-6
POOL_WINDOWS = (2, 4, 8, 16)
N_META = 16
LANES = 128
SSM_CHUNK = 16
SSM_GROUP_CH = 16
SSM_STATE = 64
SSM_OCTET = LANES // SSM_GROUP_CH
SSM_ROWS = 16
SSM_UNROLL = 4
MOE_GROUPS = 8
EXPERTS_PER_GROUP = 8
ROUTER_LANES = LANES
EXPERT_ROWS = 256
VMEM_LIMIT = 56 * 1024 * 1024


def _cparams(sem, vmem=VMEM_LIMIT):
    return pltpu.CompilerParams(dimension_semantics=sem, vmem_limit_bytes=vmem)


def _sigmoid(x):
    return 1.0 / (1.0 + jnp.exp(-x))


def _gelu_tanh(x):
    c = math.sqrt(2.0 / math.pi)
    return 0.5 * x * (1.0 + jnp.tanh(c * (x + 0.044715 * (x * x * x))))


def _rms(x, g):
    ms = jnp.mean(x * x, axis=-1, keepdims=True)
    return x * lax.rsqrt(ms + RMS_EPS) * g


def _rms_matmul_kernel(x_ref, g_ref, w_ref, o_ref, u_ref, *, act):
    @pl.when(pl.program_id(1) == 0)
    def _():
        u_ref[...] = _rms(x_ref[...], g_ref[...]).astype(BF16)

    acc = jnp.dot(u_ref[...], w_ref[...], preferred_element_type=F32)
    if act:
        acc = _sigmoid(acc)
    o_ref[...] = acc.astype(o_ref.dtype)


def rms_matmul(x, g, w, *, col0, cols, act, out_dtype, tm=512, tn=512):
    n, d = x.shape
    tm = min(tm, n)
    cb = col0 // tn
    return pl.pallas_call(
        functools.partial(_rms_matmul_kernel, act=act),
        grid=(n // tm, cols // tn),
        in_specs=[
            pl.BlockSpec((tm, d), lambda i, j: (i, 0)),
            pl.BlockSpec((1, d), lambda i, j: (0, 0)),
            pl.BlockSpec((d, tn), lambda i, j: (0, j + cb)),
        ],
        out_specs=pl.BlockSpec((tm, tn), lambda i, j: (i, j)),
        out_shape=jax.ShapeDtypeStruct((n, cols), out_dtype),
        scratch_shapes=[pltpu.VMEM((tm, d), BF16)],
        compiler_params=_cparams(("parallel", "arbitrary")),
        name="rms_matmul",
    )(x, g, w)


def _pool_kernel(z_ref, zm_ref, w_ref, sc_ref, o_ref, ext_ref, *, tl, gw):
    h = N_META

    @pl.when(pl.program_id(1) == 0)
    def _():
        ext_ref[0:h, :] = zm_ref[...]

    ext_ref[h:h + tl, :] = z_ref[...]
    for g, w in enumerate(POOL_WINDOWS):
        cols = slice(g * gw, (g + 1) * gw)
        u = ext_ref[h:h + tl, cols]
        s = u
        for k in range(1, w):
            s = s + ext_ref[h - k:h - k + tl, cols]
        d = s * (1.0 / w) - u
        y = jnp.dot(d.astype(BF16), w_ref[g], preferred_element_type=F32)
        o_ref[:, cols] = (y * sc_ref[:, cols]).astype(o_ref.dtype)
    ext_ref[0:h, :] = ext_ref[tl:tl + h, :]


def pool_mixer(z, zmeta, pool_w, pool_scale, *, tl=512):
    b, l, _ = z.shape
    ng, gw, _ = pool_w.shape
    c = ng * gw
    tl = min(tl, l)
    return pl.pallas_call(
        functools.partial(_pool_kernel, tl=tl, gw=gw),
        grid=(b, l // tl),
        in_specs=[
            pl.BlockSpec((None, tl, c), lambda i, t: (i, t, 0)),
            pl.BlockSpec((N_META, c), lambda i, t: (0, 0)),
            pl.BlockSpec((ng, gw, gw), lambda i, t: (0, 0, 0)),
            pl.BlockSpec((1, c), lambda i, t: (0, 0)),
        ],
        out_specs=pl.BlockSpec((None, tl, c), lambda i, t: (i, t, 0)),
        out_shape=jax.ShapeDtypeStruct((b, l, c), BF16),
        scratch_shapes=[pltpu.VMEM((tl + N_META, c), F32)],
        compiler_params=_cparams(("arbitrary", "arbitrary")),
        name="pool_mixer",
    )(z, zmeta, pool_w, pool_scale)


def _ssm_toeplitz_kernel(bt_ref, cl_ref, m_ref):
    w = m_ref.shape[-1]
    lane = lax.broadcasted_iota(jnp.int32, (SSM_GROUP_CH, w), 1)
    for g in range(bt_ref.shape[0]):
        kt = jnp.dot(bt_ref[g], cl_ref[g], preferred_element_type=F32,
                     precision=lax.Precision.HIGHEST)
        for s in range(SSM_CHUNK):
            off = s * SSM_GROUP_CH
            blk = kt if s == 0 else jnp.where(lane >= off, pltpu.roll(kt, off, axis=1), 0.0)
            m_ref[g, off:off + SSM_GROUP_CH, :] = blk.astype(m_ref.dtype)


def ssm_toeplitz(bt, cl):
    g, hh, k2 = bt.shape
    w = cl.shape[2]
    gb = SSM_OCTET
    return pl.pallas_call(
        _ssm_toeplitz_kernel,
        grid=(g // gb,),
        in_specs=[pl.BlockSpec((gb, hh, k2), lambda i: (i, 0, 0)),
                  pl.BlockSpec((gb, k2, w), lambda i: (i, 0, 0))],
        out_specs=pl.BlockSpec((gb, w, w), lambda i: (i, 0, 0)),
        out_shape=jax.ShapeDtypeStruct((g, w, w), BF16),
        compiler_params=_cparams(("parallel",)),
        name="ssm_toeplitz",
    )(bt, cl)


def _block_transpose8(a, lane):
    for d in (4, 2, 1):
        w = d * SSM_GROUP_CH
        keep = (lane & w) == 0
        b = list(a)
        for k in range(SSM_OCTET):
            if k & d == 0:
                x, y = a[k], a[k + d]
                b[k] = jnp.where(keep, x, pltpu.roll(y, w, axis=1))
                b[k + d] = jnp.where(keep, pltpu.roll(x, LANES - w, axis=1), y)
        a = b
    return a


def _ssm_kernel(z_ref, zm_ref, m_ref, p_ref, q_ref, ar_ref, ai_ref, o_ref, uscr, umscr, yscr,
                *, nc, nsteps):
    rt = SSM_ROWS
    t = SSM_CHUNK
    lane = lax.broadcasted_iota(jnp.int32, (rt, LANES), 1)

    def regroup(src_ref, dst_ref, it):
        base = pl.multiple_of(it * (rt * t), rt * t)
        row0 = pl.multiple_of(it * rt, rt)
        for half in range(2):
            a = [src_ref[pl.ds(base + half * SSM_OCTET + k, rt, stride=t), :]
                 for k in range(SSM_OCTET)]
            b = _block_transpose8(a, lane)
            for j in range(SSM_OCTET):
                dst_ref[j, pl.ds(row0, rt), half * LANES:(half + 1) * LANES] = b[j].astype(BF16)

    regroup(zm_ref, umscr, 0)

    def fwd(it, _):
        regroup(z_ref, uscr, it)
        return 0
    lax.fori_loop(0, nc // rt, fwd, 0, unroll=SSM_UNROLL)

    c = lax.broadcasted_iota(jnp.int32, (nc, 2 * SSM_STATE), 0)

    def per_group(j, _):
        u = uscr[j]
        p = p_ref[j]
        y_intra = jnp.dot(u, m_ref[j], preferred_element_type=F32)
        s = jnp.dot(u, p, preferred_element_type=F32)
        x0 = jnp.dot(umscr[j], p, preferred_element_type=F32)[0:1]
        x = jnp.where(c == 0, x0, pltpu.roll(s, 1, axis=0))
        d = 1
        for k in range(nsteps):
            z = jnp.where(c >= d, pltpu.roll(x, d, axis=0), 0.0)
            x = (x + ar_ref[j, k:k + 1, :] * z
                 + ai_ref[j, k:k + 1, :] * pltpu.roll(z, SSM_STATE, axis=1))
            d *= 2
        yscr[j] = y_intra + jnp.dot(x.astype(BF16), q_ref[j], preferred_element_type=F32)
        return 0
    lax.fori_loop(0, SSM_OCTET, per_group, 0)

    def bwd(it, _):
        base = pl.multiple_of(it * (rt * t), rt * t)
        row0 = pl.multiple_of(it * rt, rt)
        for half in range(2):
            b = [yscr[j, pl.ds(row0, rt), half * LANES:(half + 1) * LANES]
                 for j in range(SSM_OCTET)]
            a = _block_transpose8(b, lane)
            for k in range(SSM_OCTET):
                o_ref[pl.ds(base + half * SSM_OCTET + k, rt, stride=t), :] = a[k]
        return 0
    lax.fori_loop(0, nc // rt, bwd, 0, unroll=SSM_UNROLL)


def ssm_chunked(z, zm, m, p, q, ar, ai, *, bsz, seq, col0):
    g, w, _ = m.shape
    ns2 = p.shape[2]
    nsteps = ar.shape[1]
    nc = seq // SSM_CHUNK
    gb = SSM_OCTET
    cb = col0 // LANES
    return pl.pallas_call(
        functools.partial(_ssm_kernel, nc=nc, nsteps=nsteps),
        grid=(g // gb, bsz),
        in_specs=[
            pl.BlockSpec((seq, LANES), lambda j, b: (b, cb + j)),
            pl.BlockSpec((SSM_ROWS * SSM_CHUNK, LANES), lambda j, b: (0, cb + j)),
            pl.BlockSpec((gb, w, w), lambda j, b: (j, 0, 0)),
            pl.BlockSpec((gb, w, ns2), lambda j, b: (j, 0, 0)),
            pl.BlockSpec((gb, ns2, w), lambda j, b: (j, 0, 0)),
            pl.BlockSpec((gb, nsteps, ns2), lambda j, b: (j, 0, 0)),
            pl.BlockSpec((gb, nsteps, ns2), lambda j, b: (j, 0, 0)),
        ],
        out_specs=pl.BlockSpec((seq, LANES), lambda j, b: (b, j)),
        out_shape=jax.ShapeDtypeStruct((bsz * seq, g * SSM_GROUP_CH), F32),
        scratch_shapes=[
            pltpu.VMEM((gb, nc, w), BF16),
            pltpu.VMEM((gb, SSM_ROWS, w), BF16),
            pltpu.VMEM((gb, nc, w), F32),
        ],
        compiler_params=_cparams(("parallel", "arbitrary")),
        name="ssm_chunked",
    )(z, zm, m, p, q, ar, ai)


def _ssm_operators(a_re, a_im, log_dt, b_re, b_im, c_re, c_im, nc):
    t = SSM_CHUNK
    ar = a_re.astype(F32)
    ai = a_im.astype(F32)
    dt = jnp.exp(log_dt.astype(F32))[:, None]
    mag = jnp.exp(ar * dt)
    lam_re = mag * jnp.cos(ai * dt)
    lam_im = mag * jnp.sin(ai * dt)
    den = ar * ar + ai * ai
    nr = lam_re - 1.0
    coef_re = (nr * ar + lam_im * ai) / den
    coef_im = (lam_im * ar - nr * ai) / den
    br = b_re.astype(F32)
    bi = b_im.astype(F32)
    bb_re = coef_re[..., None] * br - coef_im[..., None] * bi
    bb_im = coef_re[..., None] * bi + coef_im[..., None] * br
    crt = c_re.astype(F32).transpose(0, 2, 1)
    cit = c_im.astype(F32).transpose(0, 2, 1)

    def lam_pow(k):
        k = k.astype(F32)
        m = jnp.exp(ar[..., None] * dt[..., None] * k)
        ang = ai[..., None] * dt[..., None] * k
        return m * jnp.cos(ang), m * jnp.sin(ang)

    g = ar.shape[0]
    hh = SSM_GROUP_CH
    pr, pi = lam_pow(jnp.arange(t + 1))

    def times_c(xr, xi):
        return (xr[..., None] * crt[:, :, None, :] - xi[..., None] * cit[:, :, None, :],
                xr[..., None] * cit[:, :, None, :] + xi[..., None] * crt[:, :, None, :])

    cl_re, cl_im = times_c(pr[:, :, :t], pi[:, :, :t])
    cl = jnp.concatenate([cl_re.reshape(g, -1, t * hh), cl_im.reshape(g, -1, t * hh)], axis=1)
    bt = jnp.concatenate([bb_re.transpose(0, 2, 1), -bb_im.transpose(0, 2, 1)], axis=-1)
    m_op = ssm_toeplitz(bt, cl)
    rr = pr[:, :, t - 1::-1][:, :, :t]
    ri = pi[:, :, t - 1::-1][:, :, :t]
    p_re = rr[..., None] * bb_re[:, :, None, :] - ri[..., None] * bb_im[:, :, None, :]
    p_im = rr[..., None] * bb_im[:, :, None, :] + ri[..., None] * bb_re[:, :, None, :]
    p_op = jnp.concatenate([p_re.transpose(0, 2, 3, 1).reshape(g, t * hh, -1),
                            p_im.transpose(0, 2, 3, 1).reshape(g, t * hh, -1)], axis=-1)
    q_re, q_im = times_c(pr[:, :, 1:], pi[:, :, 1:])
    q_op = jnp.concatenate([q_re.reshape(g, -1, t * hh), -q_im.reshape(g, -1, t * hh)], axis=1)
    nsteps = int(math.log2(nc))
    sr, si = lam_pow(t * (2 ** jnp.arange(nsteps)))
    sr = sr.transpose(0, 2, 1)
    si = si.transpose(0, 2, 1)
    ar_op = jnp.concatenate([sr, sr], axis=-1)
    ai_op = jnp.concatenate([-si, si], axis=-1)
    return m_op, p_op.astype(BF16), q_op.astype(BF16), ar_op, ai_op


def _merge_kernel(p_ref, y_ref, us_ref, gp_ref, gs_ref, h_ref, dv_ref, wglu_ref, bglu_ref,
                  wbp_ref, wbs_ref, wout_ref, nf_ref, wr_ref, br_ref, h1_ref, lg_ref):
    s = _gelu_tanh(y_ref[...] + dv_ref[...] * us_ref[...])
    gl = jnp.dot(s.astype(BF16), wglu_ref[...], preferred_element_type=F32) + bglu_ref[...]
    s = s * _sigmoid(gl)
    y_ssm = jnp.dot(s.astype(BF16), wbs_ref[...], preferred_element_type=F32)
    y_pool = jnp.dot(p_ref[...], wbp_ref[...], preferred_element_type=F32)
    merged = gp_ref[...].astype(F32) * y_pool + gs_ref[...].astype(F32) * y_ssm
    h1 = h_ref[...] + jnp.dot(merged.astype(BF16), wout_ref[...], preferred_element_type=F32)
    h1_ref[...] = h1
    v = _rms(h1, nf_ref[...])
    lg_ref[...] = jnp.dot(v.astype(BF16), wr_ref[...], preferred_element_type=F32) + br_ref[...]


def merge(pool_y, ssm_y, zps, gates, h0, dvec, w_glu, b_glu, w_bp, w_bs, w_out, nf, w_r, b_r,
          *, tm=256):
    n, d = h0.shape
    q = pool_y.shape[1]
    tm = min(tm, n)
    const = lambda shape: pl.BlockSpec(shape, lambda i: (0,) * len(shape),
                                       pipeline_mode=pl.Buffered(1))
    return pl.pallas_call(
        _merge_kernel,
        grid=(n // tm,),
        in_specs=[
            pl.BlockSpec((tm, q), lambda i: (i, 0)),
            pl.BlockSpec((tm, q), lambda i: (i, 0)),
            pl.BlockSpec((tm, q), lambda i: (i, 1)),
            pl.BlockSpec((tm, d), lambda i: (i, 0)),
            pl.BlockSpec((tm, d), lambda i: (i, 1)),
            pl.BlockSpec((tm, d), lambda i: (i, 0)),
            const((1, q)), const((q, q)), const((1, q)),
            const((q, d)), const((q, d)), const((d, d)),
            const((1, d)), const((d, ROUTER_LANES)), const((1, ROUTER_LANES)),
        ],
        out_specs=[
            pl.BlockSpec((tm, d), lambda i: (i, 0)),
            pl.BlockSpec((tm, ROUTER_LANES), lambda i: (i, 0)),
        ],
        out_shape=[
            jax.ShapeDtypeStruct((n, d), F32),
            jax.ShapeDtypeStruct((n, ROUTER_LANES), F32),
        ],
        compiler_params=_cparams(("parallel",)),
        name="merge",
    )(pool_y, ssm_y, zps, gates, gates, h0, dvec, w_glu, b_glu, w_bp, w_bs, w_out, nf, w_r, b_r)


def _route_kernel(lg_ref, gate_ref, eid_ref):
    l = lg_ref[...]
    lane = lax.broadcasted_iota(jnp.int32, l.shape, 1).astype(F32)
    neg = -jnp.inf
    big = float(ROUTER_LANES)
    is_g = lane < MOE_GROUPS
    gl = jnp.where(is_g, l, neg)
    gmax = jnp.max(gl, axis=1, keepdims=True)
    gstar = jnp.min(jnp.where(gl == gmax, lane, big), axis=1, keepdims=True)
    pg = 1.0 / jnp.sum(jnp.where(is_g, jnp.exp(gl - gmax), 0.0), axis=1, keepdims=True)
    lo = MOE_GROUPS + EXPERTS_PER_GROUP * gstar
    is_e = (lane >= lo) & (lane < lo + EXPERTS_PER_GROUP)
    el = jnp.where(is_e, l, neg)
    m1 = jnp.max(el, axis=1, keepdims=True)
    i1 = jnp.min(jnp.where(is_e & (el == m1), lane, big), axis=1, keepdims=True)
    is_e2 = is_e & (lane != i1)
    el2 = jnp.where(is_e2, l, neg)
    m2 = jnp.max(el2, axis=1, keepdims=True)
    i2 = jnp.min(jnp.where(is_e2 & (el2 == m2), lane, big), axis=1, keepdims=True)
    t = jnp.exp(m2 - m1)
    w1 = pg / (1.0 + t)
    w2 = pg * t / (1.0 + t)
    gate_ref[...] = jnp.where(lane == 0.0, w1, jnp.where(lane == 1.0, w2, 0.0))
    eid = jnp.where(lane == 0.0, i1, jnp.where(lane == 1.0, i2, float(MOE_GROUPS)))
    eid_ref[...] = (eid - MOE_GROUPS).astype(jnp.int32)


def route(logits, *, tm=1024):
    n, w = logits.shape
    tm = min(tm, n)
    return pl.pallas_call(
        _route_kernel,
        grid=(n // tm,),
        in_specs=[pl.BlockSpec((tm, w), lambda i: (i, 0))],
        out_specs=[pl.BlockSpec((tm, w), lambda i: (i, 0)),
                   pl.BlockSpec((tm, w), lambda i: (i, 0))],
        out_shape=[jax.ShapeDtypeStruct((n, w), F32), jax.ShapeDtypeStruct((n, w), jnp.int32)],
        compiler_params=_cparams(("parallel",)),
        name="route",
    )(logits)


def _expert_kernel(be_ref, qb_ref, nv_ref, nu_ref, ord_ref, h_hbm, nf_ref, wg_ref, wu_ref, wd_ref,
                   y_hbm, xbuf, vbuf, ybuf, wgb, wub, wdb, gsem, ssem, *, rows, ntok):
    i = pl.program_id(0)
    nused = nu_ref[0]
    slot = lax.rem(i, 2)
    qmax = ord_ref.shape[0] - rows

    def first_pos(blk):
        return jnp.clip(qb_ref[blk], 0, qmax)

    def gather_row(q0, r, sl):
        tok = ord_ref[q0 + r] >> 1
        return pltpu.make_async_copy(h_hbm.at[pl.ds(tok, 1), :], xbuf.at[sl, pl.ds(r, 1), :],
                                     gsem.at[sl])

    def scatter_row(q0, r, sl):
        a = ord_ref[q0 + r]
        orow = (a & 1) * ntok + (a >> 1)
        return pltpu.make_async_copy(ybuf.at[sl, pl.ds(r, 1), :], y_hbm.at[pl.ds(orow, 1), :],
                                     ssem.at[sl])

    def wait_gather(sl):
        pltpu.make_async_copy(h_hbm.at[pl.ds(0, rows), :], xbuf.at[sl], gsem.at[sl]).wait()

    def wait_scatter(nv, sl):
        p = 1
        while p <= rows:
            @pl.when((nv & p) != 0)
            def _(p=p):
                pltpu.make_async_copy(ybuf.at[sl, pl.ds(0, p), :], y_hbm.at[pl.ds(0, p), :],
                                      ssem.at[sl]).wait()
            p *= 2

    @pl.when(i == 0)
    def _():
        q0 = first_pos(0)

        def body(r, _):
            gather_row(q0, r, 0).start()
            return 0
        lax.fori_loop(0, rows, body, 0)

    @pl.when(i < nused)
    def _():
        e = be_ref[i]
        changed = jnp.logical_or(i == 0, e != be_ref[jnp.maximum(i - 1, 0)])

        @pl.when(changed)
        def _():
            wgb[...] = wg_ref[...].astype(BF16)
            wub[...] = wu_ref[...].astype(BF16)
            wdb[...] = wd_ref[...].astype(BF16)

        wait_gather(slot)
        vbuf[...] = _rms(xbuf[slot], nf_ref[...]).astype(BF16)
        wait_scatter(jnp.where(i >= 2, nv_ref[jnp.maximum(i - 2, 0)], 0), slot)

        q_next = first_pos(jnp.minimum(i + 1, pl.num_programs(0) - 1))
        for r in range(rows):
            gather_row(q_next, r, 1 - slot).start()
        prev = jnp.maximum(i - 1, 0)
        q_prev = first_pos(prev)
        nv_prev = jnp.where(i >= 1, nv_ref[prev], 0)
        for r in range(rows):
            @pl.when(r < nv_prev)
            def _(r=r):
                scatter_row(q_prev, r, 1 - slot).start()

        v = vbuf[...]
        hg = jnp.dot(v, wgb[...], preferred_element_type=F32)
        hu = jnp.dot(v, wub[...], preferred_element_type=F32)
        a = (hg * _sigmoid(hg) * hu).astype(BF16)
        ybuf[slot] = jnp.dot(a, wdb[...], preferred_element_type=F32)

        @pl.when(i == nused - 1)
        def _():
            q_here = first_pos(i)

            def body(r, _):
                scatter_row(q_here, r, slot).start()
                return 0
            lax.fori_loop(0, nv_ref[i], body, 0)
            wait_gather(1 - slot)
            wait_scatter(nv_prev, 1 - slot)
            wait_scatter(nv_ref[i], slot)


def experts(block_e, qbase, nvalid, nused, order, h1, nf, w_gate, w_up, w_down,
            *, rows=EXPERT_ROWS):
    n, d = h1.shape
    ne, _, ff = w_gate.shape
    nblk = block_e.shape[0]
    grid_spec = pltpu.PrefetchScalarGridSpec(
        num_scalar_prefetch=5,
        grid=(nblk,),
        in_specs=[
            pl.BlockSpec(memory_space=pl.ANY),
            pl.BlockSpec((1, d), lambda i, *_: (0, 0)),
            pl.BlockSpec((None, d, ff), lambda i, be, *_: (be[i], 0, 0)),
            pl.BlockSpec((None, d, ff), lambda i, be, *_: (be[i], 0, 0)),
            pl.BlockSpec((None, ff, d), lambda i, be, *_: (be[i], 0, 0)),
        ],
        out_specs=pl.BlockSpec(memory_space=pl.ANY),
        scratch_shapes=[
            pltpu.VMEM((2, rows, d), F32),
            pltpu.VMEM((rows, d), BF16),
            pltpu.VMEM((2, rows, d), F32),
            pltpu.VMEM((d, ff), BF16),
            pltpu.VMEM((d, ff), BF16),
            pltpu.VMEM((ff, d), BF16),
            pltpu.SemaphoreType.DMA((2,)),
            pltpu.SemaphoreType.DMA((2,)),
        ],
    )
    return pl.pallas_call(
        functools.partial(_expert_kernel, rows=rows, ntok=n),
        grid_spec=grid_spec,
        out_shape=jax.ShapeDtypeStruct((2 * n, d), F32),
        compiler_params=_cparams(("arbitrary",)),
        name="experts",
    )(block_e, qbase, nvalid, nused, order, h1, nf, w_gate, w_up, w_down)


def _dispatch_plan(eid, rows, n_experts):
    n = eid.shape[0]
    m = 2 * n
    e_flat = eid.reshape(m)
    order = jnp.pad(jnp.argsort(e_flat).astype(jnp.int32), (0, rows))
    counts = jnp.zeros((n_experts,), jnp.int32).at[e_flat].add(1)
    nblk_e = (counts + rows - 1) // rows
    bend = jnp.cumsum(nblk_e)
    bstart = bend - nblk_e
    start = jnp.cumsum(counts) - counts
    nblk = (m + n_experts * (rows - 1) + rows - 1) // rows
    bi = jnp.arange(nblk, dtype=jnp.int32)
    block_e = jnp.minimum(jnp.sum((bi[:, None] >= bend[None, :]).astype(jnp.int32), axis=1),
                          n_experts - 1)
    k = bi - bstart[block_e]
    qbase = (start[block_e] + k * rows).astype(jnp.int32)
    nvalid = jnp.clip(counts[block_e] - k * rows, 0, rows).astype(jnp.int32)
    nused = bend[-1:].astype(jnp.int32)
    return block_e, qbase, nvalid, nused, order


def _combine_kernel(h_ref, y0_ref, y1_ref, gate_ref, g_ref, o_ref):
    gate = gate_ref[...]
    h = h_ref[...] + gate[:, 0:1] * y0_ref[...] + gate[:, 1:2] * y1_ref[...]
    o_ref[...] = _rms(h, g_ref[...])


def combine(h1, y, gate, g, *, tm=512):
    n, d = h1.shape
    tm = min(tm, n)
    return pl.pallas_call(
        _combine_kernel,
        grid=(n // tm,),
        in_specs=[
            pl.BlockSpec((tm, d), lambda i: (i, 0)),
            pl.BlockSpec((None, tm, d), lambda i: (0, i, 0)),
            pl.BlockSpec((None, tm, d), lambda i: (1, i, 0)),
            pl.BlockSpec((tm, gate.shape[1]), lambda i: (i, 0)),
            pl.BlockSpec((1, d), lambda i: (0, 0)),
        ],
        out_specs=pl.BlockSpec((tm, d), lambda i: (i, 0)),
        out_shape=jax.ShapeDtypeStruct((n, d), F32),
        compiler_params=_cparams(("parallel",)),
        name="combine",
    )(h1, y, y, gate, g)


def kernel(x, meta, norm_mix, w_in, pool_w, pool_scale, ssm_a_re, ssm_a_im, ssm_log_dt, ssm_b_re,
           ssm_b_im, ssm_c_re, ssm_c_im, ssm_d, w_glu, b_glu, w_branch_pool, w_branch_ssm, w_out,
           norm_ffn, w_router_group, b_router_group, w_router_expert, b_router_expert, w_gate,
           w_up, w_down, norm_final):
    assert w_in.shape[0] == 1, "single layer"
    bsz, seq, d = x.shape
    n = bsz * seq
    q = pool_scale.shape[1]
    assert meta.shape[0] == N_META == SSM_CHUNK and N_META >= max(POOL_WINDOWS)
    nc = seq // SSM_CHUNK
    assert nc & (nc - 1) == 0 and nc % SSM_ROWS == 0

    xf = x.reshape(n, d)
    g_mix = norm_mix[0][None, :]
    w_in_b = w_in[0].astype(BF16)

    zps = rms_matmul(xf, g_mix, w_in_b, col0=0, cols=2 * q, act=False, out_dtype=F32)
    zps_meta = rms_matmul(meta.astype(F32), g_mix, w_in_b, col0=0, cols=2 * q, act=False,
                          out_dtype=F32)
    zps_meta = jnp.pad(zps_meta, ((0, SSM_ROWS * SSM_CHUNK - N_META), (0, 0)))
    gates = rms_matmul(xf, g_mix, w_in_b, col0=2 * q, cols=2 * d, act=True, out_dtype=BF16)

    pool_y = pool_mixer(zps.reshape(bsz, seq, 2 * q), zps_meta, pool_w[0].astype(BF16),
                        pool_scale[0][None, :]).reshape(n, q)

    m_op, p_op, q_op, ar_op, ai_op = _ssm_operators(
        ssm_a_re[0], ssm_a_im[0], ssm_log_dt[0], ssm_b_re[0], ssm_b_im[0], ssm_c_re[0],
        ssm_c_im[0], nc)
    ssm_y = ssm_chunked(zps, zps_meta, m_op, p_op, q_op, ar_op, ai_op, bsz=bsz, seq=seq, col0=q)

    n_groups = w_router_group.shape[2]
    n_exp = w_router_expert.shape[2]
    w_r = jnp.concatenate([w_router_group[0], w_router_expert[0]], axis=1)
    w_r = jnp.pad(w_r, ((0, 0), (0, ROUTER_LANES - n_groups - n_exp))).astype(BF16)
    b_r = jnp.concatenate([b_router_group[0], b_router_expert[0]])
    b_r = jnp.pad(b_r, (0, ROUTER_LANES - n_groups - n_exp))[None, :].astype(F32)
    h1, logits = merge(pool_y, ssm_y, zps, gates, xf, ssm_d[0][None, :], w_glu[0].astype(BF16),
                       b_glu[0][None, :], w_branch_pool[0].astype(BF16),
                       w_branch_ssm[0].astype(BF16), w_out[0].astype(BF16),
                       norm_ffn[0][None, :], w_r, b_r)

    gate, eid = route(logits)
    block_e, qbase, nvalid, nused, order = _dispatch_plan(eid[:, :2], EXPERT_ROWS, n_exp)
    y = experts(block_e, qbase, nvalid, nused, order, h1, norm_ffn[0][None, :], w_gate[0],
                w_up[0], w_down[0])
    out = combine(h1, y.reshape(2, n, d), gate, norm_final[None, :])
    return out.reshape(bsz, seq, d)
```

```python
import functools
import math

import jax
import jax.numpy as jnp
from jax import lax
from jax.experimental import pallas as pl
from jax.experimental.pallas import tpu as pltpu

F32 = jnp.float32
BF16 = jnp.bfloat16

RMS_EPS = 1e-6
POOL_WINDOWS = (2, 4, 8, 16)
N_META = 16
LANES = 128
SSM_CHUNK = 16
SSM_GROUP_CH = 16
SSM_STATE = 64
SSM_OCTET = LANES // SSM_GROUP_CH
SSM_ROWS = 16
SSM_UNROLL = 4
MOE_GROUPS = 8
EXPERTS_PER_GROUP = 8
ROUTER_LANES = LANES
EXPERT_ROWS = 256
VMEM_LIMIT = 56 * 1024 * 1024


def _cparams(sem, vmem=VMEM_LIMIT):
    return pltpu.CompilerParams(dimension_semantics=sem, vmem_limit_bytes=vmem)


def _sigmoid(x):
    return 1.0 / (1.0 + jnp.exp(-x))


def _gelu_tanh(x):
    c = math.sqrt(2.0 / math.pi)
    return 0.5 * x * (1.0 + jnp.tanh(c * (x + 0.044715 * (x * x * x))))


def _rms(x, g):
    ms = jnp.mean(x * x, axis=-1, keepdims=True)
    return x * lax.rsqrt(ms + RMS_EPS) * g


def _rms_matmul_kernel(x_ref, g_ref, w_ref, o_ref, u_ref, *, act):
    @pl.when(pl.program_id(1) == 0)
    def _():
        u_ref[...] = _rms(x_ref[...], g_ref[...]).astype(BF16)

    acc = jnp.dot(u_ref[...], w_ref[...], preferred_element_type=F32)
    if act:
        acc = _sigmoid(acc)
    o_ref[...] = acc.astype(o_ref.dtype)


def rms_matmul(x, g, w, *, col0, cols, act, out_dtype, tm=1024, tn=1024):
    n, d = x.shape
    tm = min(tm, n)
    cb = col0 // tn
    return pl.pallas_call(
        functools.partial(_rms_matmul_kernel, act=act),
        grid=(n // tm, cols // tn),
        in_specs=[
            pl.BlockSpec((tm, d), lambda i, j: (i, 0)),
            pl.BlockSpec((1, d), lambda i, j: (0, 0)),
            pl.BlockSpec((d, tn), lambda i, j: (0, j + cb)),
        ],
        out_specs=pl.BlockSpec((tm, tn), lambda i, j: (i, j)),
        out_shape=jax.ShapeDtypeStruct((n, cols), out_dtype),
        scratch_shapes=[pltpu.VMEM((tm, d), BF16)],
        compiler_params=_cparams(("parallel", "arbitrary")),
        name="rms_matmul",
    )(x, g, w)


def _pool_kernel(z_ref, zm_ref, w_ref, sc_ref, o_ref, ext_ref, *, tl, gw):
    h = N_META

    @pl.when(pl.program_id(1) == 0)
    def _():
        ext_ref[0:h, :] = zm_ref[...]

    ext_ref[h:h + tl, :] = z_ref[...]
    for g, w in enumerate(POOL_WINDOWS):
        cols = slice(g * gw, (g + 1) * gw)
        u = ext_ref[h:h + tl, cols]
        s = u
        for k in range(1, w):
            s = s + ext_ref[h - k:h - k + tl, cols]
        d = s * (1.0 / w) - u
        y = jnp.dot(d.astype(BF16), w_ref[g], preferred_element_type=F32)
        o_ref[:, cols] = (y * sc_ref[:, cols]).astype(o_ref.dtype)
    ext_ref[0:h, :] = ext_ref[tl:tl + h, :]


def pool_mixer(z, zmeta, pool_w, pool_scale, *, tl=512):
    b, l, _ = z.shape
    ng, gw, _ = pool_w.shape
    c = ng * gw
    tl = min(tl, l)
    return pl.pallas_call(
        functools.partial(_pool_kernel, tl=tl, gw=gw),
        grid=(b, l // tl),
        in_specs=[
            pl.BlockSpec((None, tl, c), lambda i, t: (i, t, 0)),
            pl.BlockSpec((N_META, c), lambda i, t: (0, 0)),
            pl.BlockSpec((ng, gw, gw), lambda i, t: (0, 0, 0)),
            pl.BlockSpec((1, c), lambda i, t: (0, 0)),
        ],
        out_specs=pl.BlockSpec((None, tl, c), lambda i, t: (i, t, 0)),
        out_shape=jax.ShapeDtypeStruct((b, l, c), BF16),
        scratch_shapes=[pltpu.VMEM((tl + N_META, c), F32)],
        compiler_params=_cparams(("arbitrary", "arbitrary")),
        name="pool_mixer",
    )(z, zmeta, pool_w, pool_scale)


def _ssm_toeplitz_kernel(bt_ref, cl_ref, m_ref):
    w = m_ref.shape[-1]
    lane = lax.broadcasted_iota(jnp.int32, (SSM_GROUP_CH, w), 1)
    for g in range(bt_ref.shape[0]):
        kt = jnp.dot(bt_ref[g], cl_ref[g], preferred_element_type=F32,
                     precision=lax.Precision.HIGHEST)
        for s in range(SSM_CHUNK):
            off = s * SSM_GROUP_CH
            blk = kt if s == 0 else jnp.where(lane >= off, pltpu.roll(kt, off, axis=1), 0.0)
            m_ref[g, off:off + SSM_GROUP_CH, :] = blk.astype(m_ref.dtype)


def ssm_toeplitz(bt, cl):
    g, hh, k2 = bt.shape
    w = cl.shape[2]
    gb = SSM_OCTET
    return pl.pallas_call(
        _ssm_toeplitz_kernel,
        grid=(g // gb,),
        in_specs=[pl.BlockSpec((gb, hh, k2), lambda i: (i, 0, 0)),
                  pl.BlockSpec((gb, k2, w), lambda i: (i, 0, 0))],
        out_specs=pl.BlockSpec((gb, w, w), lambda i: (i, 0, 0)),
        out_shape=jax.ShapeDtypeStruct((g, w, w), BF16),
        compiler_params=_cparams(("parallel",)),
        name="ssm_toeplitz",
    )(bt, cl)


def _block_transpose8(a, lane):
    for d in (4, 2, 1):
        w = d * SSM_GROUP_CH
        keep = (lane & w) == 0
        b = list(a)
        for k in range(SSM_OCTET):
            if k & d == 0:
                x, y = a[k], a[k + d]
                b[k] = jnp.where(keep, x, pltpu.roll(y, w, axis=1))
                b[k + d] = jnp.where(keep, pltpu.roll(x, LANES - w, axis=1), y)
        a = b
    return a


def _ssm_kernel(z_ref, zm_ref, m_ref, p_ref, q_ref, ar_ref, ai_ref, o_ref, uscr, umscr, yscr,
                *, nc, nsteps):
    rt = SSM_ROWS
    t = SSM_CHUNK
    lane = lax.broadcasted_iota(jnp.int32, (rt, LANES), 1)

    def regroup(src_ref, dst_ref, it):
        base = pl.multiple_of(it * (rt * t), rt * t)
        row0 = pl.multiple_of(it * rt, rt)
        for half in range(2):
            a = [src_ref[pl.ds(base + half * SSM_OCTET + k, rt, stride=t), :]
                 for k in range(SSM_OCTET)]
            b = _block_transpose8(a, lane)
            for j in range(SSM_OCTET):
                dst_ref[j, pl.ds(row0, rt), half * LANES:(half + 1) * LANES] = b[j].astype(BF16)

    regroup(zm_ref, umscr, 0)

    def fwd(it, _):
        regroup(z_ref, uscr, it)
        return 0
    lax.fori_loop(0, nc // rt, fwd, 0, unroll=SSM_UNROLL)

    c = lax.broadcasted_iota(jnp.int32, (nc, 2 * SSM_STATE), 0)

    def per_group(j, _):
        u = uscr[j]
        p = p_ref[j]
        y_intra = jnp.dot(u, m_ref[j], preferred_element_type=F32)
        s = jnp.dot(u, p, preferred_element_type=F32)
        x0 = jnp.dot(umscr[j], p, preferred_element_type=F32)[0:1]
        x = jnp.where(c == 0, x0, pltpu.roll(s, 1, axis=0))
        d = 1
        for k in range(nsteps):
            z = jnp.where(c >= d, pltpu.roll(x, d, axis=0), 0.0)
            x = (x + ar_ref[j, k:k + 1, :] * z
                 + ai_ref[j, k:k + 1, :] * pltpu.roll(z, SSM_STATE, axis=1))
            d *= 2
        yscr[j] = y_intra + jnp.dot(x.astype(BF16), q_ref[j], preferred_element_type=F32)
        return 0
    lax.fori_loop(0, SSM_OCTET, per_group, 0)

    def bwd(it, _):
        base = pl.multiple_of(it * (rt * t), rt * t)
        row0 = pl.multiple_of(it * rt, rt)
        for half in range(2):
            b = [yscr[j, pl.ds(row0, rt), half * LANES:(half + 1) * LANES]
                 for j in range(SSM_OCTET)]
            a = _block_transpose8(b, lane)
            for k in range(SSM_OCTET):
                o_ref[pl.ds(base + half * SSM_OCTET + k, rt, stride=t), :] = a[k]
        return 0
    lax.fori_loop(0, nc // rt, bwd, 0, unroll=SSM_UNROLL)


def ssm_chunked(z, zm, m, p, q, ar, ai, *, bsz, seq, col0):
    g, w, _ = m.shape
    ns2 = p.shape[2]
    nsteps = ar.shape[1]
    nc = seq // SSM_CHUNK
    gb = SSM_OCTET
    cb = col0 // LANES
    return pl.pallas_call(
        functools.partial(_ssm_kernel, nc=nc, nsteps=nsteps),
        grid=(g // gb, bsz),
        in_specs=[
            pl.BlockSpec((seq, LANES), lambda j, b: (b, cb + j)),
            pl.BlockSpec((SSM_ROWS * SSM_CHUNK, LANES), lambda j, b: (0, cb + j)),
            pl.BlockSpec((gb, w, w), lambda j, b: (j, 0, 0)),
            pl.BlockSpec((gb, w, ns2), lambda j, b: (j, 0, 0)),
            pl.BlockSpec((gb, ns2, w), lambda j, b: (j, 0, 0)),
            pl.BlockSpec((gb, nsteps, ns2), lambda j, b: (j, 0, 0)),
            pl.BlockSpec((gb, nsteps, ns2), lambda j, b: (j, 0, 0)),
        ],
        out_specs=pl.BlockSpec((seq, LANES), lambda j, b: (b, j)),
        out_shape=jax.ShapeDtypeStruct((bsz * seq, g * SSM_GROUP_CH), F32),
        scratch_shapes=[
            pltpu.VMEM((gb, nc, w), BF16),
            pltpu.VMEM((gb, SSM_ROWS, w), BF16),
            pltpu.VMEM((gb, nc, w), F32),
        ],
        compiler_params=_cparams(("parallel", "arbitrary")),
        name="ssm_chunked",
    )(z, zm, m, p, q, ar, ai)


def _ssm_operators(a_re, a_im, log_dt, b_re, b_im, c_re, c_im, nc):
    t = SSM_CHUNK
    ar = a_re.astype(F32)
    ai = a_im.astype(F32)
    dt = jnp.exp(log_dt.astype(F32))[:, None]
    mag = jnp.exp(ar * dt)
    lam_re = mag * jnp.cos(ai * dt)
    lam_im = mag * jnp.sin(ai * dt)
    den = ar * ar + ai * ai
    nr = lam_re - 1.0
    coef_re = (nr * ar + lam_im * ai) / den
    coef_im = (lam_im * ar - nr * ai) / den
    br = b_re.astype(F32)
    bi = b_im.astype(F32)
    bb_re = coef_re[..., None] * br - coef_im[..., None] * bi
    bb_im = coef_re[..., None] * bi + coef_im[..., None] * br
    crt = c_re.astype(F32).transpose(0, 2, 1)
    cit = c_im.astype(F32).transpose(0, 2, 1)

    def lam_pow(k):
        k = k.astype(F32)
        m = jnp.exp(ar[..., None] * dt[..., None] * k)
        ang = ai[..., None] * dt[..., None] * k
        return m * jnp.cos(ang), m * jnp.sin(ang)

    g = ar.shape[0]
    hh = SSM_GROUP_CH
    pr, pi = lam_pow(jnp.arange(t + 1))

    def times_c(xr, xi):
        return (xr[..., None] * crt[:, :, None, :] - xi[..., None] * cit[:, :, None, :],
                xr[..., None] * cit[:, :, None, :] + xi[..., None] * crt[:, :, None, :])

    cl_re, cl_im = times_c(pr[:, :, :t], pi[:, :, :t])
    cl = jnp.concatenate([cl_re.reshape(g, -1, t * hh), cl_im.reshape(g, -1, t * hh)], axis=1)
    bt = jnp.concatenate([bb_re.transpose(0, 2, 1), -bb_im.transpose(0, 2, 1)], axis=-1)
    m_op = ssm_toeplitz(bt, cl)
    rr = pr[:, :, t - 1::-1][:, :, :t]
    ri = pi[:, :, t - 1::-1][:, :, :t]
    p_re = rr[..., None] * bb_re[:, :, None, :] - ri[..., None] * bb_im[:, :, None, :]
    p_im = rr[..., None] * bb_im[:, :, None, :] + ri[..., None] * bb_re[:, :, None, :]
    p_op = jnp.concatenate([p_re.transpose(0, 2, 3, 1).reshape(g, t * hh, -1),
                            p_im.transpose(0, 2, 3, 1).reshape(g, t * hh, -1)], axis=-1)
    q_re, q_im = times_c(pr[:, :, 1:], pi[:, :, 1:])
    q_op = jnp.concatenate([q_re.reshape(g, -1, t * hh), -q_im.reshape(g, -1, t * hh)], axis=1)
    nsteps = int(math.log2(nc))
    sr, si = lam_pow(t * (2 ** jnp.arange(nsteps)))
    sr = sr.transpose(0, 2, 1)
    si = si.transpose(0, 2, 1)
    ar_op = jnp.concatenate([sr, sr], axis=-1)
    ai_op = jnp.concatenate([-si, si], axis=-1)
    return m_op, p_op.astype(BF16), q_op.astype(BF16), ar_op, ai_op


def _merge_kernel(p_ref, y_ref, us_ref, gp_ref, gs_ref, h_ref, dv_ref, wglu_ref, bglu_ref,
                  wbp_ref, wbs_ref, wout_ref, nf_ref, wr_ref, br_ref, h1_ref, lg_ref):
    s = _gelu_tanh(y_ref[...] + dv_ref[...] * us_ref[...])
    gl = jnp.dot(s.astype(BF16), wglu_ref[...], preferred_element_type=F32) + bglu_ref[...]
    s = s * _sigmoid(gl)
    y_ssm = jnp.dot(s.astype(BF16), wbs_ref[...], preferred_element_type=F32)
    y_pool = jnp.dot(p_ref[...], wbp_ref[...], preferred_element_type=F32)
    merged = gp_ref[...].astype(F32) * y_pool + gs_ref[...].astype(F32) * y_ssm
    h1 = h_ref[...] + jnp.dot(merged.astype(BF16), wout_ref[...], preferred_element_type=F32)
    h1_ref[...] = h1
    v = _rms(h1, nf_ref[...])
    lg_ref[...] = jnp.dot(v.astype(BF16), wr_ref[...], preferred_element_type=F32) + br_ref[...]


def merge(pool_y, ssm_y, zps, gates, h0, dvec, w_glu, b_glu, w_bp, w_bs, w_out, nf, w_r, b_r,
          *, tm=256):
    n, d = h0.shape
    q = pool_y.shape[1]
    tm = min(tm, n)
    const = lambda shape: pl.BlockSpec(shape, lambda i: (0,) * len(shape),
                                       pipeline_mode=pl.Buffered(1))
    return pl.pallas_call(
        _merge_kernel,
        grid=(n // tm,),
        in_specs=[
            pl.BlockSpec((tm, q), lambda i: (i, 0)),
            pl.BlockSpec((tm, q), lambda i: (i, 0)),
            pl.BlockSpec((tm, q), lambda i: (i, 1)),
            pl.BlockSpec((tm, d), lambda i: (i, 0)),
            pl.BlockSpec((tm, d), lambda i: (i, 1)),
            pl.BlockSpec((tm, d), lambda i: (i, 0)),
            const((1, q)), const((q, q)), const((1, q)),
            const((q, d)), const((q, d)), const((d, d)),
            const((1, d)), const((d, ROUTER_LANES)), const((1, ROUTER_LANES)),
        ],
        out_specs=[
            pl.BlockSpec((tm, d), lambda i: (i, 0)),
            pl.BlockSpec((tm, ROUTER_LANES), lambda i: (i, 0)),
        ],
        out_shape=[
            jax.ShapeDtypeStruct((n, d), F32),
            jax.ShapeDtypeStruct((n, ROUTER_LANES), F32),
        ],
        compiler_params=_cparams(("parallel",)),
        name="merge",
    )(pool_y, ssm_y, zps, gates, gates, h0, dvec, w_glu, b_glu, w_bp, w_bs, w_out, nf, w_r, b_r)


def _route_kernel(lg_ref, gate_ref, eid_ref, rank_ref, cnt_ref, tri_ref, carry_ref):
    step = pl.program_id(0)
    tm = lg_ref.shape[0]

    @pl.when(step == 0)
    def _():
        rr = lax.broadcasted_iota(jnp.int32, (tm, tm), 0)
        cc = lax.broadcasted_iota(jnp.int32, (tm, tm), 1)
        tri_ref[...] = jnp.where(cc < rr, 1.0, 0.0).astype(BF16)
        carry_ref[...] = jnp.zeros_like(carry_ref)

    l = lg_ref[...]
    lane = lax.broadcasted_iota(jnp.int32, l.shape, 1).astype(F32)
    neg = -jnp.inf
    big = float(ROUTER_LANES)
    is_g = lane < MOE_GROUPS
    gl = jnp.where(is_g, l, neg)
    gmax = jnp.max(gl, axis=1, keepdims=True)
    gstar = jnp.min(jnp.where(gl == gmax, lane, big), axis=1, keepdims=True)
    pg = 1.0 / jnp.sum(jnp.where(is_g, jnp.exp(gl - gmax), 0.0), axis=1, keepdims=True)
    lo = MOE_GROUPS + EXPERTS_PER_GROUP * gstar
    is_e = (lane >= lo) & (lane < lo + EXPERTS_PER_GROUP)
    el = jnp.where(is_e, l, neg)
    m1 = jnp.max(el, axis=1, keepdims=True)
    i1 = jnp.min(jnp.where(is_e & (el == m1), lane, big), axis=1, keepdims=True)
    is_e2 = is_e & (lane != i1)
    el2 = jnp.where(is_e2, l, neg)
    m2 = jnp.max(el2, axis=1, keepdims=True)
    i2 = jnp.min(jnp.where(is_e2 & (el2 == m2), lane, big), axis=1, keepdims=True)
    t = jnp.exp(m2 - m1)
    w1 = pg / (1.0 + t)
    w2 = pg * t / (1.0 + t)
    gate_ref[...] = jnp.where(lane == 0.0, w1, jnp.where(lane == 1.0, w2, 0.0))
    eid = jnp.where(lane == 0.0, i1, jnp.where(lane == 1.0, i2, float(MOE_GROUPS)))
    eid_ref[...] = (eid - MOE_GROUPS).astype(jnp.int32)
    oh1 = lane == i1
    oh2 = lane == i2
    both = jnp.where(oh1 | oh2, 1.0, 0.0)
    before = carry_ref[...] + jnp.dot(tri_ref[...], both.astype(BF16),
                                      preferred_element_type=F32)
    r1 = jnp.sum(jnp.where(oh1, before, 0.0), axis=1, keepdims=True)
    r2 = jnp.sum(jnp.where(oh2, before, 0.0), axis=1, keepdims=True)
    rank_ref[...] = jnp.where(lane == 0.0, r1, jnp.where(lane == 1.0, r2, 0.0)).astype(jnp.int32)
    carry_ref[...] = carry_ref[...] + jnp.sum(both, axis=0, keepdims=True)
    cnt_ref[...] = carry_ref[...]


def route(logits, *, tm=1024):
    n, w = logits.shape
    tm = min(tm, n)
    tok = pl.BlockSpec((tm, w), lambda i: (i, 0))
    return pl.pallas_call(
        _route_kernel,
        grid=(n // tm,),
        in_specs=[tok],
        out_specs=[tok, tok, tok, pl.BlockSpec((1, w), lambda i: (0, 0))],
        out_shape=[jax.ShapeDtypeStruct((n, w), F32), jax.ShapeDtypeStruct((n, w), jnp.int32),
                   jax.ShapeDtypeStruct((n, w), jnp.int32), jax.ShapeDtypeStruct((1, w), F32)],
        scratch_shapes=[pltpu.VMEM((tm, tm), BF16), pltpu.VMEM((1, w), F32)],
        compiler_params=_cparams(("arbitrary",)),
        name="route",
    )(logits)


def _expert_kernel(be_ref, qb_ref, nu_ref, ord_ref, h_hbm, nf_ref, wg_ref, wu_ref, wd_ref,
                   o_ref, xbuf, vbuf, wgb, wub, wdb, gsem, *, rows):
    i = pl.program_id(0)
    nused = nu_ref[0]
    slot = lax.rem(i, 2)
    qmax = ord_ref.shape[0] - rows

    def first_pos(blk):
        return jnp.clip(qb_ref[blk], 0, qmax)

    def gather_row(q0, r, sl):
        tok = ord_ref[q0 + r] >> 1
        return pltpu.make_async_copy(h_hbm.at[pl.ds(tok, 1), :], xbuf.at[sl, pl.ds(r, 1), :],
                                     gsem.at[sl])

    def wait_gather(sl):
        pltpu.make_async_copy(h_hbm.at[pl.ds(0, rows), :], xbuf.at[sl], gsem.at[sl]).wait()

    @pl.when(i == 0)
    def _():
        q0 = first_pos(0)

        def body(r, _):
            gather_row(q0, r, 0).start()
            return 0
        lax.fori_loop(0, rows, body, 0)

    @pl.when(i < nused)
    def _():
        e = be_ref[i]
        changed = jnp.logical_or(i == 0, e != be_ref[jnp.maximum(i - 1, 0)])

        @pl.when(changed)
        def _():
            wgb[...] = wg_ref[...].astype(BF16)
            wub[...] = wu_ref[...].astype(BF16)
            wdb[...] = wd_ref[...].astype(BF16)

        wait_gather(slot)
        vbuf[...] = _rms(xbuf[slot], nf_ref[...]).astype(BF16)

        q_next = first_pos(jnp.minimum(i + 1, pl.num_programs(0) - 1))
        for r in range(rows):
            gather_row(q_next, r, 1 - slot).start()

        v = vbuf[...]
        hg = jnp.dot(v, wgb[...], preferred_element_type=F32)
        hu = jnp.dot(v, wub[...], preferred_element_type=F32)
        a = (hg * _sigmoid(hg) * hu).astype(BF16)
        o_ref[...] = jnp.dot(a, wdb[...], preferred_element_type=F32)

        @pl.when(i == nused - 1)
        def _():
            wait_gather(1 - slot)

    @pl.when(i == nused)
    def _():
        o_ref[...] = jnp.zeros_like(o_ref)


def experts(block_e, qbase, nused, order, h1, nf, w_gate, w_up, w_down, *, rows=EXPERT_ROWS):
    n, d = h1.shape
    ne, _, ff = w_gate.shape
    nblk = block_e.shape[0]
    grid_spec = pltpu.PrefetchScalarGridSpec(
        num_scalar_prefetch=4,
        grid=(nblk,),
        in_specs=[
            pl.BlockSpec(memory_space=pl.ANY),
            pl.BlockSpec((1, d), lambda i, *_: (0, 0)),
            pl.BlockSpec((None, d, ff), lambda i, be, *_: (be[i], 0, 0)),
            pl.BlockSpec((None, d, ff), lambda i, be, *_: (be[i], 0, 0)),
            pl.BlockSpec((None, ff, d), lambda i, be, *_: (be[i], 0, 0)),
        ],
        out_specs=pl.BlockSpec(
            (rows, d), lambda i, be, qb, nu, od: (jnp.where(i < nu[0], i, nblk - 1), 0)),
        scratch_shapes=[
            pltpu.VMEM((2, rows, d), F32),
            pltpu.VMEM((rows, d), BF16),
            pltpu.VMEM((d, ff), BF16),
            pltpu.VMEM((d, ff), BF16),
            pltpu.VMEM((ff, d), BF16),
            pltpu.SemaphoreType.DMA((2,)),
        ],
    )
    return pl.pallas_call(
        functools.partial(_expert_kernel, rows=rows),
        grid_spec=grid_spec,
        out_shape=jax.ShapeDtypeStruct((nblk * rows, d), F32),
        compiler_params=_cparams(("arbitrary",)),
        name="experts",
    )(block_e, qbase, nused, order, h1, nf, w_gate, w_up, w_down)


def _dispatch_plan(eid, rank, counts, rows):
    n_experts = counts.shape[0]
    m = eid.size
    e_flat = eid.reshape(m)
    order = jnp.pad(jnp.argsort(e_flat).astype(jnp.int32), (0, rows))
    nblk_e = (counts + rows - 1) // rows
    bend = jnp.cumsum(nblk_e)
    bstart = bend - nblk_e
    start = jnp.cumsum(counts) - counts
    nblk = (m + n_experts * (rows - 1) + rows - 1) // rows
    bi = jnp.arange(nblk, dtype=jnp.int32)
    block_e = jnp.minimum(jnp.sum((bi[:, None] >= bend[None, :]).astype(jnp.int32), axis=1),
                          n_experts - 1)
    qbase = (start[block_e] + (bi - bstart[block_e]) * rows).astype(jnp.int32)
    nused = bend[-1:].astype(jnp.int32)
    dest = (bstart[e_flat] * rows + rank.reshape(m)).astype(jnp.int32)
    return block_e, qbase, nused, order, dest


def _combine_kernel(dest_ref, h_ref, y_hbm, gate_ref, g_ref, o_ref, ybuf, sem, *, tm):
    i = pl.program_id(0)
    slot = lax.rem(i, 2)

    def row_copy(tile, t, k, sl):
        row = dest_ref[(tile * tm + t) * 2 + k]
        return pltpu.make_async_copy(y_hbm.at[pl.ds(row, 1), :], ybuf.at[sl, k, pl.ds(t, 1), :],
                                     sem.at[sl])

    def wait_rows(sl):
        for k in range(2):
            pltpu.make_async_copy(y_hbm.at[pl.ds(0, tm), :], ybuf.at[sl, k], sem.at[sl]).wait()

    @pl.when(i == 0)
    def _():
        def body(t, _):
            row_copy(0, t, 0, 0).start()
            row_copy(0, t, 1, 0).start()
            return 0
        lax.fori_loop(0, tm, body, 0)

    wait_rows(slot)
    nxt = jnp.minimum(i + 1, pl.num_programs(0) - 1)
    for t in range(tm):
        for k in range(2):
            row_copy(nxt, t, k, 1 - slot).start()
    gate = gate_ref[...]
    h = h_ref[...] + gate[:, 0:1] * ybuf[slot, 0] + gate[:, 1:2] * ybuf[slot, 1]
    o_ref[...] = _rms(h, g_ref[...])

    @pl.when(i == pl.num_programs(0) - 1)
    def _():
        wait_rows(1 - slot)


def combine(dest, h1, y, gate, g, *, tm=256):
    n, d = h1.shape
    tm = min(tm, n)
    grid_spec = pltpu.PrefetchScalarGridSpec(
        num_scalar_prefetch=1,
        grid=(n // tm,),
        in_specs=[
            pl.BlockSpec((tm, d), lambda i, ds: (i, 0)),
            pl.BlockSpec(memory_space=pl.ANY),
            pl.BlockSpec((tm, gate.shape[1]), lambda i, ds: (i, 0)),
            pl.BlockSpec((1, d), lambda i, ds: (0, 0)),
        ],
        out_specs=pl.BlockSpec((tm, d), lambda i, ds: (i, 0)),
        scratch_shapes=[pltpu.VMEM((2, 2, tm, d), F32), pltpu.SemaphoreType.DMA((2,))],
    )
    return pl.pallas_call(
        functools.partial(_combine_kernel, tm=tm),
        grid_spec=grid_spec,
        out_shape=jax.ShapeDtypeStruct((n, d), F32),
        compiler_params=_cparams(("arbitrary",)),
        name="combine",
    )(dest, h1, y, gate, g)


def kernel(x, meta, norm_mix, w_in, pool_w, pool_scale, ssm_a_re, ssm_a_im, ssm_log_dt, ssm_b_re,
           ssm_b_im, ssm_c_re, ssm_c_im, ssm_d, w_glu, b_glu, w_branch_pool, w_branch_ssm, w_out,
           norm_ffn, w_router_group, b_router_group, w_router_expert, b_router_expert, w_gate,
           w_up, w_down, norm_final):
    assert w_in.shape[0] == 1, "single layer"
    bsz, seq, d = x.shape
    n = bsz * seq
    q = pool_scale.shape[1]
    assert meta.shape[0] == N_META == SSM_CHUNK and N_META >= max(POOL_WINDOWS)
    nc = seq // SSM_CHUNK
    assert nc & (nc - 1) == 0 and nc % SSM_ROWS == 0

    xf = x.reshape(n, d)
    g_mix = norm_mix[0][None, :]
    w_in_b = w_in[0].astype(BF16)

    zps = rms_matmul(xf, g_mix, w_in_b, col0=0, cols=2 * q, act=False, out_dtype=F32)
    zps_meta = rms_matmul(meta.astype(F32), g_mix, w_in_b, col0=0, cols=2 * q, act=False,
                          out_dtype=F32)
    zps_meta = jnp.pad(zps_meta, ((0, SSM_ROWS * SSM_CHUNK - N_META), (0, 0)))
    gates = rms_matmul(xf, g_mix, w_in_b, col0=2 * q, cols=2 * d, act=True, out_dtype=BF16)

    pool_y = pool_mixer(zps.reshape(bsz, seq, 2 * q), zps_meta, pool_w[0].astype(BF16),
                        pool_scale[0][None, :]).reshape(n, q)

    m_op, p_op, q_op, ar_op, ai_op = _ssm_operators(
        ssm_a_re[0], ssm_a_im[0], ssm_log_dt[0], ssm_b_re[0], ssm_b_im[0], ssm_c_re[0],
        ssm_c_im[0], nc)
    ssm_y = ssm_chunked(zps, zps_meta, m_op, p_op, q_op, ar_op, ai_op, bsz=bsz, seq=seq, col0=q)

    n_groups = w_router_group.shape[2]
    n_exp = w_router_expert.shape[2]
    w_r = jnp.concatenate([w_router_group[0], w_router_expert[0]], axis=1)
    w_r = jnp.pad(w_r, ((0, 0), (0, ROUTER_LANES - n_groups - n_exp))).astype(BF16)
    b_r = jnp.concatenate([b_router_group[0], b_router_expert[0]])
    b_r = jnp.pad(b_r, (0, ROUTER_LANES - n_groups - n_exp))[None, :].astype(F32)
    h1, logits = merge(pool_y, ssm_y, zps, gates, xf, ssm_d[0][None, :], w_glu[0].astype(BF16),
                       b_glu[0][None, :], w_branch_pool[0].astype(BF16),
                       w_branch_ssm[0].astype(BF16), w_out[0].astype(BF16),
                       norm_ffn[0][None, :], w_r, b_r)

    gate, eid, rank, cnt = route(logits)
    counts = cnt[0, n_groups:n_groups + n_exp].astype(jnp.int32)
    block_e, qbase, nused, order, dest = _dispatch_plan(eid[:, :2], rank[:, :2], counts,
                                                        EXPERT_ROWS)
    y = experts(block_e, qbase, nused, order, h1, norm_ffn[0][None, :], w_gate[0], w_up[0],
                w_down[0])
    out = combine(dest, h1, y, gate, norm_final[None, :])
    return out.reshape(bsz, seq, d)
```

```python
import functools
import math

import jax
import jax.numpy as jnp
from jax import lax
from jax.experimental import pallas as pl
from jax.experimental.pallas import tpu as pltpu

F32 = jnp.float32
BF16 = jnp.bfloat16

RMS_EPS = 1e-6
POOL_WINDOWS = (2, 4, 8, 16)
N_META = 16
LANES = 128
SSM_CHUNK = 16
SSM_GROUP_CH = 16
SSM_STATE = 64
SSM_OCTET = LANES // SSM_GROUP_CH
SSM_ROWS = 16
SSM_UNROLL = 4
MOE_GROUPS = 8
EXPERTS_PER_GROUP = 8
ROUTER_LANES = LANES
EXPERT_ROWS = 256
VMEM_LIMIT = 56 * 1024 * 1024


def _cparams(sem, vmem=VMEM_LIMIT):
    return pltpu.CompilerParams(dimension_semantics=sem, vmem_limit_bytes=vmem)


def _sigmoid(x):
    return 1.0 / (1.0 + jnp.exp(-x))


def _gelu_tanh(x):
    c = math.sqrt(2.0 / math.pi)
    return 0.5 * x * (1.0 + jnp.tanh(c * (x + 0.044715 * (x * x * x))))


def _rms(x, g):
    ms = jnp.mean(x * x, axis=-1, keepdims=True)
    return x * lax.rsqrt(ms + RMS_EPS) * g


def _rms_matmul_kernel(x_ref, g_ref, w_ref, o_ref, u_ref, *, act):
    @pl.when(pl.program_id(1) == 0)
    def _():
        u_ref[...] = _rms(x_ref[...], g_ref[...]).astype(BF16)

    acc = jnp.dot(u_ref[...], w_ref[...], preferred_element_type=F32)
    if act:
        acc = _sigmoid(acc)
    o_ref[...] = acc.astype(o_ref.dtype)


def rms_matmul(x, g, w, *, col0, cols, act, out_dtype, tm=1024, tn=1024):
    n, d = x.shape
    tm = min(tm, n)
    cb = col0 // tn
    return pl.pallas_call(
        functools.partial(_rms_matmul_kernel, act=act),
        grid=(n // tm, cols // tn),
        in_specs=[
            pl.BlockSpec((tm, d), lambda i, j: (i, 0)),
            pl.BlockSpec((1, d), lambda i, j: (0, 0)),
            pl.BlockSpec((d, tn), lambda i, j: (0, j + cb)),
        ],
        out_specs=pl.BlockSpec((tm, tn), lambda i, j: (i, j)),
        out_shape=jax.ShapeDtypeStruct((n, cols), out_dtype),
        scratch_shapes=[pltpu.VMEM((tm, d), BF16)],
        compiler_params=_cparams(("parallel", "arbitrary")),
        name="rms_matmul",
    )(x, g, w)


def _pool_kernel(z_ref, zm_ref, w_ref, sc_ref, o_ref, ext_ref, *, tl, gw):
    h = N_META

    @pl.when(pl.program_id(1) == 0)
    def _():
        ext_ref[0:h, :] = zm_ref[...]

    ext_ref[h:h + tl, :] = z_ref[...]
    for g, w in enumerate(POOL_WINDOWS):
        cols = slice(g * gw, (g + 1) * gw)
        u = ext_ref[h:h + tl, cols]
        s = u
        for k in range(1, w):
            s = s + ext_ref[h - k:h - k + tl, cols]
        d = s * (1.0 / w) - u
        y = jnp.dot(d.astype(BF16), w_ref[g], preferred_element_type=F32)
        o_ref[:, cols] = (y * sc_ref[:, cols]).astype(o_ref.dtype)
    ext_ref[0:h, :] = ext_ref[tl:tl + h, :]


def pool_mixer(z, zmeta, pool_w, pool_scale, *, tl=512):
    b, l, _ = z.shape
    ng, gw, _ = pool_w.shape
    c = ng * gw
    tl = min(tl, l)
    return pl.pallas_call(
        functools.partial(_pool_kernel, tl=tl, gw=gw),
        grid=(b, l // tl),
        in_specs=[
            pl.BlockSpec((None, tl, c), lambda i, t: (i, t, 0)),
            pl.BlockSpec((N_META, c), lambda i, t: (0, 0)),
            pl.BlockSpec((ng, gw, gw), lambda i, t: (0, 0, 0)),
            pl.BlockSpec((1, c), lambda i, t: (0, 0)),
        ],
        out_specs=pl.BlockSpec((None, tl, c), lambda i, t: (i, t, 0)),
        out_shape=jax.ShapeDtypeStruct((b, l, c), BF16),
        scratch_shapes=[pltpu.VMEM((tl + N_META, c), F32)],
        compiler_params=_cparams(("arbitrary", "arbitrary")),
        name="pool_mixer",
    )(z, zmeta, pool_w, pool_scale)


def _ssm_toeplitz_kernel(bt_ref, cl_ref, m_ref):
    w = m_ref.shape[-1]
    lane = lax.broadcasted_iota(jnp.int32, (SSM_GROUP_CH, w), 1)
    for g in range(bt_ref.shape[0]):
        kt = jnp.dot(bt_ref[g], cl_ref[g], preferred_element_type=F32,
                     precision=lax.Precision.HIGHEST)
        for s in range(SSM_CHUNK):
            off = s * SSM_GROUP_CH
            blk = kt if s == 0 else jnp.where(lane >= off, pltpu.roll(kt, off, axis=1), 0.0)
            m_ref[g, off:off + SSM_GROUP_CH, :] = blk.astype(m_ref.dtype)


def ssm_toeplitz(bt, cl):
    g, hh, k2 = bt.shape
    w = cl.shape[2]
    gb = SSM_OCTET
    return pl.pallas_call(
        _ssm_toeplitz_kernel,
        grid=(g // gb,),
        in_specs=[pl.BlockSpec((gb, hh, k2), lambda i: (i, 0, 0)),
                  pl.BlockSpec((gb, k2, w), lambda i: (i, 0, 0))],
        out_specs=pl.BlockSpec((gb, w, w), lambda i: (i, 0, 0)),
        out_shape=jax.ShapeDtypeStruct((g, w, w), BF16),
        compiler_params=_cparams(("parallel",)),
        name="ssm_toeplitz",
    )(bt, cl)


def _block_transpose8(a, lane):
    for d in (4, 2, 1):
        w = d * SSM_GROUP_CH
        keep = (lane & w) == 0
        b = list(a)
        for k in range(SSM_OCTET):
            if k & d == 0:
                x, y = a[k], a[k + d]
                b[k] = jnp.where(keep, x, pltpu.roll(y, w, axis=1))
                b[k + d] = jnp.where(keep, pltpu.roll(x, LANES - w, axis=1), y)
        a = b
    return a


def _ssm_kernel(z_ref, zm_ref, m_ref, pre_ref, pim_ref, q_ref, ar_ref, ai_ref, o_ref, uscr, umscr,
                yscr, *, nc, nsteps):
    rt = SSM_ROWS
    t = SSM_CHUNK
    lane = lax.broadcasted_iota(jnp.int32, (rt, LANES), 1)

    def regroup(src_ref, dst_ref, it):
        base = pl.multiple_of(it * (rt * t), rt * t)
        row0 = pl.multiple_of(it * rt, rt)
        for half in range(2):
            a = [src_ref[pl.ds(base + half * SSM_OCTET + k, rt, stride=t), :]
                 for k in range(SSM_OCTET)]
            b = _block_transpose8(a, lane)
            for j in range(SSM_OCTET):
                dst_ref[j, pl.ds(row0, rt), half * LANES:(half + 1) * LANES] = b[j].astype(BF16)

    regroup(zm_ref, umscr, 0)

    def fwd(it, _):
        regroup(z_ref, uscr, it)
        return 0
    lax.fori_loop(0, nc // rt, fwd, 0, unroll=SSM_UNROLL)

    c = lax.broadcasted_iota(jnp.int32, (nc, 2 * SSM_STATE), 0)

    def per_pair(pi, _):
        j0 = 2 * pi
        j1 = j0 + 1
        u0 = uscr[j0]
        u1 = uscr[j1]
        up = jnp.concatenate([u0, u1], axis=1)
        ump = jnp.concatenate([umscr[j0], umscr[j1]], axis=1)
        pre = pre_ref[pi]
        pim = pim_ref[pi]
        s_re = jnp.dot(up, pre, preferred_element_type=F32)
        s_im = jnp.dot(up, pim, preferred_element_type=F32)
        x0_re = jnp.dot(ump, pre, preferred_element_type=F32)[0:1]
        x0_im = jnp.dot(ump, pim, preferred_element_type=F32)[0:1]
        xr = jnp.where(c == 0, x0_re, pltpu.roll(s_re, 1, axis=0))
        xi = jnp.where(c == 0, x0_im, pltpu.roll(s_im, 1, axis=0))
        d = 1
        for k in range(nsteps):
            zr = jnp.where(c >= d, pltpu.roll(xr, d, axis=0), 0.0)
            zi = jnp.where(c >= d, pltpu.roll(xi, d, axis=0), 0.0)
            a = ar_ref[pi, k:k + 1, :]
            b = ai_ref[pi, k:k + 1, :]
            xr, xi = xr + a * zr - b * zi, xi + a * zi + b * zr
            d *= 2
        xc = jnp.concatenate([xr, xi], axis=1).astype(BF16)
        y_inter = jnp.dot(xc, q_ref[pi], preferred_element_type=F32)
        w = u0.shape[1]
        y0 = jnp.dot(u0, m_ref[j0], preferred_element_type=F32) + y_inter[:, :w]
        y1 = jnp.dot(u1, m_ref[j1], preferred_element_type=F32) + y_inter[:, w:]
        yscr[j0] = y0
        yscr[j1] = y1
        return 0
    lax.fori_loop(0, SSM_OCTET // 2, per_pair, 0)

    def bwd(it, _):
        base = pl.multiple_of(it * (rt * t), rt * t)
        row0 = pl.multiple_of(it * rt, rt)
        for half in range(2):
            b = [yscr[j, pl.ds(row0, rt), half * LANES:(half + 1) * LANES]
                 for j in range(SSM_OCTET)]
            a = _block_transpose8(b, lane)
            for k in range(SSM_OCTET):
                o_ref[pl.ds(base + half * SSM_OCTET + k, rt, stride=t), :] = a[k]
        return 0
    lax.fori_loop(0, nc // rt, bwd, 0, unroll=SSM_UNROLL)


def ssm_chunked(z, zm, m, pre, pim, q, ar, ai, *, bsz, seq, col0):
    g, w, _ = m.shape
    ns2 = pre.shape[2]
    nsteps = ar.shape[1]
    nc = seq // SSM_CHUNK
    gb = SSM_OCTET
    pb = gb // 2
    cb = col0 // LANES
    return pl.pallas_call(
        functools.partial(_ssm_kernel, nc=nc, nsteps=nsteps),
        grid=(g // gb, bsz),
        in_specs=[
            pl.BlockSpec((seq, LANES), lambda j, b: (b, cb + j)),
            pl.BlockSpec((SSM_ROWS * SSM_CHUNK, LANES), lambda j, b: (0, cb + j)),
            pl.BlockSpec((gb, w, w), lambda j, b: (j, 0, 0)),
            pl.BlockSpec((pb, 2 * w, ns2), lambda j, b: (j, 0, 0)),
            pl.BlockSpec((pb, 2 * w, ns2), lambda j, b: (j, 0, 0)),
            pl.BlockSpec((pb, 2 * ns2, 2 * w), lambda j, b: (j, 0, 0)),
            pl.BlockSpec((pb, nsteps, ns2), lambda j, b: (j, 0, 0)),
            pl.BlockSpec((pb, nsteps, ns2), lambda j, b: (j, 0, 0)),
        ],
        out_specs=pl.BlockSpec((seq, LANES), lambda j, b: (b, j)),
        out_shape=jax.ShapeDtypeStruct((bsz * seq, g * SSM_GROUP_CH), F32),
        scratch_shapes=[
            pltpu.VMEM((gb, nc, w), BF16),
            pltpu.VMEM((gb, SSM_ROWS, w), BF16),
            pltpu.VMEM((gb, nc, w), F32),
        ],
        compiler_params=_cparams(("parallel", "arbitrary")),
        name="ssm_chunked",
    )(z, zm, m, pre, pim, q, ar, ai)


def _ssm_operators(a_re, a_im, log_dt, b_re, b_im, c_re, c_im, nc):
    t = SSM_CHUNK
    ar = a_re.astype(F32)
    ai = a_im.astype(F32)
    dt = jnp.exp(log_dt.astype(F32))[:, None]
    mag = jnp.exp(ar * dt)
    lam_re = mag * jnp.cos(ai * dt)
    lam_im = mag * jnp.sin(ai * dt)
    den = ar * ar + ai * ai
    nr = lam_re - 1.0
    coef_re = (nr * ar + lam_im * ai) / den
    coef_im = (lam_im * ar - nr * ai) / den
    br = b_re.astype(F32)
    bi = b_im.astype(F32)
    bb_re = coef_re[..., None] * br - coef_im[..., None] * bi
    bb_im = coef_re[..., None] * bi + coef_im[..., None] * br
    crt = c_re.astype(F32).transpose(0, 2, 1)
    cit = c_im.astype(F32).transpose(0, 2, 1)

    def lam_pow(k):
        k = k.astype(F32)
        m = jnp.exp(ar[..., None] * dt[..., None] * k)
        ang = ai[..., None] * dt[..., None] * k
        return m * jnp.cos(ang), m * jnp.sin(ang)

    g = ar.shape[0]
    hh = SSM_GROUP_CH
    pr, pi = lam_pow(jnp.arange(t + 1))

    def times_c(xr, xi):
        return (xr[..., None] * crt[:, :, None, :] - xi[..., None] * cit[:, :, None, :],
                xr[..., None] * cit[:, :, None, :] + xi[..., None] * crt[:, :, None, :])

    cl_re, cl_im = times_c(pr[:, :, :t], pi[:, :, :t])
    cl = jnp.concatenate([cl_re.reshape(g, -1, t * hh), cl_im.reshape(g, -1, t * hh)], axis=1)
    bt = jnp.concatenate([bb_re.transpose(0, 2, 1), -bb_im.transpose(0, 2, 1)], axis=-1)
    m_op = ssm_toeplitz(bt, cl)
    rr = pr[:, :, t - 1::-1][:, :, :t]
    ri = pi[:, :, t - 1::-1][:, :, :t]
    p_re = rr[..., None] * bb_re[:, :, None, :] - ri[..., None] * bb_im[:, :, None, :]
    p_im = rr[..., None] * bb_im[:, :, None, :] + ri[..., None] * bb_re[:, :, None, :]
    p_re = p_re.transpose(0, 2, 3, 1).reshape(g, t * hh, -1)
    p_im = p_im.transpose(0, 2, 3, 1).reshape(g, t * hh, -1)
    q_re, q_im = times_c(pr[:, :, 1:], pi[:, :, 1:])
    q_re = q_re.reshape(g, -1, t * hh)
    q_im = -q_im.reshape(g, -1, t * hh)
    nsteps = int(math.log2(nc))
    sr, si = lam_pow(t * (2 ** jnp.arange(nsteps)))

    def pair_rows(x):
        z = jnp.zeros_like(x[0::2])
        return jnp.concatenate([jnp.concatenate([x[0::2], z], axis=-1),
                                jnp.concatenate([z, x[1::2]], axis=-1)], axis=1)

    def pair_lanes(x):
        x = x.transpose(0, 2, 1)
        return jnp.concatenate([x[0::2], x[1::2]], axis=-1)

    pre_op = pair_rows(p_re).astype(BF16)
    pim_op = pair_rows(p_im).astype(BF16)
    q_op = jnp.concatenate([pair_rows(q_re), pair_rows(q_im)], axis=1).astype(BF16)
    return m_op, pre_op, pim_op, q_op, pair_lanes(sr), pair_lanes(si)


def _merge_kernel(p_ref, y_ref, us_ref, gp_ref, gs_ref, h_ref, dv_ref, wglu_ref, bglu_ref,
                  wbp_ref, wbs_ref, wout_ref, nf_ref, wr_ref, br_ref, h1_ref, lg_ref):
    s = _gelu_tanh(y_ref[...] + dv_ref[...] * us_ref[...])
    gl = jnp.dot(s.astype(BF16), wglu_ref[...], preferred_element_type=F32) + bglu_ref[...]
    s = s * _sigmoid(gl)
    y_ssm = jnp.dot(s.astype(BF16), wbs_ref[...], preferred_element_type=F32)
    y_pool = jnp.dot(p_ref[...], wbp_ref[...], preferred_element_type=F32)
    merged = gp_ref[...].astype(F32) * y_pool + gs_ref[...].astype(F32) * y_ssm
    h1 = h_ref[...] + jnp.dot(merged.astype(BF16), wout_ref[...], preferred_element_type=F32)
    h1_ref[...] = h1
    v = _rms(h1, nf_ref[...]).astype(BF16)
    lg_ref[...] = jnp.dot(v, wr_ref[...], preferred_element_type=F32) + br_ref[...]


def merge(pool_y, ssm_y, zps, gates, h0, dvec, w_glu, b_glu, w_bp, w_bs, w_out, nf, w_r, b_r,
          *, tm=256):
    n, d = h0.shape
    q = pool_y.shape[1]
    tm = min(tm, n)
    const = lambda shape: pl.BlockSpec(shape, lambda i: (0,) * len(shape),
                                       pipeline_mode=pl.Buffered(1))
    return pl.pallas_call(
        _merge_kernel,
        grid=(n // tm,),
        in_specs=[
            pl.BlockSpec((tm, q), lambda i: (i, 0)),
            pl.BlockSpec((tm, q), lambda i: (i, 0)),
            pl.BlockSpec((tm, q), lambda i: (i, 1)),
            pl.BlockSpec((tm, d), lambda i: (i, 0)),
            pl.BlockSpec((tm, d), lambda i: (i, 1)),
            pl.BlockSpec((tm, d), lambda i: (i, 0)),
            const((1, q)), const((q, q)), const((1, q)),
            const((q, d)), const((q, d)), const((d, d)),
            const((1, d)), const((d, ROUTER_LANES)), const((1, ROUTER_LANES)),
        ],
        out_specs=[
            pl.BlockSpec((tm, d), lambda i: (i, 0)),
            pl.BlockSpec((tm, ROUTER_LANES), lambda i: (i, 0)),
        ],
        out_shape=[
            jax.ShapeDtypeStruct((n, d), F32),
            jax.ShapeDtypeStruct((n, ROUTER_LANES), F32),
        ],
        compiler_params=_cparams(("parallel",)),
        name="merge",
    )(pool_y, ssm_y, zps, gates, gates, h0, dvec, w_glu, b_glu, w_bp, w_bs, w_out, nf, w_r, b_r)


def _route_kernel(lg_ref, gate_ref, eid_ref, rank_ref, cnt_ref, tri_ref, carry_ref):
    step = pl.program_id(0)
    tm = lg_ref.shape[0]

    @pl.when(step == 0)
    def _():
        rr = lax.broadcasted_iota(jnp.int32, (tm, tm), 0)
        cc = lax.broadcasted_iota(jnp.int32, (tm, tm), 1)
        tri_ref[...] = jnp.where(cc < rr, 1.0, 0.0).astype(BF16)
        carry_ref[...] = jnp.zeros_like(carry_ref)

    l = lg_ref[...]
    lane = lax.broadcasted_iota(jnp.int32, l.shape, 1).astype(F32)
    neg = -jnp.inf
    big = float(ROUTER_LANES)
    is_g = lane < MOE_GROUPS
    gl = jnp.where(is_g, l, neg)
    gmax = jnp.max(gl, axis=1, keepdims=True)
    gstar = jnp.min(jnp.where(gl == gmax, lane, big), axis=1, keepdims=True)
    pg = 1.0 / jnp.sum(jnp.where(is_g, jnp.exp(gl - gmax), 0.0), axis=1, keepdims=True)
    lo = MOE_GROUPS + EXPERTS_PER_GROUP * gstar
    is_e = (lane >= lo) & (lane < lo + EXPERTS_PER_GROUP)
    el = jnp.where(is_e, l, neg)
    m1 = jnp.max(el, axis=1, keepdims=True)
    i1 = jnp.min(jnp.where(is_e & (el == m1), lane, big), axis=1, keepdims=True)
    is_e2 = is_e & (lane != i1)
    el2 = jnp.where(is_e2, l, neg)
    m2 = jnp.max(el2, axis=1, keepdims=True)
    i2 = jnp.min(jnp.where(is_e2 & (el2 == m2), lane, big), axis=1, keepdims=True)
    t = jnp.exp(m2 - m1)
    w1 = pg / (1.0 + t)
    w2 = pg * t / (1.0 + t)
    gate_ref[...] = jnp.where(lane == 0.0, w1, jnp.where(lane == 1.0, w2, 0.0))
    eid = jnp.where(lane == 0.0, i1, jnp.where(lane == 1.0, i2, float(MOE_GROUPS)))
    eid_ref[...] = (eid - MOE_GROUPS).astype(jnp.int32)
    oh1 = lane == i1
    oh2 = lane == i2
    both = jnp.where(oh1 | oh2, 1.0, 0.0)
    before = carry_ref[...] + jnp.dot(tri_ref[...], both.astype(BF16),
                                      preferred_element_type=F32)
    r1 = jnp.sum(jnp.where(oh1, before, 0.0), axis=1, keepdims=True)
    r2 = jnp.sum(jnp.where(oh2, before, 0.0), axis=1, keepdims=True)
    rank_ref[...] = jnp.where(lane == 0.0, r1, jnp.where(lane == 1.0, r2, 0.0)).astype(jnp.int32)
    carry_ref[...] = carry_ref[...] + jnp.sum(both, axis=0, keepdims=True)
    cnt_ref[...] = carry_ref[...]


def route(logits, *, tm=1024):
    n, w = logits.shape
    tm = min(tm, n)
    tok = pl.BlockSpec((tm, w), lambda i: (i, 0))
    return pl.pallas_call(
        _route_kernel,
        grid=(n // tm,),
        in_specs=[tok],
        out_specs=[tok, tok, tok, pl.BlockSpec((1, w), lambda i: (0, 0))],
        out_shape=[jax.ShapeDtypeStruct((n, w), F32), jax.ShapeDtypeStruct((n, w), jnp.int32),
                   jax.ShapeDtypeStruct((n, w), jnp.int32), jax.ShapeDtypeStruct((1, w), F32)],
        scratch_shapes=[pltpu.VMEM((tm, tm), BF16), pltpu.VMEM((1, w), F32)],
        compiler_params=_cparams(("arbitrary",)),
        name="route",
    )(logits)


def _expert_kernel(be_ref, qb_ref, nu_ref, ord_ref, h_hbm, nf_ref, wg_ref, wu_ref, wd_ref,
                   o_ref, xbuf, vbuf, wgb, wub, wdb, gsem, *, rows):
    i = pl.program_id(0)
    nused = nu_ref[0]
    slot = lax.rem(i, 2)
    qmax = ord_ref.shape[0] - rows

    def first_pos(blk):
        return jnp.clip(qb_ref[blk], 0, qmax)

    def gather_row(q0, r, sl):
        tok = ord_ref[q0 + r] >> 1
        return pltpu.make_async_copy(h_hbm.at[pl.ds(tok, 1), :], xbuf.at[sl, pl.ds(r, 1), :],
                                     gsem.at[sl])

    def wait_gather(sl):
        pltpu.make_async_copy(h_hbm.at[pl.ds(0, rows), :], xbuf.at[sl], gsem.at[sl]).wait()

    @pl.when(i == 0)
    def _():
        q0 = first_pos(0)

        def body(r, _):
            gather_row(q0, r, 0).start()
            return 0
        lax.fori_loop(0, rows, body, 0)

    @pl.when(i < nused)
    def _():
        e = be_ref[i]
        changed = jnp.logical_or(i == 0, e != be_ref[jnp.maximum(i - 1, 0)])

        @pl.when(changed)
        def _():
            wgb[...] = wg_ref[...].astype(BF16)
            wub[...] = wu_ref[...].astype(BF16)
            wdb[...] = wd_ref[...].astype(BF16)

        wait_gather(slot)
        vbuf[...] = _rms(xbuf[slot], nf_ref[...]).astype(BF16)

        q_next = first_pos(jnp.minimum(i + 1, pl.num_programs(0) - 1))
        for r in range(rows):
            gather_row(q_next, r, 1 - slot).start()

        v = vbuf[...]
        hg = jnp.dot(v, wgb[...], preferred_element_type=F32)
        hu = jnp.dot(v, wub[...], preferred_element_type=F32)
        a = (hg * _sigmoid(hg) * hu).astype(BF16)
        o_ref[...] = jnp.dot(a, wdb[...], preferred_element_type=F32)

        @pl.when(i == nused - 1)
        def _():
            wait_gather(1 - slot)

    @pl.when(i >= nused)
    def _():
        o_ref[...] = jnp.zeros_like(o_ref)


def experts(block_e, qbase, nused, order, h1, nf, w_gate, w_up, w_down, *, rows=EXPERT_ROWS):
    n, d = h1.shape
    ne, _, ff = w_gate.shape
    nblk = block_e.shape[0]
    grid_spec = pltpu.PrefetchScalarGridSpec(
        num_scalar_prefetch=4,
        grid=(nblk,),
        in_specs=[
            pl.BlockSpec(memory_space=pl.ANY),
            pl.BlockSpec((1, d), lambda i, *_: (0, 0)),
            pl.BlockSpec((None, d, ff), lambda i, be, *_: (be[i], 0, 0)),
            pl.BlockSpec((None, d, ff), lambda i, be, *_: (be[i], 0, 0)),
            pl.BlockSpec((None, ff, d), lambda i, be, *_: (be[i], 0, 0)),
        ],
        out_specs=pl.BlockSpec((rows, d), lambda i, *_: (i, 0)),
        scratch_shapes=[
            pltpu.VMEM((2, rows, d), F32),
            pltpu.VMEM((rows, d), BF16),
            pltpu.VMEM((d, ff), BF16),
            pltpu.VMEM((d, ff), BF16),
            pltpu.VMEM((ff, d), BF16),
            pltpu.SemaphoreType.DMA((2,)),
        ],
    )
    return pl.pallas_call(
        functools.partial(_expert_kernel, rows=rows),
        grid_spec=grid_spec,
        out_shape=jax.ShapeDtypeStruct((nblk * rows, d), F32),
        compiler_params=_cparams(("arbitrary",)),
        name="experts",
    )(block_e, qbase, nused, order, h1, nf, w_gate, w_up, w_down)


def _dispatch_plan(eid, rank, counts, rows):
    n_experts = counts.shape[0]
    m = eid.size
    e_flat = eid.reshape(m)
    order = jnp.pad(jnp.argsort(e_flat).astype(jnp.int32), (0, rows))
    nblk_e = (counts + rows - 1) // rows
    bend = jnp.cumsum(nblk_e)
    bstart = bend - nblk_e
    start = jnp.cumsum(counts) - counts
    nblk = (m + n_experts * (rows - 1) + rows - 1) // rows
    bi = jnp.arange(nblk, dtype=jnp.int32)
    block_e = jnp.minimum(jnp.sum((bi[:, None] >= bend[None, :]).astype(jnp.int32), axis=1),
                          n_experts - 1)
    qbase = (start[block_e] + (bi - bstart[block_e]) * rows).astype(jnp.int32)
    nused = bend[-1:].astype(jnp.int32)
    dest = (bstart[e_flat] * rows + rank.reshape(m)).astype(jnp.int32)
    return block_e, qbase, nused, order, dest


def _combine_kernel(dest_ref, h_ref, y_hbm, gate_ref, g_ref, o_ref, ybuf, sem, *, tm):
    i = pl.program_id(0)
    slot = lax.rem(i, 2)

    def row_copy(tile, t, k, sl):
        row = dest_ref[(tile * tm + t) * 2 + k]
        return pltpu.make_async_copy(y_hbm.at[pl.ds(row, 1), :], ybuf.at[sl, k, pl.ds(t, 1), :],
                                     sem.at[sl])

    def wait_rows(sl):
        for k in range(2):
            pltpu.make_async_copy(y_hbm.at[pl.ds(0, tm), :], ybuf.at[sl, k], sem.at[sl]).wait()

    @pl.when(i == 0)
    def _():
        def body(t, _):
            row_copy(0, t, 0, 0).start()
            row_copy(0, t, 1, 0).start()
            return 0
        lax.fori_loop(0, tm, body, 0)

    wait_rows(slot)
    nxt = jnp.minimum(i + 1, pl.num_programs(0) - 1)
    for t in range(tm):
        for k in range(2):
            row_copy(nxt, t, k, 1 - slot).start()
    gate = gate_ref[...]
    h = h_ref[...] + gate[:, 0:1] * ybuf[slot, 0] + gate[:, 1:2] * ybuf[slot, 1]
    o_ref[...] = _rms(h, g_ref[...])

    @pl.when(i == pl.num_programs(0) - 1)
    def _():
        wait_rows(1 - slot)


def combine(dest, h1, y, gate, g, *, tm=256):
    n, d = h1.shape
    tm = min(tm, n)
    grid_spec = pltpu.PrefetchScalarGridSpec(
        num_scalar_prefetch=1,
        grid=(n // tm,),
        in_specs=[
            pl.BlockSpec((tm, d), lambda i, ds: (i, 0)),
            pl.BlockSpec(memory_space=pl.ANY),
            pl.BlockSpec((tm, gate.shape[1]), lambda i, ds: (i, 0)),
            pl.BlockSpec((1, d), lambda i, ds: (0, 0)),
        ],
        out_specs=pl.BlockSpec((tm, d), lambda i, ds: (i, 0)),
        scratch_shapes=[pltpu.VMEM((2, 2, tm, d), F32), pltpu.SemaphoreType.DMA((2,))],
    )
    return pl.pallas_call(
        functools.partial(_combine_kernel, tm=tm),
        grid_spec=grid_spec,
        out_shape=jax.ShapeDtypeStruct((n, d), F32),
        compiler_params=_cparams(("arbitrary",)),
        name="combine",
    )(dest, h1, y, gate, g)


def kernel(x, meta, norm_mix, w_in, pool_w, pool_scale, ssm_a_re, ssm_a_im, ssm_log_dt, ssm_b_re,
           ssm_b_im, ssm_c_re, ssm_c_im, ssm_d, w_glu, b_glu, w_branch_pool, w_branch_ssm, w_out,
           norm_ffn, w_router_group, b_router_group, w_router_expert, b_router_expert, w_gate,
           w_up, w_down, norm_final):
    assert w_in.shape[0] == 1, "single layer"
    bsz, seq, d = x.shape
    n = bsz * seq
    q = pool_scale.shape[1]
    assert meta.shape[0] == N_META == SSM_CHUNK and N_META >= max(POOL_WINDOWS)
    nc = seq // SSM_CHUNK
    assert nc & (nc - 1) == 0 and nc % SSM_ROWS == 0

    xf = x.reshape(n, d)
    g_mix = norm_mix[0][None, :]
    w_in_b = w_in[0].astype(BF16)

    zps = rms_matmul(xf, g_mix, w_in_b, col0=0, cols=2 * q, act=False, out_dtype=F32)
    zps_meta = rms_matmul(meta.astype(F32), g_mix, w_in_b, col0=0, cols=2 * q, act=False,
                          out_dtype=F32)
    zps_meta = jnp.pad(zps_meta, ((0, SSM_ROWS * SSM_CHUNK - N_META), (0, 0)))
    gates = rms_matmul(xf, g_mix, w_in_b, col0=2 * q, cols=2 * d, act=True, out_dtype=BF16)

    pool_y = pool_mixer(zps.reshape(bsz, seq, 2 * q), zps_meta, pool_w[0].astype(BF16),
                        pool_scale[0][None, :]).reshape(n, q)

    ssm_ops = _ssm_operators(ssm_a_re[0], ssm_a_im[0], ssm_log_dt[0], ssm_b_re[0], ssm_b_im[0],
                             ssm_c_re[0], ssm_c_im[0], nc)
    ssm_y = ssm_chunked(zps, zps_meta, *ssm_ops, bsz=bsz, seq=seq, col0=q)

    n_groups = w_router_group.shape[2]
    n_exp = w_router_expert.shape[2]
    w_r = jnp.concatenate([w_router_group[0], w_router_expert[0]], axis=1)
    w_r = jnp.pad(w_r, ((0, 0), (0, ROUTER_LANES - n_groups - n_exp))).astype(BF16)
    b_r = jnp.concatenate([b_router_group[0], b_router_expert[0]])
    b_r = jnp.pad(b_r, (0, ROUTER_LANES - n_groups - n_exp))[None, :].astype(F32)
    h1, logits = merge(pool_y, ssm_y, zps, gates, xf, ssm_d[0][None, :], w_glu[0].astype(BF16),
                       b_glu[0][None, :], w_branch_pool[0].astype(BF16),
                       w_branch_ssm[0].astype(BF16), w_out[0].astype(BF16),
                       norm_ffn[0][None, :], w_r, b_r)

    gate, eid, rank, cnt = route(logits)
    counts = cnt[0, n_groups:n_groups + n_exp].astype(jnp.int32)
    block_e, qbase, nused, order, dest = _dispatch_plan(eid[:, :2], rank[:, :2], counts,
                                                        EXPERT_ROWS)
    y = experts(block_e, qbase, nused, order, h1, norm_ffn[0][None, :], w_gate[0], w_up[0],
                w_down[0])
    out = combine(dest, h1, y, gate, norm_final[None, :])
    return out.reshape(bsz, seq, d)
```

```python
import functools
import math

import jax
import jax.numpy as jnp
from jax import lax
from jax.experimental import pallas as pl
from jax.experimental.pallas import tpu as pltpu

F32 = jnp.float32
BF16 = jnp.bfloat16

RMS_EPS = 1e-6
POOL_WINDOWS = (2, 4, 8, 16)
N_META = 16
LANES = 128
SSM_CHUNK = 16
SSM_GROUP_CH = 16
SSM_STATE = 64
SSM_OCTET = LANES // SSM_GROUP_CH
SSM_ROWS = 16
SSM_UNROLL = 4
MOE_GROUPS = 8
EXPERTS_PER_GROUP = 8
ROUTER_LANES = LANES
EXPERT_ROWS = 256
VMEM_LIMIT = 56 * 1024 * 1024


def _cparams(sem, vmem=VMEM_LIMIT):
    return pltpu.CompilerParams(dimension_semantics=sem, vmem_limit_bytes=vmem)


def _sigmoid(x):
    return 1.0 / (1.0 + jnp.exp(-x))


def _gelu_tanh(x):
    c = math.sqrt(2.0 / math.pi)
    return 0.5 * x * (1.0 + jnp.tanh(c * (x + 0.044715 * (x * x * x))))


def _rms(x, g):
    ms = jnp.mean(x * x, axis=-1, keepdims=True)
    return x * lax.rsqrt(ms + RMS_EPS) * g


def _rms_matmul_kernel(x_ref, g_ref, w_ref, o_ref, u_ref, *, act):
    @pl.when(pl.program_id(1) == 0)
    def _():
        u_ref[...] = _rms(x_ref[...], g_ref[...]).astype(BF16)

    acc = jnp.dot(u_ref[...], w_ref[...], preferred_element_type=F32)
    if act:
        acc = _sigmoid(acc)
    o_ref[...] = acc.astype(o_ref.dtype)


def rms_matmul(x, g, w, *, col0, cols, act, out_dtype, tm=1024, tn=1024):
    n, d = x.shape
    tm = min(tm, n)
    cb = col0 // tn
    return pl.pallas_call(
        functools.partial(_rms_matmul_kernel, act=act),
        grid=(n // tm, cols // tn),
        in_specs=[
            pl.BlockSpec((tm, d), lambda i, j: (i, 0)),
            pl.BlockSpec((1, d), lambda i, j: (0, 0)),
            pl.BlockSpec((d, tn), lambda i, j: (0, j + cb)),
        ],
        out_specs=pl.BlockSpec((tm, tn), lambda i, j: (i, j)),
        out_shape=jax.ShapeDtypeStruct((n, cols), out_dtype),
        scratch_shapes=[pltpu.VMEM((tm, d), BF16)],
        compiler_params=_cparams(("parallel", "arbitrary")),
        name="rms_matmul",
    )(x, g, w)


def _pool_kernel(z_ref, zm_ref, w_ref, sc_ref, o_ref, ext_ref, *, tl, gw):
    h = N_META

    @pl.when(pl.program_id(1) == 0)
    def _():
        ext_ref[0:h, :] = zm_ref[...]

    ext_ref[h:h + tl, :] = z_ref[...]
    for g, w in enumerate(POOL_WINDOWS):
        cols = slice(g * gw, (g + 1) * gw)
        u = ext_ref[h:h + tl, cols]
        s = u
        for k in range(1, w):
            s = s + ext_ref[h - k:h - k + tl, cols]
        d = s * (1.0 / w) - u
        y = jnp.dot(d.astype(BF16), w_ref[g], preferred_element_type=F32)
        o_ref[:, cols] = (y * sc_ref[:, cols]).astype(o_ref.dtype)
    ext_ref[0:h, :] = ext_ref[tl:tl + h, :]


def pool_mixer(z, zmeta, pool_w, pool_scale, *, tl=512):
    b, l, _ = z.shape
    ng, gw, _ = pool_w.shape
    c = ng * gw
    tl = min(tl, l)
    return pl.pallas_call(
        functools.partial(_pool_kernel, tl=tl, gw=gw),
        grid=(b, l // tl),
        in_specs=[
            pl.BlockSpec((None, tl, c), lambda i, t: (i, t, 0)),
            pl.BlockSpec((N_META, c), lambda i, t: (0, 0)),
            pl.BlockSpec((ng, gw, gw), lambda i, t: (0, 0, 0)),
            pl.BlockSpec((1, c), lambda i, t: (0, 0)),
        ],
        out_specs=pl.BlockSpec((None, tl, c), lambda i, t: (i, t, 0)),
        out_shape=jax.ShapeDtypeStruct((b, l, c), BF16),
        scratch_shapes=[pltpu.VMEM((tl + N_META, c), F32)],
        compiler_params=_cparams(("arbitrary", "arbitrary")),
        name="pool_mixer",
    )(z, zmeta, pool_w, pool_scale)


def _ssm_toeplitz_kernel(bt_ref, cl_ref, m_ref):
    w = m_ref.shape[-1]
    lane = lax.broadcasted_iota(jnp.int32, (SSM_GROUP_CH, w), 1)
    for g in range(bt_ref.shape[0]):
        kt = jnp.dot(bt_ref[g], cl_ref[g], preferred_element_type=F32,
                     precision=lax.Precision.HIGHEST)
        for s in range(SSM_CHUNK):
            off = s * SSM_GROUP_CH
            blk = kt if s == 0 else jnp.where(lane >= off, pltpu.roll(kt, off, axis=1), 0.0)
            m_ref[g, off:off + SSM_GROUP_CH, :] = blk.astype(m_ref.dtype)


def ssm_toeplitz(bt, cl):
    g, hh, k2 = bt.shape
    w = cl.shape[2]
    gb = SSM_OCTET
    return pl.pallas_call(
        _ssm_toeplitz_kernel,
        grid=(g // gb,),
        in_specs=[pl.BlockSpec((gb, hh, k2), lambda i: (i, 0, 0)),
                  pl.BlockSpec((gb, k2, w), lambda i: (i, 0, 0))],
        out_specs=pl.BlockSpec((gb, w, w), lambda i: (i, 0, 0)),
        out_shape=jax.ShapeDtypeStruct((g, w, w), BF16),
        compiler_params=_cparams(("parallel",)),
        name="ssm_toeplitz",
    )(bt, cl)


def _block_transpose8(a, lane):
    for d in (4, 2, 1):
        w = d * SSM_GROUP_CH
        keep = (lane & w) == 0
        b = list(a)
        for k in range(SSM_OCTET):
            if k & d == 0:
                x, y = a[k], a[k + d]
                b[k] = jnp.where(keep, x, pltpu.roll(y, w, axis=1))
                b[k + d] = jnp.where(keep, pltpu.roll(x, LANES - w, axis=1), y)
        a = b
    return a


def _ssm_kernel(z_ref, zm_ref, m_ref, pre_ref, pim_ref, q_ref, ar_ref, ai_ref, o_ref, uscr, umscr,
                yscr, *, nc, nsteps):
    rt = SSM_ROWS
    t = SSM_CHUNK
    lane = lax.broadcasted_iota(jnp.int32, (rt, LANES), 1)

    def regroup(src_ref, dst_ref, it):
        base = pl.multiple_of(it * (rt * t), rt * t)
        row0 = pl.multiple_of(it * rt, rt)
        for half in range(2):
            a = [src_ref[pl.ds(base + half * SSM_OCTET + k, rt, stride=t), :]
                 for k in range(SSM_OCTET)]
            b = _block_transpose8(a, lane)
            for j in range(SSM_OCTET):
                dst_ref[j, pl.ds(row0, rt), half * LANES:(half + 1) * LANES] = b[j].astype(BF16)

    regroup(zm_ref, umscr, 0)

    def fwd(it, _):
        regroup(z_ref, uscr, it)
        return 0
    lax.fori_loop(0, nc // rt, fwd, 0, unroll=SSM_UNROLL)

    c = lax.broadcasted_iota(jnp.int32, (nc, 2 * SSM_STATE), 0)

    def per_pair(pi, _):
        j0 = 2 * pi
        j1 = j0 + 1
        u0 = uscr[j0]
        u1 = uscr[j1]
        up = jnp.concatenate([u0, u1], axis=1)
        ump = jnp.concatenate([umscr[j0], umscr[j1]], axis=1)
        pre = pre_ref[pi]
        pim = pim_ref[pi]
        s_re = jnp.dot(up, pre, preferred_element_type=F32)
        s_im = jnp.dot(up, pim, preferred_element_type=F32)
        x0_re = jnp.dot(ump, pre, preferred_element_type=F32)[0:1]
        x0_im = jnp.dot(ump, pim, preferred_element_type=F32)[0:1]
        xr = jnp.where(c == 0, x0_re, pltpu.roll(s_re, 1, axis=0))
        xi = jnp.where(c == 0, x0_im, pltpu.roll(s_im, 1, axis=0))
        d = 1
        for k in range(nsteps):
            zr = jnp.where(c >= d, pltpu.roll(xr, d, axis=0), 0.0)
            zi = jnp.where(c >= d, pltpu.roll(xi, d, axis=0), 0.0)
            a = ar_ref[pi, k:k + 1, :]
            b = ai_ref[pi, k:k + 1, :]
            xr, xi = xr + a * zr - b * zi, xi + a * zi + b * zr
            d *= 2
        xc = jnp.concatenate([xr, xi], axis=1).astype(BF16)
        y_inter = jnp.dot(xc, q_ref[pi], preferred_element_type=F32)
        w = u0.shape[1]
        y0 = jnp.dot(u0, m_ref[j0], preferred_element_type=F32) + y_inter[:, :w]
        y1 = jnp.dot(u1, m_ref[j1], preferred_element_type=F32) + y_inter[:, w:]
        yscr[j0] = y0
        yscr[j1] = y1
        return 0
    lax.fori_loop(0, SSM_OCTET // 2, per_pair, 0)

    def bwd(it, _):
        base = pl.multiple_of(it * (rt * t), rt * t)
        row0 = pl.multiple_of(it * rt, rt)
        for half in range(2):
            b = [yscr[j, pl.ds(row0, rt), half * LANES:(half + 1) * LANES]
                 for j in range(SSM_OCTET)]
            a = _block_transpose8(b, lane)
            for k in range(SSM_OCTET):
                o_ref[pl.ds(base + half * SSM_OCTET + k, rt, stride=t), :] = a[k]
        return 0
    lax.fori_loop(0, nc // rt, bwd, 0, unroll=SSM_UNROLL)


def ssm_chunked(z, zm, m, pre, pim, q, ar, ai, *, bsz, seq, col0):
    g, w, _ = m.shape
    ns2 = pre.shape[2]
    nsteps = ar.shape[1]
    nc = seq // SSM_CHUNK
    gb = SSM_OCTET
    pb = gb // 2
    cb = col0 // LANES
    return pl.pallas_call(
        functools.partial(_ssm_kernel, nc=nc, nsteps=nsteps),
        grid=(g // gb, bsz),
        in_specs=[
            pl.BlockSpec((seq, LANES), lambda j, b: (b, cb + j)),
            pl.BlockSpec((SSM_ROWS * SSM_CHUNK, LANES), lambda j, b: (0, cb + j)),
            pl.BlockSpec((gb, w, w), lambda j, b: (j, 0, 0)),
            pl.BlockSpec((pb, 2 * w, ns2), lambda j, b: (j, 0, 0)),
            pl.BlockSpec((pb, 2 * w, ns2), lambda j, b: (j, 0, 0)),
            pl.BlockSpec((pb, 2 * ns2, 2 * w), lambda j, b: (j, 0, 0)),
            pl.BlockSpec((pb, nsteps, ns2), lambda j, b: (j, 0, 0)),
            pl.BlockSpec((pb, nsteps, ns2), lambda j, b: (j, 0, 0)),
        ],
        out_specs=pl.BlockSpec((seq, LANES), lambda j, b: (b, j)),
        out_shape=jax.ShapeDtypeStruct((bsz * seq, g * SSM_GROUP_CH), F32),
        scratch_shapes=[
            pltpu.VMEM((gb, nc, w), BF16),
            pltpu.VMEM((gb, SSM_ROWS, w), BF16),
            pltpu.VMEM((gb, nc, w), F32),
        ],
        compiler_params=_cparams(("parallel", "arbitrary")),
        name="ssm_chunked",
    )(z, zm, m, pre, pim, q, ar, ai)


def _ssm_operators(a_re, a_im, log_dt, b_re, b_im, c_re, c_im, nc):
    t = SSM_CHUNK
    ar = a_re.astype(F32)
    ai = a_im.astype(F32)
    dt = jnp.exp(log_dt.astype(F32))[:, None]
    mag = jnp.exp(ar * dt)
    lam_re = mag * jnp.cos(ai * dt)
    lam_im = mag * jnp.sin(ai * dt)
    den = ar * ar + ai * ai
    nr = lam_re - 1.0
    coef_re = (nr * ar + lam_im * ai) / den
    coef_im = (lam_im * ar - nr * ai) / den
    br = b_re.astype(F32)
    bi = b_im.astype(F32)
    bb_re = coef_re[..., None] * br - coef_im[..., None] * bi
    bb_im = coef_re[..., None] * bi + coef_im[..., None] * br
    crt = c_re.astype(F32).transpose(0, 2, 1)
    cit = c_im.astype(F32).transpose(0, 2, 1)

    def lam_pow(k):
        k = k.astype(F32)
        m = jnp.exp(ar[..., None] * dt[..., None] * k)
        ang = ai[..., None] * dt[..., None] * k
        return m * jnp.cos(ang), m * jnp.sin(ang)

    g = ar.shape[0]
    hh = SSM_GROUP_CH
    pr, pi = lam_pow(jnp.arange(t + 1))

    def times_c(xr, xi):
        return (xr[..., None] * crt[:, :, None, :] - xi[..., None] * cit[:, :, None, :],
                xr[..., None] * cit[:, :, None, :] + xi[..., None] * crt[:, :, None, :])

    cl_re, cl_im = times_c(pr[:, :, :t], pi[:, :, :t])
    cl = jnp.concatenate([cl_re.reshape(g, -1, t * hh), cl_im.reshape(g, -1, t * hh)], axis=1)
    bt = jnp.concatenate([bb_re.transpose(0, 2, 1), -bb_im.transpose(0, 2, 1)], axis=-1)
    m_op = ssm_toeplitz(bt, cl)
    rr = pr[:, :, t - 1::-1][:, :, :t]
    ri = pi[:, :, t - 1::-1][:, :, :t]
    p_re = rr[..., None] * bb_re[:, :, None, :] - ri[..., None] * bb_im[:, :, None, :]
    p_im = rr[..., None] * bb_im[:, :, None, :] + ri[..., None] * bb_re[:, :, None, :]
    p_re = p_re.transpose(0, 2, 3, 1).reshape(g, t * hh, -1)
    p_im = p_im.transpose(0, 2, 3, 1).reshape(g, t * hh, -1)
    q_re, q_im = times_c(pr[:, :, 1:], pi[:, :, 1:])
    q_re = q_re.reshape(g, -1, t * hh)
    q_im = -q_im.reshape(g, -1, t * hh)
    nsteps = int(math.log2(nc))
    sr, si = lam_pow(t * (2 ** jnp.arange(nsteps)))

    def pair_rows(x):
        z = jnp.zeros_like(x[0::2])
        return jnp.concatenate([jnp.concatenate([x[0::2], z], axis=-1),
                                jnp.concatenate([z, x[1::2]], axis=-1)], axis=1)

    def pair_lanes(x):
        x = x.transpose(0, 2, 1)
        return jnp.concatenate([x[0::2], x[1::2]], axis=-1)

    pre_op = pair_rows(p_re).astype(BF16)
    pim_op = pair_rows(p_im).astype(BF16)
    q_op = jnp.concatenate([pair_rows(q_re), pair_rows(q_im)], axis=1).astype(BF16)
    return m_op, pre_op, pim_op, q_op, pair_lanes(sr), pair_lanes(si)


def _merge_kernel(p_ref, y_ref, us_ref, gp_ref, gs_ref, h_ref, dv_ref, wglu_ref, bglu_ref,
                  wbp_ref, wbs_ref, wout_ref, nf_ref, wr_ref, br_ref, h1_ref, lg_ref):
    s = _gelu_tanh(y_ref[...] + dv_ref[...] * us_ref[...])
    gl = jnp.dot(s.astype(BF16), wglu_ref[...], preferred_element_type=F32) + bglu_ref[...]
    s = s * _sigmoid(gl)
    y_ssm = jnp.dot(s.astype(BF16), wbs_ref[...], preferred_element_type=F32)
    y_pool = jnp.dot(p_ref[...], wbp_ref[...], preferred_element_type=F32)
    merged = gp_ref[...].astype(F32) * y_pool + gs_ref[...].astype(F32) * y_ssm
    h1 = h_ref[...] + jnp.dot(merged.astype(BF16), wout_ref[...], preferred_element_type=F32)
    h1_ref[...] = h1
    v = _rms(h1, nf_ref[...]).astype(BF16)
    lg_ref[...] = jnp.dot(v, wr_ref[...], preferred_element_type=F32) + br_ref[...]


def merge(pool_y, ssm_y, zps, gates, h0, dvec, w_glu, b_glu, w_bp, w_bs, w_out, nf, w_r, b_r,
          *, tm=256):
    n, d = h0.shape
    q = pool_y.shape[1]
    tm = min(tm, n)
    const = lambda shape: pl.BlockSpec(shape, lambda i: (0,) * len(shape),
                                       pipeline_mode=pl.Buffered(1))
    return pl.pallas_call(
        _merge_kernel,
        grid=(n // tm,),
        in_specs=[
            pl.BlockSpec((tm, q), lambda i: (i, 0)),
            pl.BlockSpec((tm, q), lambda i: (i, 0)),
            pl.BlockSpec((tm, q), lambda i: (i, 1)),
            pl.BlockSpec((tm, d), lambda i: (i, 0)),
            pl.BlockSpec((tm, d), lambda i: (i, 1)),
            pl.BlockSpec((tm, d), lambda i: (i, 0)),
            const((1, q)), const((q, q)), const((1, q)),
            const((q, d)), const((q, d)), const((d, d)),
            const((1, d)), const((d, ROUTER_LANES)), const((1, ROUTER_LANES)),
        ],
        out_specs=[
            pl.BlockSpec((tm, d), lambda i: (i, 0)),
            pl.BlockSpec((tm, ROUTER_LANES), lambda i: (i, 0)),
        ],
        out_shape=[
            jax.ShapeDtypeStruct((n, d), F32),
            jax.ShapeDtypeStruct((n, ROUTER_LANES), F32),
        ],
        compiler_params=_cparams(("parallel",)),
        name="merge",
    )(pool_y, ssm_y, zps, gates, gates, h0, dvec, w_glu, b_glu, w_bp, w_bs, w_out, nf, w_r, b_r)


def _route_kernel(lg_ref, gate_ref, eid_ref, rank_ref, cnt_ref, tri_ref, carry_ref):
    step = pl.program_id(0)
    tm = lg_ref.shape[0]

    @pl.when(step == 0)
    def _():
        rr = lax.broadcasted_iota(jnp.int32, (tm, tm), 0)
        cc = lax.broadcasted_iota(jnp.int32, (tm, tm), 1)
        tri_ref[...] = jnp.where(cc < rr, 1.0, 0.0).astype(BF16)
        carry_ref[...] = jnp.zeros_like(carry_ref)

    l = lg_ref[...]
    lane = lax.broadcasted_iota(jnp.int32, l.shape, 1).astype(F32)
    neg = -jnp.inf
    big = float(ROUTER_LANES)
    is_g = lane < MOE_GROUPS
    gl = jnp.where(is_g, l, neg)
    gmax = jnp.max(gl, axis=1, keepdims=True)
    gstar = jnp.min(jnp.where(gl == gmax, lane, big), axis=1, keepdims=True)
    pg = 1.0 / jnp.sum(jnp.where(is_g, jnp.exp(gl - gmax), 0.0), axis=1, keepdims=True)
    lo = MOE_GROUPS + EXPERTS_PER_GROUP * gstar
    is_e = (lane >= lo) & (lane < lo + EXPERTS_PER_GROUP)
    el = jnp.where(is_e, l, neg)
    m1 = jnp.max(el, axis=1, keepdims=True)
    i1 = jnp.min(jnp.where(is_e & (el == m1), lane, big), axis=1, keepdims=True)
    is_e2 = is_e & (lane != i1)
    el2 = jnp.where(is_e2, l, neg)
    m2 = jnp.max(el2, axis=1, keepdims=True)
    i2 = jnp.min(jnp.where(is_e2 & (el2 == m2), lane, big), axis=1, keepdims=True)
    t = jnp.exp(m2 - m1)
    w1 = pg / (1.0 + t)
    w2 = pg * t / (1.0 + t)
    gate_ref[...] = jnp.where(lane == 0.0, w1, jnp.where(lane == 1.0, w2, 0.0))
    eid = jnp.where(lane == 0.0, i1, jnp.where(lane == 1.0, i2, float(MOE_GROUPS)))
    eid_ref[...] = (eid - MOE_GROUPS).astype(jnp.int32)
    oh1 = lane == i1
    oh2 = lane == i2
    both = jnp.where(oh1 | oh2, 1.0, 0.0)
    before = carry_ref[...] + jnp.dot(tri_ref[...], both.astype(BF16),
                                      preferred_element_type=F32)
    r1 = jnp.sum(jnp.where(oh1, before, 0.0), axis=1, keepdims=True)
    r2 = jnp.sum(jnp.where(oh2, before, 0.0), axis=1, keepdims=True)
    rank_ref[...] = jnp.where(lane == 0.0, r1, jnp.where(lane == 1.0, r2, 0.0)).astype(jnp.int32)
    carry_ref[...] = carry_ref[...] + jnp.sum(both, axis=0, keepdims=True)
    cnt_ref[...] = carry_ref[...]


def route(logits, *, tm=1024):
    n, w = logits.shape
    tm = min(tm, n)
    tok = pl.BlockSpec((tm, w), lambda i: (i, 0))
    return pl.pallas_call(
        _route_kernel,
        grid=(n // tm,),
        in_specs=[tok],
        out_specs=[tok, tok, tok, pl.BlockSpec((1, w), lambda i: (0, 0))],
        out_shape=[jax.ShapeDtypeStruct((n, w), F32), jax.ShapeDtypeStruct((n, w), jnp.int32),
                   jax.ShapeDtypeStruct((n, w), jnp.int32), jax.ShapeDtypeStruct((1, w), F32)],
        scratch_shapes=[pltpu.VMEM((tm, tm), BF16), pltpu.VMEM((1, w), F32)],
        compiler_params=_cparams(("arbitrary",)),
        name="route",
    )(logits)


def _expert_kernel(be_ref, qb_ref, nu_ref, ord_ref, h_hbm, nf_ref, wg_ref, wu_ref, wd_ref,
                   o_ref, xbuf, vbuf, wgb, wub, wdb, gsem, *, rows):
    i = pl.program_id(0)
    nused = nu_ref[0]
    slot = lax.rem(i, 2)
    qmax = ord_ref.shape[0] - rows

    def first_pos(blk):
        return jnp.clip(qb_ref[blk], 0, qmax)

    def gather_row(q0, r, sl):
        tok = ord_ref[q0 + r] >> 1
        return pltpu.make_async_copy(h_hbm.at[pl.ds(tok, 1), :], xbuf.at[sl, pl.ds(r, 1), :],
                                     gsem.at[sl])

    def wait_gather(sl):
        pltpu.make_async_copy(h_hbm.at[pl.ds(0, rows), :], xbuf.at[sl], gsem.at[sl]).wait()

    @pl.when(i == 0)
    def _():
        q0 = first_pos(0)

        def body(r, _):
            gather_row(q0, r, 0).start()
            return 0
        lax.fori_loop(0, rows, body, 0)

    @pl.when(i < nused)
    def _():
        e = be_ref[i]
        changed = jnp.logical_or(i == 0, e != be_ref[jnp.maximum(i - 1, 0)])

        @pl.when(changed)
        def _():
            wgb[...] = wg_ref[...].astype(BF16)
            wub[...] = wu_ref[...].astype(BF16)
            wdb[...] = wd_ref[...].astype(BF16)

        wait_gather(slot)
        vbuf[...] = _rms(xbuf[slot], nf_ref[...]).astype(BF16)

        q_next = first_pos(jnp.minimum(i + 1, pl.num_programs(0) - 1))
        for r in range(rows):
            gather_row(q_next, r, 1 - slot).start()

        v = vbuf[...]
        hg = jnp.dot(v, wgb[...], preferred_element_type=F32)
        hu = jnp.dot(v, wub[...], preferred_element_type=F32)
        a = (hg * _sigmoid(hg) * hu).astype(BF16)
        o_ref[...] = jnp.dot(a, wdb[...], preferred_element_type=F32)

        @pl.when(i == nused - 1)
        def _():
            wait_gather(1 - slot)

    @pl.when(i >= nused)
    def _():
        o_ref[...] = jnp.zeros_like(o_ref)


def experts(block_e, qbase, nused, order, h1, nf, w_gate, w_up, w_down, *, rows=EXPERT_ROWS):
    n, d = h1.shape
    ne, _, ff = w_gate.shape
    nblk = block_e.shape[0]
    grid_spec = pltpu.PrefetchScalarGridSpec(
        num_scalar_prefetch=4,
        grid=(nblk,),
        in_specs=[
            pl.BlockSpec(memory_space=pl.ANY),
            pl.BlockSpec((1, d), lambda i, *_: (0, 0)),
            pl.BlockSpec((None, d, ff), lambda i, be, *_: (be[i], 0, 0)),
            pl.BlockSpec((None, d, ff), lambda i, be, *_: (be[i], 0, 0)),
            pl.BlockSpec((None, ff, d), lambda i, be, *_: (be[i], 0, 0)),
        ],
        out_specs=pl.BlockSpec((rows, d), lambda i, *_: (i, 0)),
        scratch_shapes=[
            pltpu.VMEM((2, rows, d), F32),
            pltpu.VMEM((rows, d), BF16),
            pltpu.VMEM((d, ff), BF16),
            pltpu.VMEM((d, ff), BF16),
            pltpu.VMEM((ff, d), BF16),
            pltpu.SemaphoreType.DMA((2,)),
        ],
    )
    return pl.pallas_call(
        functools.partial(_expert_kernel, rows=rows),
        grid_spec=grid_spec,
        out_shape=jax.ShapeDtypeStruct((nblk * rows, d), F32),
        compiler_params=_cparams(("arbitrary",)),
        name="experts",
    )(block_e, qbase, nused, order, h1, nf, w_gate, w_up, w_down)


def _dispatch_plan(eid, rank, counts, rows):
    n_experts = counts.shape[0]
    m = eid.size
    e_flat = eid.reshape(m)
    order = jnp.pad(jnp.argsort(e_flat).astype(jnp.int32), (0, rows))
    nblk_e = (counts + rows - 1) // rows
    bend = jnp.cumsum(nblk_e)
    bstart = bend - nblk_e
    start = jnp.cumsum(counts) - counts
    nblk = (m + n_experts * (rows - 1) + rows - 1) // rows
    bi = jnp.arange(nblk, dtype=jnp.int32)
    block_e = jnp.minimum(jnp.sum((bi[:, None] >= bend[None, :]).astype(jnp.int32), axis=1),
                          n_experts - 1)
    qbase = (start[block_e] + (bi - bstart[block_e]) * rows).astype(jnp.int32)
    nused = bend[-1:].astype(jnp.int32)
    assert nblk <= 256
    first_blk = jnp.dot(jax.nn.one_hot(e_flat, n_experts, dtype=BF16),
                        bstart.astype(F32).astype(BF16)[:, None],
                        preferred_element_type=F32)[:, 0]
    dest = (first_blk.astype(jnp.int32) * rows + rank.reshape(m)).astype(jnp.int32)
    return block_e, qbase, nused, order, dest


def _combine_kernel(dest_ref, h_ref, y_hbm, gate_ref, g_ref, o_ref, ybuf, sem, *, tm):
    i = pl.program_id(0)
    last_tile = 2 * pl.num_programs(0) - 1

    def row_copy(tile, t, k, sl):
        row = dest_ref[(tile * tm + t) * 2 + k]
        return pltpu.make_async_copy(y_hbm.at[pl.ds(row, 1), :], ybuf.at[sl, k, pl.ds(t, 1), :],
                                     sem.at[sl])

    def wait_rows(sl):
        for k in range(2):
            pltpu.make_async_copy(y_hbm.at[pl.ds(0, tm), :], ybuf.at[sl, k], sem.at[sl]).wait()

    def issue(tile, sl):
        for t in range(tm):
            for k in range(2):
                row_copy(tile, t, k, sl).start()

    def finish(sl):
        rows = slice(sl * tm, (sl + 1) * tm)
        gate = gate_ref[rows, :]
        h = h_ref[rows, :] + gate[:, 0:1] * ybuf[sl, 0] + gate[:, 1:2] * ybuf[sl, 1]
        o_ref[rows, :] = _rms(h, g_ref[...])

    @pl.when(i == 0)
    def _():
        def body(t, _):
            row_copy(0, t, 0, 0).start()
            row_copy(0, t, 1, 0).start()
            return 0
        lax.fori_loop(0, tm, body, 0)

    wait_rows(0)
    issue(2 * i + 1, 1)
    finish(0)
    wait_rows(1)
    issue(jnp.minimum(2 * i + 2, last_tile), 0)
    finish(1)

    @pl.when(i == pl.num_programs(0) - 1)
    def _():
        wait_rows(0)


def combine(dest, h1, y, gate, g, *, tm=256):
    n, d = h1.shape
    tm = min(tm, n // 2)
    grid_spec = pltpu.PrefetchScalarGridSpec(
        num_scalar_prefetch=1,
        grid=(n // (2 * tm),),
        in_specs=[
            pl.BlockSpec((2 * tm, d), lambda i, ds: (i, 0)),
            pl.BlockSpec(memory_space=pl.ANY),
            pl.BlockSpec((2 * tm, gate.shape[1]), lambda i, ds: (i, 0)),
            pl.BlockSpec((1, d), lambda i, ds: (0, 0)),
        ],
        out_specs=pl.BlockSpec((2 * tm, d), lambda i, ds: (i, 0)),
        scratch_shapes=[pltpu.VMEM((2, 2, tm, d), F32), pltpu.SemaphoreType.DMA((2,))],
    )
    return pl.pallas_call(
        functools.partial(_combine_kernel, tm=tm),
        grid_spec=grid_spec,
        out_shape=jax.ShapeDtypeStruct((n, d), F32),
        compiler_params=_cparams(("arbitrary",)),
        name="combine",
    )(dest, h1, y, gate, g)


def kernel(x, meta, norm_mix, w_in, pool_w, pool_scale, ssm_a_re, ssm_a_im, ssm_log_dt, ssm_b_re,
           ssm_b_im, ssm_c_re, ssm_c_im, ssm_d, w_glu, b_glu, w_branch_pool, w_branch_ssm, w_out,
           norm_ffn, w_router_group, b_router_group, w_router_expert, b_router_expert, w_gate,
           w_up, w_down, norm_final):
    assert w_in.shape[0] == 1, "single layer"
    bsz, seq, d = x.shape
    n = bsz * seq
    q = pool_scale.shape[1]
    assert meta.shape[0] == N_META == SSM_CHUNK and N_META >= max(POOL_WINDOWS)
    nc = seq // SSM_CHUNK
    assert nc & (nc - 1) == 0 and nc % SSM_ROWS == 0

    xf = x.reshape(n, d)
    g_mix = norm_mix[0][None, :]
    w_in_b = w_in[0].astype(BF16)

    zps = rms_matmul(xf, g_mix, w_in_b, col0=0, cols=2 * q, act=False, out_dtype=F32)
    zps_meta = rms_matmul(meta.astype(F32), g_mix, w_in_b, col0=0, cols=2 * q, act=False,
                          out_dtype=F32)
    zps_meta = jnp.pad(zps_meta, ((0, SSM_ROWS * SSM_CHUNK - N_META), (0, 0)))
    gates = rms_matmul(xf, g_mix, w_in_b, col0=2 * q, cols=2 * d, act=True, out_dtype=BF16)

    pool_y = pool_mixer(zps.reshape(bsz, seq, 2 * q), zps_meta, pool_w[0].astype(BF16),
                        pool_scale[0][None, :]).reshape(n, q)

    ssm_ops = _ssm_operators(ssm_a_re[0], ssm_a_im[0], ssm_log_dt[0], ssm_b_re[0], ssm_b_im[0],
                             ssm_c_re[0], ssm_c_im[0], nc)
    ssm_y = ssm_chunked(zps, zps_meta, *ssm_ops, bsz=bsz, seq=seq, col0=q)

    n_groups = w_router_group.shape[2]
    n_exp = w_router_expert.shape[2]
    w_r = jnp.concatenate([w_router_group[0], w_router_expert[0]], axis=1)
    w_r = jnp.pad(w_r, ((0, 0), (0, ROUTER_LANES - n_groups - n_exp))).astype(BF16)
    b_r = jnp.concatenate([b_router_group[0], b_router_expert[0]])
    b_r = jnp.pad(b_r, (0, ROUTER_LANES - n_groups - n_exp))[None, :].astype(F32)
    h1, logits = merge(pool_y, ssm_y, zps, gates, xf, ssm_d[0][None, :], w_glu[0].astype(BF16),
                       b_glu[0][None, :], w_branch_pool[0].astype(BF16),
                       w_branch_ssm[0].astype(BF16), w_out[0].astype(BF16),
                       norm_ffn[0][None, :], w_r, b_r)

    gate, eid, rank, cnt = route(logits)
    counts = cnt[0, n_groups:n_groups + n_exp].astype(jnp.int32)
    block_e, qbase, nused, order, dest = _dispatch_plan(eid[:, :2], rank[:, :2], counts,
                                                        EXPERT_ROWS)
    y = experts(block_e, qbase, nused, order, h1, norm_ffn[0][None, :], w_gate[0], w_up[0],
                w_down[0])
    out = combine(dest, h1, y, gate, norm_final[None, :])
    return out.reshape(bsz, seq, d)
```

```python
import functools
import math

import jax
import jax.numpy as jnp
from jax import lax
from jax.experimental import pallas as pl
from jax.experimental.pallas import tpu as pltpu

F32 = jnp.float32
BF16 = jnp.bfloat16

RMS_EPS = 1e-6
POOL_WINDOWS = (2, 4, 8, 16)
N_META = 16
LANES = 128
SSM_CHUNK = 16
SSM_GROUP_CH = 16
SSM_STATE = 64
SSM_OCTET = LANES // SSM_GROUP_CH
SSM_ROWS = 16
SSM_UNROLL = 4
MOE_GROUPS = 8
EXPERTS_PER_GROUP = 8
ROUTER_LANES = LANES
EXPERT_ROWS = 256
VMEM_LIMIT = 56 * 1024 * 1024


def _cparams(sem, vmem=VMEM_LIMIT):
    return pltpu.CompilerParams(dimension_semantics=sem, vmem_limit_bytes=vmem)


def _sigmoid(x):
    return 1.0 / (1.0 + jnp.exp(-x))


def _gelu_tanh(x):
    c = math.sqrt(2.0 / math.pi)
    return 0.5 * x * (1.0 + jnp.tanh(c * (x + 0.044715 * (x * x * x))))


def _rms(x, g):
    ms = jnp.mean(x * x, axis=-1, keepdims=True)
    return x * lax.rsqrt(ms + RMS_EPS) * g


def _rms_matmul_kernel(x_ref, g_ref, w_ref, o_ref, u_ref, *, act):
    @pl.when(pl.program_id(1) == 0)
    def _():
        u_ref[...] = _rms(x_ref[...], g_ref[...]).astype(BF16)

    acc = jnp.dot(u_ref[...], w_ref[...], preferred_element_type=F32)
    if act:
        acc = _sigmoid(acc)
    o_ref[...] = acc.astype(o_ref.dtype)


def rms_matmul(x, g, w, *, col0, cols, act, out_dtype, tm=1024, tn=1024):
    n, d = x.shape
    tm = min(tm, n)
    cb = col0 // tn
    return pl.pallas_call(
        functools.partial(_rms_matmul_kernel, act=act),
        grid=(n // tm, cols // tn),
        in_specs=[
            pl.BlockSpec((tm, d), lambda i, j: (i, 0)),
            pl.BlockSpec((1, d), lambda i, j: (0, 0)),
            pl.BlockSpec((d, tn), lambda i, j: (0, j + cb)),
        ],
        out_specs=pl.BlockSpec((tm, tn), lambda i, j: (i, j)),
        out_shape=jax.ShapeDtypeStruct((n, cols), out_dtype),
        scratch_shapes=[pltpu.VMEM((tm, d), BF16)],
        compiler_params=_cparams(("parallel", "arbitrary")),
        name="rms_matmul",
    )(x, g, w)


def _pool_kernel(z_ref, zm_ref, w_ref, sc_ref, o_ref, ext_ref, *, tl, gw):
    h = N_META

    @pl.when(pl.program_id(1) == 0)
    def _():
        ext_ref[0:h, :] = zm_ref[...]

    ext_ref[h:h + tl, :] = z_ref[...]
    for g, w in enumerate(POOL_WINDOWS):
        cols = slice(g * gw, (g + 1) * gw)
        u = ext_ref[h:h + tl, cols]
        s = u
        for k in range(1, w):
            s = s + ext_ref[h - k:h - k + tl, cols]
        d = s * (1.0 / w) - u
        y = jnp.dot(d.astype(BF16), w_ref[g], preferred_element_type=F32)
        o_ref[:, cols] = (y * sc_ref[:, cols]).astype(o_ref.dtype)
    ext_ref[0:h, :] = ext_ref[tl:tl + h, :]


def pool_mixer(z, zmeta, pool_w, pool_scale, *, tl=512):
    b, l, _ = z.shape
    ng, gw, _ = pool_w.shape
    c = ng * gw
    tl = min(tl, l)
    return pl.pallas_call(
        functools.partial(_pool_kernel, tl=tl, gw=gw),
        grid=(b, l // tl),
        in_specs=[
            pl.BlockSpec((None, tl, c), lambda i, t: (i, t, 0)),
            pl.BlockSpec((N_META, c), lambda i, t: (0, 0)),
            pl.BlockSpec((ng, gw, gw), lambda i, t: (0, 0, 0)),
            pl.BlockSpec((1, c), lambda i, t: (0, 0)),
        ],
        out_specs=pl.BlockSpec((None, tl, c), lambda i, t: (i, t, 0)),
        out_shape=jax.ShapeDtypeStruct((b, l, c), BF16),
        scratch_shapes=[pltpu.VMEM((tl + N_META, c), F32)],
        compiler_params=_cparams(("arbitrary", "arbitrary")),
        name="pool_mixer",
    )(z, zmeta, pool_w, pool_scale)


def _ssm_toeplitz_kernel(bt_ref, cl_ref, m_ref):
    w = m_ref.shape[-1]
    lane = lax.broadcasted_iota(jnp.int32, (SSM_GROUP_CH, w), 1)
    for g in range(bt_ref.shape[0]):
        kt = jnp.dot(bt_ref[g], cl_ref[g], preferred_element_type=F32,
                     precision=lax.Precision.HIGHEST)
        for s in range(SSM_CHUNK):
            off = s * SSM_GROUP_CH
            blk = kt if s == 0 else jnp.where(lane >= off, pltpu.roll(kt, off, axis=1), 0.0)
            m_ref[g, off:off + SSM_GROUP_CH, :] = blk.astype(m_ref.dtype)


def ssm_toeplitz(bt, cl):
    g, hh, k2 = bt.shape
    w = cl.shape[2]
    gb = SSM_OCTET
    return pl.pallas_call(
        _ssm_toeplitz_kernel,
        grid=(g // gb,),
        in_specs=[pl.BlockSpec((gb, hh, k2), lambda i: (i, 0, 0)),
                  pl.BlockSpec((gb, k2, w), lambda i: (i, 0, 0))],
        out_specs=pl.BlockSpec((gb, w, w), lambda i: (i, 0, 0)),
        out_shape=jax.ShapeDtypeStruct((g, w, w), BF16),
        compiler_params=_cparams(("parallel",)),
        name="ssm_toeplitz",
    )(bt, cl)


def _block_transpose8(a, lane):
    for d in (4, 2, 1):
        w = d * SSM_GROUP_CH
        keep = (lane & w) == 0
        b = list(a)
        for k in range(SSM_OCTET):
            if k & d == 0:
                x, y = a[k], a[k + d]
                b[k] = jnp.where(keep, x, pltpu.roll(y, w, axis=1))
                b[k + d] = jnp.where(keep, pltpu.roll(x, LANES - w, axis=1), y)
        a = b
    return a


def _ssm_kernel(z_ref, zm_ref, m_ref, pre_ref, pim_ref, q_ref, ar_ref, ai_ref, o_ref, uscr, umscr,
                yscr, *, nc, nsteps):
    rt = SSM_ROWS
    t = SSM_CHUNK
    lane = lax.broadcasted_iota(jnp.int32, (rt, LANES), 1)

    def regroup(src_ref, dst_ref, it):
        base = pl.multiple_of(it * (rt * t), rt * t)
        row0 = pl.multiple_of(it * rt, rt)
        for half in range(2):
            a = [src_ref[pl.ds(base + half * SSM_OCTET + k, rt, stride=t), :]
                 for k in range(SSM_OCTET)]
            b = _block_transpose8(a, lane)
            for j in range(SSM_OCTET):
                dst_ref[j, pl.ds(row0, rt), half * LANES:(half + 1) * LANES] = b[j].astype(BF16)

    regroup(zm_ref, umscr, 0)

    def fwd(it, _):
        regroup(z_ref, uscr, it)
        return 0
    lax.fori_loop(0, nc // rt, fwd, 0, unroll=SSM_UNROLL)

    c = lax.broadcasted_iota(jnp.int32, (nc, 2 * SSM_STATE), 0)

    def per_pair(pi, _):
        j0 = 2 * pi
        j1 = j0 + 1
        u0 = uscr[j0]
        u1 = uscr[j1]
        up = jnp.concatenate([u0, u1], axis=1)
        ump = jnp.concatenate([umscr[j0], umscr[j1]], axis=1)
        pre = pre_ref[pi]
        pim = pim_ref[pi]
        s_re = jnp.dot(up, pre, preferred_element_type=F32)
        s_im = jnp.dot(up, pim, preferred_element_type=F32)
        x0_re = jnp.dot(ump, pre, preferred_element_type=F32)[0:1]
        x0_im = jnp.dot(ump, pim, preferred_element_type=F32)[0:1]
        xr = jnp.where(c == 0, x0_re, pltpu.roll(s_re, 1, axis=0))
        xi = jnp.where(c == 0, x0_im, pltpu.roll(s_im, 1, axis=0))
        d = 1
        for k in range(nsteps):
            zr = jnp.where(c >= d, pltpu.roll(xr, d, axis=0), 0.0)
            zi = jnp.where(c >= d, pltpu.roll(xi, d, axis=0), 0.0)
            a = ar_ref[pi, k:k + 1, :]
            b = ai_ref[pi, k:k + 1, :]
            xr, xi = xr + a * zr - b * zi, xi + a * zi + b * zr
            d *= 2
        xc = jnp.concatenate([xr, xi], axis=1).astype(BF16)
        y_inter = jnp.dot(xc, q_ref[pi], preferred_element_type=F32)
        w = u0.shape[1]
        y0 = jnp.dot(u0, m_ref[j0], preferred_element_type=F32) + y_inter[:, :w]
        y1 = jnp.dot(u1, m_ref[j1], preferred_element_type=F32) + y_inter[:, w:]
        yscr[j0] = y0
        yscr[j1] = y1
        return 0
    lax.fori_loop(0, SSM_OCTET // 2, per_pair, 0)

    def bwd(it, _):
        base = pl.multiple_of(it * (rt * t), rt * t)
        row0 = pl.multiple_of(it * rt, rt)
        for half in range(2):
            b = [yscr[j, pl.ds(row0, rt), half * LANES:(half + 1) * LANES]
                 for j in range(SSM_OCTET)]
            a = _block_transpose8(b, lane)
            for k in range(SSM_OCTET):
                o_ref[pl.ds(base + half * SSM_OCTET + k, rt, stride=t), :] = a[k]
        return 0
    lax.fori_loop(0, nc // rt, bwd, 0, unroll=SSM_UNROLL)


def ssm_chunked(z, zm, m, pre, pim, q, ar, ai, *, bsz, seq, col0):
    g, w, _ = m.shape
    ns2 = pre.shape[2]
    nsteps = ar.shape[1]
    nc = seq // SSM_CHUNK
    gb = SSM_OCTET
    pb = gb // 2
    cb = col0 // LANES
    return pl.pallas_call(
        functools.partial(_ssm_kernel, nc=nc, nsteps=nsteps),
        grid=(g // gb, bsz),
        in_specs=[
            pl.BlockSpec((seq, LANES), lambda j, b: (b, cb + j)),
            pl.BlockSpec((SSM_ROWS * SSM_CHUNK, LANES), lambda j, b: (0, cb + j)),
            pl.BlockSpec((gb, w, w), lambda j, b: (j, 0, 0)),
            pl.BlockSpec((pb, 2 * w, ns2), lambda j, b: (j, 0, 0)),
            pl.BlockSpec((pb, 2 * w, ns2), lambda j, b: (j, 0, 0)),
            pl.BlockSpec((pb, 2 * ns2, 2 * w), lambda j, b: (j, 0, 0)),
            pl.BlockSpec((pb, nsteps, ns2), lambda j, b: (j, 0, 0)),
            pl.BlockSpec((pb, nsteps, ns2), lambda j, b: (j, 0, 0)),
        ],
        out_specs=pl.BlockSpec((seq, LANES), lambda j, b: (b, j)),
        out_shape=jax.ShapeDtypeStruct((bsz * seq, g * SSM_GROUP_CH), F32),
        scratch_shapes=[
            pltpu.VMEM((gb, nc, w), BF16),
            pltpu.VMEM((gb, SSM_ROWS, w), BF16),
            pltpu.VMEM((gb, nc, w), F32),
        ],
        compiler_params=_cparams(("parallel", "arbitrary")),
        name="ssm_chunked",
    )(z, zm, m, pre, pim, q, ar, ai)


def _ssm_operators(a_re, a_im, log_dt, b_re, b_im, c_re, c_im, nc):
    t = SSM_CHUNK
    ar = a_re.astype(F32)
    ai = a_im.astype(F32)
    dt = jnp.exp(log_dt.astype(F32))[:, None]
    mag = jnp.exp(ar * dt)
    lam_re = mag * jnp.cos(ai * dt)
    lam_im = mag * jnp.sin(ai * dt)
    den = ar * ar + ai * ai
    nr = lam_re - 1.0
    coef_re = (nr * ar + lam_im * ai) / den
    coef_im = (lam_im * ar - nr * ai) / den
    br = b_re.astype(F32)
    bi = b_im.astype(F32)
    bb_re = coef_re[..., None] * br - coef_im[..., None] * bi
    bb_im = coef_re[..., None] * bi + coef_im[..., None] * br
    crt = c_re.astype(F32).transpose(0, 2, 1)
    cit = c_im.astype(F32).transpose(0, 2, 1)

    def lam_pow(k):
        k = k.astype(F32)
        m = jnp.exp(ar[..., None] * dt[..., None] * k)
        ang = ai[..., None] * dt[..., None] * k
        return m * jnp.cos(ang), m * jnp.sin(ang)

    g = ar.shape[0]
    hh = SSM_GROUP_CH
    pr, pi = lam_pow(jnp.arange(t + 1))

    def times_c(xr, xi):
        return (xr[..., None] * crt[:, :, None, :] - xi[..., None] * cit[:, :, None, :],
                xr[..., None] * cit[:, :, None, :] + xi[..., None] * crt[:, :, None, :])

    cl_re, cl_im = times_c(pr[:, :, :t], pi[:, :, :t])
    cl = jnp.concatenate([cl_re.reshape(g, -1, t * hh), cl_im.reshape(g, -1, t * hh)], axis=1)
    bt = jnp.concatenate([bb_re.transpose(0, 2, 1), -bb_im.transpose(0, 2, 1)], axis=-1)
    m_op = ssm_toeplitz(bt, cl)
    rr = pr[:, :, t - 1::-1][:, :, :t]
    ri = pi[:, :, t - 1::-1][:, :, :t]
    p_re = rr[..., None] * bb_re[:, :, None, :] - ri[..., None] * bb_im[:, :, None, :]
    p_im = rr[..., None] * bb_im[:, :, None, :] + ri[..., None] * bb_re[:, :, None, :]
    p_re = p_re.transpose(0, 2, 3, 1).reshape(g, t * hh, -1)
    p_im = p_im.transpose(0, 2, 3, 1).reshape(g, t * hh, -1)
    q_re, q_im = times_c(pr[:, :, 1:], pi[:, :, 1:])
    q_re = q_re.reshape(g, -1, t * hh)
    q_im = -q_im.reshape(g, -1, t * hh)
    nsteps = int(math.log2(nc))
    sr, si = lam_pow(t * (2 ** jnp.arange(nsteps)))

    def pair_rows(x):
        z = jnp.zeros_like(x[0::2])
        return jnp.concatenate([jnp.concatenate([x[0::2], z], axis=-1),
                                jnp.concatenate([z, x[1::2]], axis=-1)], axis=1)

    def pair_lanes(x):
        x = x.transpose(0, 2, 1)
        return jnp.concatenate([x[0::2], x[1::2]], axis=-1)

    pre_op = pair_rows(p_re).astype(BF16)
    pim_op = pair_rows(p_im).astype(BF16)
    q_op = jnp.concatenate([pair_rows(q_re), pair_rows(q_im)], axis=1).astype(BF16)
    return m_op, pre_op, pim_op, q_op, pair_lanes(sr), pair_lanes(si)


def _merge_kernel(p_ref, y_ref, us_ref, gp_ref, gs_ref, h_ref, dv_ref, wglu_ref, bglu_ref,
                  wbp_ref, wbs_ref, wout_ref, nf_ref, wr_ref, br_ref, h1_ref, lg_ref):
    s = _gelu_tanh(y_ref[...] + dv_ref[...] * us_ref[...])
    gl = jnp.dot(s.astype(BF16), wglu_ref[...], preferred_element_type=F32) + bglu_ref[...]
    s = s * _sigmoid(gl)
    y_ssm = jnp.dot(s.astype(BF16), wbs_ref[...], preferred_element_type=F32)
    y_pool = jnp.dot(p_ref[...], wbp_ref[...], preferred_element_type=F32)
    merged = gp_ref[...].astype(F32) * y_pool + gs_ref[...].astype(F32) * y_ssm
    h1 = h_ref[...] + jnp.dot(merged.astype(BF16), wout_ref[...], preferred_element_type=F32)
    h1_ref[...] = h1
    v = _rms(h1, nf_ref[...]).astype(BF16)
    lg_ref[...] = jnp.dot(v, wr_ref[...], preferred_element_type=F32) + br_ref[...]


def merge(pool_y, ssm_y, zps, gates, h0, dvec, w_glu, b_glu, w_bp, w_bs, w_out, nf, w_r, b_r,
          *, tm=256):
    n, d = h0.shape
    q = pool_y.shape[1]
    tm = min(tm, n)
    const = lambda shape: pl.BlockSpec(shape, lambda i: (0,) * len(shape),
                                       pipeline_mode=pl.Buffered(1))
    return pl.pallas_call(
        _merge_kernel,
        grid=(n // tm,),
        in_specs=[
            pl.BlockSpec((tm, q), lambda i: (i, 0)),
            pl.BlockSpec((tm, q), lambda i: (i, 0)),
            pl.BlockSpec((tm, q), lambda i: (i, 1)),
            pl.BlockSpec((tm, d), lambda i: (i, 0)),
            pl.BlockSpec((tm, d), lambda i: (i, 1)),
            pl.BlockSpec((tm, d), lambda i: (i, 0)),
            const((1, q)), const((q, q)), const((1, q)),
            const((q, d)), const((q, d)), const((d, d)),
            const((1, d)), const((d, ROUTER_LANES)), const((1, ROUTER_LANES)),
        ],
        out_specs=[
            pl.BlockSpec((tm, d), lambda i: (i, 0)),
            pl.BlockSpec((tm, ROUTER_LANES), lambda i: (i, 0)),
        ],
        out_shape=[
            jax.ShapeDtypeStruct((n, d), F32),
            jax.ShapeDtypeStruct((n, ROUTER_LANES), F32),
        ],
        compiler_params=_cparams(("parallel",)),
        name="merge",
    )(pool_y, ssm_y, zps, gates, gates, h0, dvec, w_glu, b_glu, w_bp, w_bs, w_out, nf, w_r, b_r)


def _route_kernel(lg_ref, gate_ref, eid_ref, rank_ref, cnt_ref, tri_ref, carry_ref):
    step = pl.program_id(0)
    tm = lg_ref.shape[0]

    @pl.when(step == 0)
    def _():
        rr = lax.broadcasted_iota(jnp.int32, (tm, tm), 0)
        cc = lax.broadcasted_iota(jnp.int32, (tm, tm), 1)
        tri_ref[...] = jnp.where(cc < rr, 1.0, 0.0).astype(BF16)
        carry_ref[...] = jnp.zeros_like(carry_ref)

    l = lg_ref[...]
    lane = lax.broadcasted_iota(jnp.int32, l.shape, 1).astype(F32)
    neg = -jnp.inf
    big = float(ROUTER_LANES)
    is_g = lane < MOE_GROUPS
    gl = jnp.where(is_g, l, neg)
    gmax = jnp.max(gl, axis=1, keepdims=True)
    gstar = jnp.min(jnp.where(gl == gmax, lane, big), axis=1, keepdims=True)
    pg = 1.0 / jnp.sum(jnp.where(is_g, jnp.exp(gl - gmax), 0.0), axis=1, keepdims=True)
    lo = MOE_GROUPS + EXPERTS_PER_GROUP * gstar
    is_e = (lane >= lo) & (lane < lo + EXPERTS_PER_GROUP)
    el = jnp.where(is_e, l, neg)
    m1 = jnp.max(el, axis=1, keepdims=True)
    i1 = jnp.min(jnp.where(is_e & (el == m1), lane, big), axis=1, keepdims=True)
    is_e2 = is_e & (lane != i1)
    el2 = jnp.where(is_e2, l, neg)
    m2 = jnp.max(el2, axis=1, keepdims=True)
    i2 = jnp.min(jnp.where(is_e2 & (el2 == m2), lane, big), axis=1, keepdims=True)
    t = jnp.exp(m2 - m1)
    w1 = pg / (1.0 + t)
    w2 = pg * t / (1.0 + t)
    gate_ref[...] = jnp.where(lane == 0.0, w1, jnp.where(lane == 1.0, w2, 0.0))
    eid = jnp.where(lane == 0.0, i1, jnp.where(lane == 1.0, i2, float(MOE_GROUPS)))
    eid_ref[...] = (eid - MOE_GROUPS).astype(jnp.int32)
    oh1 = lane == i1
    oh2 = lane == i2
    both = jnp.where(oh1 | oh2, 1.0, 0.0)
    before = carry_ref[...] + jnp.dot(tri_ref[...], both.astype(BF16),
                                      preferred_element_type=F32)
    r1 = jnp.sum(jnp.where(oh1, before, 0.0), axis=1, keepdims=True)
    r2 = jnp.sum(jnp.where(oh2, before, 0.0), axis=1, keepdims=True)
    rank_ref[...] = jnp.where(lane == 0.0, r1, jnp.where(lane == 1.0, r2, 0.0)).astype(jnp.int32)
    carry_ref[...] = carry_ref[...] + jnp.sum(both, axis=0, keepdims=True)
    cnt_ref[...] = carry_ref[...]


def route(logits, *, tm=1024):
    n, w = logits.shape
    tm = min(tm, n)
    tok = pl.BlockSpec((tm, w), lambda i: (i, 0))
    return pl.pallas_call(
        _route_kernel,
        grid=(n // tm,),
        in_specs=[tok],
        out_specs=[tok, tok, tok, pl.BlockSpec((1, w), lambda i: (0, 0))],
        out_shape=[jax.ShapeDtypeStruct((n, w), F32), jax.ShapeDtypeStruct((n, w), jnp.int32),
                   jax.ShapeDtypeStruct((n, w), jnp.int32), jax.ShapeDtypeStruct((1, w), F32)],
        scratch_shapes=[pltpu.VMEM((tm, tm), BF16), pltpu.VMEM((1, w), F32)],
        compiler_params=_cparams(("arbitrary",)),
        name="route",
    )(logits)


def _expert_kernel(be_ref, qb_ref, nu_ref, ord_ref, h_hbm, nf_ref, wg_ref, wu_ref, wd_ref,
                   o_ref, xbuf, vbuf, wgb, wub, wdb, gsem, *, rows):
    i = pl.program_id(0)
    nused = nu_ref[0]
    slot = lax.rem(i, 2)
    qmax = ord_ref.shape[0] - rows

    def first_pos(blk):
        return jnp.clip(qb_ref[blk], 0, qmax)

    def gather_row(q0, r, sl):
        tok = ord_ref[q0 + r] >> 1
        return pltpu.make_async_copy(h_hbm.at[pl.ds(tok, 1), :], xbuf.at[sl, pl.ds(r, 1), :],
                                     gsem.at[sl])

    def wait_gather(sl):
        pltpu.make_async_copy(h_hbm.at[pl.ds(0, rows), :], xbuf.at[sl], gsem.at[sl]).wait()

    @pl.when(i == 0)
    def _():
        q0 = first_pos(0)

        def body(r, _):
            gather_row(q0, r, 0).start()
            return 0
        lax.fori_loop(0, rows, body, 0)

    @pl.when(i < nused)
    def _():
        e = be_ref[i]
        changed = jnp.logical_or(i == 0, e != be_ref[jnp.maximum(i - 1, 0)])

        @pl.when(changed)
        def _():
            wgb[...] = wg_ref[...].astype(BF16)
            wub[...] = wu_ref[...].astype(BF16)
            wdb[...] = wd_ref[...].astype(BF16)

        wait_gather(slot)
        vbuf[...] = _rms(xbuf[slot], nf_ref[...]).astype(BF16)

        q_next = first_pos(jnp.minimum(i + 1, pl.num_programs(0) - 1))
        for r in range(rows):
            gather_row(q_next, r, 1 - slot).start(priority=r % 2)

        v = vbuf[...]
        hg = jnp.dot(v, wgb[...], preferred_element_type=F32)
        hu = jnp.dot(v, wub[...], preferred_element_type=F32)
        a = (hg * _sigmoid(hg) * hu).astype(BF16)
        o_ref[...] = jnp.dot(a, wdb[...], preferred_element_type=F32)

        @pl.when(i == nused - 1)
        def _():
            wait_gather(1 - slot)

    @pl.when(i >= nused)
    def _():
        o_ref[...] = jnp.zeros_like(o_ref)


def experts(block_e, qbase, nused, order, h1, nf, w_gate, w_up, w_down, *, rows=EXPERT_ROWS):
    n, d = h1.shape
    ne, _, ff = w_gate.shape
    nblk = block_e.shape[0]
    grid_spec = pltpu.PrefetchScalarGridSpec(
        num_scalar_prefetch=4,
        grid=(nblk,),
        in_specs=[
            pl.BlockSpec(memory_space=pl.ANY),
            pl.BlockSpec((1, d), lambda i, *_: (0, 0)),
            pl.BlockSpec((None, d, ff), lambda i, be, *_: (be[i], 0, 0)),
            pl.BlockSpec((None, d, ff), lambda i, be, *_: (be[i], 0, 0)),
            pl.BlockSpec((None, ff, d), lambda i, be, *_: (be[i], 0, 0)),
        ],
        out_specs=pl.BlockSpec((rows, d), lambda i, *_: (i, 0)),
        scratch_shapes=[
            pltpu.VMEM((2, rows, d), F32),
            pltpu.VMEM((rows, d), BF16),
            pltpu.VMEM((d, ff), BF16),
            pltpu.VMEM((d, ff), BF16),
            pltpu.VMEM((ff, d), BF16),
            pltpu.SemaphoreType.DMA((2,)),
        ],
    )
    return pl.pallas_call(
        functools.partial(_expert_kernel, rows=rows),
        grid_spec=grid_spec,
        out_shape=jax.ShapeDtypeStruct((nblk * rows, d), F32),
        compiler_params=_cparams(("arbitrary",)),
        name="experts",
    )(block_e, qbase, nused, order, h1, nf, w_gate, w_up, w_down)


def _dispatch_plan(eid, rank, counts, rows):
    n_experts = counts.shape[0]
    m = eid.size
    e_flat = eid.reshape(m)
    order = jnp.pad(jnp.argsort(e_flat).astype(jnp.int32), (0, rows))
    nblk_e = (counts + rows - 1) // rows
    bend = jnp.cumsum(nblk_e)
    bstart = bend - nblk_e
    start = jnp.cumsum(counts) - counts
    nblk = (m + n_experts * (rows - 1) + rows - 1) // rows
    bi = jnp.arange(nblk, dtype=jnp.int32)
    block_e = jnp.minimum(jnp.sum((bi[:, None] >= bend[None, :]).astype(jnp.int32), axis=1),
                          n_experts - 1)
    qbase = (start[block_e] + (bi - bstart[block_e]) * rows).astype(jnp.int32)
    nused = bend[-1:].astype(jnp.int32)
    assert nblk <= 256
    first_blk = jnp.dot(jax.nn.one_hot(e_flat, n_experts, dtype=BF16),
                        bstart.astype(F32).astype(BF16)[:, None],
                        preferred_element_type=F32)[:, 0]
    dest = (first_blk.astype(jnp.int32) * rows + rank.reshape(m)).astype(jnp.int32)
    return block_e, qbase, nused, order, dest


def _combine_kernel(dest_ref, h_ref, y_hbm, gate_ref, g_ref, o_ref, ybuf, sem, *, tm):
    i = pl.program_id(0)
    slot = lax.rem(i, 2)

    def row_copy(tile, t, k, sl):
        row = dest_ref[(tile * tm + t) * 2 + k]
        return pltpu.make_async_copy(y_hbm.at[pl.ds(row, 1), :], ybuf.at[sl, k, pl.ds(t, 1), :],
                                     sem.at[sl])

    def wait_rows(sl):
        for k in range(2):
            pltpu.make_async_copy(y_hbm.at[pl.ds(0, tm), :], ybuf.at[sl, k], sem.at[sl]).wait()

    @pl.when(i == 0)
    def _():
        def body(t, _):
            row_copy(0, t, 0, 0).start()
            row_copy(0, t, 1, 0).start()
            return 0
        lax.fori_loop(0, tm, body, 0)

    wait_rows(slot)
    nxt = jnp.minimum(i + 1, pl.num_programs(0) - 1)
    for t in range(tm):
        for k in range(2):
            row_copy(nxt, t, k, 1 - slot).start(priority=k)
    gate = gate_ref[...]
    h = h_ref[...] + gate[:, 0:1] * ybuf[slot, 0] + gate[:, 1:2] * ybuf[slot, 1]
    o_ref[...] = _rms(h, g_ref[...])

    @pl.when(i == pl.num_programs(0) - 1)
    def _():
        wait_rows(1 - slot)


def combine(dest, h1, y, gate, g, *, tm=256):
    n, d = h1.shape
    tm = min(tm, n)
    grid_spec = pltpu.PrefetchScalarGridSpec(
        num_scalar_prefetch=1,
        grid=(n // tm,),
        in_specs=[
            pl.BlockSpec((tm, d), lambda i, ds: (i, 0)),
            pl.BlockSpec(memory_space=pl.ANY),
            pl.BlockSpec((tm, gate.shape[1]), lambda i, ds: (i, 0)),
            pl.BlockSpec((1, d), lambda i, ds: (0, 0)),
        ],
        out_specs=pl.BlockSpec((tm, d), lambda i, ds: (i, 0)),
        scratch_shapes=[pltpu.VMEM((2, 2, tm, d), F32), pltpu.SemaphoreType.DMA((2,))],
    )
    return pl.pallas_call(
        functools.partial(_combine_kernel, tm=tm),
        grid_spec=grid_spec,
        out_shape=jax.ShapeDtypeStruct((n, d), F32),
        compiler_params=_cparams(("arbitrary",)),
        name="combine",
    )(dest, h1, y, gate, g)


def kernel(x, meta, norm_mix, w_in, pool_w, pool_scale, ssm_a_re, ssm_a_im, ssm_log_dt, ssm_b_re,
           ssm_b_im, ssm_c_re, ssm_c_im, ssm_d, w_glu, b_glu, w_branch_pool, w_branch_ssm, w_out,
           norm_ffn, w_router_group, b_router_group, w_router_expert, b_router_expert, w_gate,
           w_up, w_down, norm_final):
    assert w_in.shape[0] == 1, "single layer"
    bsz, seq, d = x.shape
    n = bsz * seq
    q = pool_scale.shape[1]
    assert meta.shape[0] == N_META == SSM_CHUNK and N_META >= max(POOL_WINDOWS)
    nc = seq // SSM_CHUNK
    assert nc & (nc - 1) == 0 and nc % SSM_ROWS == 0

    xf = x.reshape(n, d)
    g_mix = norm_mix[0][None, :]
    w_in_b = w_in[0].astype(BF16)

    zps = rms_matmul(xf, g_mix, w_in_b, col0=0, cols=2 * q, act=False, out_dtype=F32)
    zps_meta = rms_matmul(meta.astype(F32), g_mix, w_in_b, col0=0, cols=2 * q, act=False,
                          out_dtype=F32)
    zps_meta = jnp.pad(zps_meta, ((0, SSM_ROWS * SSM_CHUNK - N_META), (0, 0)))
    gates = rms_matmul(xf, g_mix, w_in_b, col0=2 * q, cols=2 * d, act=True, out_dtype=BF16)

    pool_y = pool_mixer(zps.reshape(bsz, seq, 2 * q), zps_meta, pool_w[0].astype(BF16),
                        pool_scale[0][None, :]).reshape(n, q)

    ssm_ops = _ssm_operators(ssm_a_re[0], ssm_a_im[0], ssm_log_dt[0], ssm_b_re[0], ssm_b_im[0],
                             ssm_c_re[0], ssm_c_im[0], nc)
    ssm_y = ssm_chunked(zps, zps_meta, *ssm_ops, bsz=bsz, seq=seq, col0=q)

    n_groups = w_router_group.shape[2]
    n_exp = w_router_expert.shape[2]
    w_r = jnp.concatenate([w_router_group[0], w_router_expert[0]], axis=1)
    w_r = jnp.pad(w_r, ((0, 0), (0, ROUTER_LANES - n_groups - n_exp))).astype(BF16)
    b_r = jnp.concatenate([b_router_group[0], b_router_expert[0]])
    b_r = jnp.pad(b_r, (0, ROUTER_LANES - n_groups - n_exp))[None, :].astype(F32)
    h1, logits = merge(pool_y, ssm_y, zps, gates, xf, ssm_d[0][None, :], w_glu[0].astype(BF16),
                       b_glu[0][None, :], w_branch_pool[0].astype(BF16),
                       w_branch_ssm[0].astype(BF16), w_out[0].astype(BF16),
                       norm_ffn[0][None, :], w_r, b_r)

    gate, eid, rank, cnt = route(logits)
    counts = cnt[0, n_groups:n_groups + n_exp].astype(jnp.int32)
    block_e, qbase, nused, order, dest = _dispatch_plan(eid[:, :2], rank[:, :2], counts,
                                                        EXPERT_ROWS)
    y = experts(block_e, qbase, nused, order, h1, norm_ffn[0][None, :], w_gate[0], w_up[0],
                w_down[0])
    out = combine(dest, h1, y, gate, norm_final[None, :])
    return out.reshape(bsz, seq, d)
```

```python
import functools
import math

import jax
import jax.numpy as jnp
from jax import lax
from jax.experimental import pallas as pl
from jax.experimental.pallas import tpu as pltpu

F32 = jnp.float32
BF16 = jnp.bfloat16

RMS_EPS = 1e-6
POOL_WINDOWS = (2, 4, 8, 16)
N_META = 16
LANES = 128
SSM_CHUNK = 16
SSM_GROUP_CH = 16
SSM_STATE = 64
SSM_OCTET = LANES // SSM_GROUP_CH
SSM_ROWS = 16
SSM_UNROLL = 4
MOE_GROUPS = 8
EXPERTS_PER_GROUP = 8
ROUTER_LANES = LANES
EXPERT_ROWS = 256
VMEM_LIMIT = 56 * 1024 * 1024


def _cparams(sem, vmem=VMEM_LIMIT):
    return pltpu.CompilerParams(dimension_semantics=sem, vmem_limit_bytes=vmem)


def _sigmoid(x):
    return 1.0 / (1.0 + jnp.exp(-x))


def _gelu_tanh(x):
    c = math.sqrt(2.0 / math.pi)
    return 0.5 * x * (1.0 + jnp.tanh(c * (x + 0.044715 * (x * x * x))))


def _rms(x, g):
    ms = jnp.mean(x * x, axis=-1, keepdims=True)
    return x * lax.rsqrt(ms + RMS_EPS) * g


def _rms_matmul_kernel(x_ref, g_ref, w_ref, o_ref, u_ref, *, act):
    @pl.when(pl.program_id(1) == 0)
    def _():
        u_ref[...] = _rms(x_ref[...], g_ref[...]).astype(BF16)

    acc = jnp.dot(u_ref[...], w_ref[...], preferred_element_type=F32)
    if act:
        acc = _sigmoid(acc)
    o_ref[...] = acc.astype(o_ref.dtype)


def rms_matmul(x, g, w, *, col0, cols, act, out_dtype, tm=1024, tn=1024):
    n, d = x.shape
    tm = min(tm, n)
    cb = col0 // tn
    return pl.pallas_call(
        functools.partial(_rms_matmul_kernel, act=act),
        grid=(n // tm, cols // tn),
        in_specs=[
            pl.BlockSpec((tm, d), lambda i, j: (i, 0)),
            pl.BlockSpec((1, d), lambda i, j: (0, 0)),
            pl.BlockSpec((d, tn), lambda i, j: (0, j + cb)),
        ],
        out_specs=pl.BlockSpec((tm, tn), lambda i, j: (i, j)),
        out_shape=jax.ShapeDtypeStruct((n, cols), out_dtype),
        scratch_shapes=[pltpu.VMEM((tm, d), BF16)],
        compiler_params=_cparams(("parallel", "arbitrary")),
        name="rms_matmul",
    )(x, g, w)


def _pool_kernel(z_ref, zm_ref, w_ref, sc_ref, o_ref, ext_ref, *, tl, gw):
    h = N_META

    @pl.when(pl.program_id(1) == 0)
    def _():
        ext_ref[0:h, :] = zm_ref[...]

    ext_ref[h:h + tl, :] = z_ref[...]
    for g, w in enumerate(POOL_WINDOWS):
        cols = slice(g * gw, (g + 1) * gw)
        u = ext_ref[h:h + tl, cols]
        s = u
        for k in range(1, w):
            s = s + ext_ref[h - k:h - k + tl, cols]
        d = s * (1.0 / w) - u
        y = jnp.dot(d.astype(BF16), w_ref[g], preferred_element_type=F32)
        o_ref[:, cols] = (y * sc_ref[:, cols]).astype(o_ref.dtype)
    ext_ref[0:h, :] = ext_ref[tl:tl + h, :]


def pool_mixer(z, zmeta, pool_w, pool_scale, *, tl=512):
    b, l, _ = z.shape
    ng, gw, _ = pool_w.shape
    c = ng * gw
    tl = min(tl, l)
    return pl.pallas_call(
        functools.partial(_pool_kernel, tl=tl, gw=gw),
        grid=(b, l // tl),
        in_specs=[
            pl.BlockSpec((None, tl, c), lambda i, t: (i, t, 0)),
            pl.BlockSpec((N_META, c), lambda i, t: (0, 0)),
            pl.BlockSpec((ng, gw, gw), lambda i, t: (0, 0, 0)),
            pl.BlockSpec((1, c), lambda i, t: (0, 0)),
        ],
        out_specs=pl.BlockSpec((None, tl, c), lambda i, t: (i, t, 0)),
        out_shape=jax.ShapeDtypeStruct((b, l, c), BF16),
        scratch_shapes=[pltpu.VMEM((tl + N_META, c), F32)],
        compiler_params=_cparams(("arbitrary", "arbitrary")),
        name="pool_mixer",
    )(z, zmeta, pool_w, pool_scale)


def _ssm_toeplitz_kernel(bt_ref, cl_ref, m_ref):
    w = m_ref.shape[-1]
    lane = lax.broadcasted_iota(jnp.int32, (SSM_GROUP_CH, w), 1)
    for g in range(bt_ref.shape[0]):
        kt = jnp.dot(bt_ref[g], cl_ref[g], preferred_element_type=F32,
                     precision=lax.Precision.HIGHEST)
        for s in range(SSM_CHUNK):
            off = s * SSM_GROUP_CH
            blk = kt if s == 0 else jnp.where(lane >= off, pltpu.roll(kt, off, axis=1), 0.0)
            m_ref[g, off:off + SSM_GROUP_CH, :] = blk.astype(m_ref.dtype)


def ssm_toeplitz(bt, cl):
    g, hh, k2 = bt.shape
    w = cl.shape[2]
    gb = SSM_OCTET
    return pl.pallas_call(
        _ssm_toeplitz_kernel,
        grid=(g // gb,),
        in_specs=[pl.BlockSpec((gb, hh, k2), lambda i: (i, 0, 0)),
                  pl.BlockSpec((gb, k2, w), lambda i: (i, 0, 0))],
        out_specs=pl.BlockSpec((gb, w, w), lambda i: (i, 0, 0)),
        out_shape=jax.ShapeDtypeStruct((g, w, w), BF16),
        compiler_params=_cparams(("parallel",)),
        name="ssm_toeplitz",
    )(bt, cl)


def _block_transpose8(a, lane):
    for d in (4, 2, 1):
        w = d * SSM_GROUP_CH
        keep = (lane & w) == 0
        b = list(a)
        for k in range(SSM_OCTET):
            if k & d == 0:
                x, y = a[k], a[k + d]
                b[k] = jnp.where(keep, x, pltpu.roll(y, w, axis=1))
                b[k + d] = jnp.where(keep, pltpu.roll(x, LANES - w, axis=1), y)
        a = b
    return a


def _ssm_kernel(z_ref, zm_ref, m_ref, pre_ref, pim_ref, q_ref, ar_ref, ai_ref, o_ref, uscr, umscr,
                yscr, *, nc, nsteps):
    rt = SSM_ROWS
    t = SSM_CHUNK
    lane = lax.broadcasted_iota(jnp.int32, (rt, LANES), 1)

    def regroup(src_ref, dst_ref, it):
        base = pl.multiple_of(it * (rt * t), rt * t)
        row0 = pl.multiple_of(it * rt, rt)
        for half in range(2):
            a = [src_ref[pl.ds(base + half * SSM_OCTET + k, rt, stride=t), :]
                 for k in range(SSM_OCTET)]
            b = _block_transpose8(a, lane)
            for j in range(SSM_OCTET):
                dst_ref[j, pl.ds(row0, rt), half * LANES:(half + 1) * LANES] = b[j].astype(BF16)

    regroup(zm_ref, umscr, 0)

    def fwd(it, _):
        regroup(z_ref, uscr, it)
        return 0
    lax.fori_loop(0, nc // rt, fwd, 0, unroll=SSM_UNROLL)

    c = lax.broadcasted_iota(jnp.int32, (nc, 2 * SSM_STATE), 0)

    def per_pair(pi, _):
        j0 = 2 * pi
        j1 = j0 + 1
        u0 = uscr[j0]
        u1 = uscr[j1]
        up = jnp.concatenate([u0, u1], axis=1)
        ump = jnp.concatenate([umscr[j0], umscr[j1]], axis=1)
        pre = pre_ref[pi]
        pim = pim_ref[pi]
        s_re = jnp.dot(up, pre, preferred_element_type=F32)
        s_im = jnp.dot(up, pim, preferred_element_type=F32)
        x0_re = jnp.dot(ump, pre, preferred_element_type=F32)[0:1]
        x0_im = jnp.dot(ump, pim, preferred_element_type=F32)[0:1]
        xr = jnp.where(c == 0, x0_re, pltpu.roll(s_re, 1, axis=0))
        xi = jnp.where(c == 0, x0_im, pltpu.roll(s_im, 1, axis=0))
        d = 1
        for k in range(nsteps):
            zr = jnp.where(c >= d, pltpu.roll(xr, d, axis=0), 0.0)
            zi = jnp.where(c >= d, pltpu.roll(xi, d, axis=0), 0.0)
            a = ar_ref[pi, k:k + 1, :]
            b = ai_ref[pi, k:k + 1, :]
            xr, xi = xr + a * zr - b * zi, xi + a * zi + b * zr
            d *= 2
        xc = jnp.concatenate([xr, xi], axis=1).astype(BF16)
        y_inter = jnp.dot(xc, q_ref[pi], preferred_element_type=F32)
        w = u0.shape[1]
        y0 = jnp.dot(u0, m_ref[j0], preferred_element_type=F32) + y_inter[:, :w]
        y1 = jnp.dot(u1, m_ref[j1], preferred_element_type=F32) + y_inter[:, w:]
        yscr[j0] = y0
        yscr[j1] = y1
        return 0
    lax.fori_loop(0, SSM_OCTET // 2, per_pair, 0)

    def bwd(it, _):
        base = pl.multiple_of(it * (rt * t), rt * t)
        row0 = pl.multiple_of(it * rt, rt)
        for half in range(2):
            b = [yscr[j, pl.ds(row0, rt), half * LANES:(half + 1) * LANES]
                 for j in range(SSM_OCTET)]
            a = _block_transpose8(b, lane)
            for k in range(SSM_OCTET):
                o_ref[pl.ds(base + half * SSM_OCTET + k, rt, stride=t), :] = a[k]
        return 0
    lax.fori_loop(0, nc // rt, bwd, 0, unroll=SSM_UNROLL)


def ssm_chunked(z, zm, m, pre, pim, q, ar, ai, *, bsz, seq, col0):
    g, w, _ = m.shape
    ns2 = pre.shape[2]
    nsteps = ar.shape[1]
    nc = seq // SSM_CHUNK
    gb = SSM_OCTET
    pb = gb // 2
    cb = col0 // LANES
    return pl.pallas_call(
        functools.partial(_ssm_kernel, nc=nc, nsteps=nsteps),
        grid=(g // gb, bsz),
        in_specs=[
            pl.BlockSpec((seq, LANES), lambda j, b: (b, cb + j)),
            pl.BlockSpec((SSM_ROWS * SSM_CHUNK, LANES), lambda j, b: (0, cb + j)),
            pl.BlockSpec((gb, w, w), lambda j, b: (j, 0, 0)),
            pl.BlockSpec((pb, 2 * w, ns2), lambda j, b: (j, 0, 0)),
            pl.BlockSpec((pb, 2 * w, ns2), lambda j, b: (j, 0, 0)),
            pl.BlockSpec((pb, 2 * ns2, 2 * w), lambda j, b: (j, 0, 0)),
            pl.BlockSpec((pb, nsteps, ns2), lambda j, b: (j, 0, 0)),
            pl.BlockSpec((pb, nsteps, ns2), lambda j, b: (j, 0, 0)),
        ],
        out_specs=pl.BlockSpec((seq, LANES), lambda j, b: (b, j)),
        out_shape=jax.ShapeDtypeStruct((bsz * seq, g * SSM_GROUP_CH), F32),
        scratch_shapes=[
            pltpu.VMEM((gb, nc, w), BF16),
            pltpu.VMEM((gb, SSM_ROWS, w), BF16),
            pltpu.VMEM((gb, nc, w), F32),
        ],
        compiler_params=_cparams(("parallel", "arbitrary")),
        name="ssm_chunked",
    )(z, zm, m, pre, pim, q, ar, ai)


def _ssm_operators(a_re, a_im, log_dt, b_re, b_im, c_re, c_im, nc):
    t = SSM_CHUNK
    ar = a_re.astype(F32)
    ai = a_im.astype(F32)
    dt = jnp.exp(log_dt.astype(F32))[:, None]
    mag = jnp.exp(ar * dt)
    lam_re = mag * jnp.cos(ai * dt)
    lam_im = mag * jnp.sin(ai * dt)
    den = ar * ar + ai * ai
    nr = lam_re - 1.0
    coef_re = (nr * ar + lam_im * ai) / den
    coef_im = (lam_im * ar - nr * ai) / den
    br = b_re.astype(F32)
    bi = b_im.astype(F32)
    bb_re = coef_re[..., None] * br - coef_im[..., None] * bi
    bb_im = coef_re[..., None] * bi + coef_im[..., None] * br
    crt = c_re.astype(F32).transpose(0, 2, 1)
    cit = c_im.astype(F32).transpose(0, 2, 1)

    def lam_pow(k):
        k = k.astype(F32)
        m = jnp.exp(ar[..., None] * dt[..., None] * k)
        ang = ai[..., None] * dt[..., None] * k
        return m * jnp.cos(ang), m * jnp.sin(ang)

    g = ar.shape[0]
    hh = SSM_GROUP_CH
    pr, pi = lam_pow(jnp.arange(t + 1))

    def times_c(xr, xi):
        return (xr[..., None] * crt[:, :, None, :] - xi[..., None] * cit[:, :, None, :],
                xr[..., None] * cit[:, :, None, :] + xi[..., None] * crt[:, :, None, :])

    cl_re, cl_im = times_c(pr[:, :, :t], pi[:, :, :t])
    cl = jnp.concatenate([cl_re.reshape(g, -1, t * hh), cl_im.reshape(g, -1, t * hh)], axis=1)
    bt = jnp.concatenate([bb_re.transpose(0, 2, 1), -bb_im.transpose(0, 2, 1)], axis=-1)
    m_op = ssm_toeplitz(bt, cl)
    rr = pr[:, :, t - 1::-1][:, :, :t]
    ri = pi[:, :, t - 1::-1][:, :, :t]
    p_re = rr[..., None] * bb_re[:, :, None, :] - ri[..., None] * bb_im[:, :, None, :]
    p_im = rr[..., None] * bb_im[:, :, None, :] + ri[..., None] * bb_re[:, :, None, :]
    p_re = p_re.transpose(0, 2, 3, 1).reshape(g, t * hh, -1)
    p_im = p_im.transpose(0, 2, 3, 1).reshape(g, t * hh, -1)
    q_re, q_im = times_c(pr[:, :, 1:], pi[:, :, 1:])
    q_re = q_re.reshape(g, -1, t * hh)
    q_im = -q_im.reshape(g, -1, t * hh)
    nsteps = int(math.log2(nc))
    sr, si = lam_pow(t * (2 ** jnp.arange(nsteps)))

    def pair_rows(x):
        z = jnp.zeros_like(x[0::2])
        return jnp.concatenate([jnp.concatenate([x[0::2], z], axis=-1),
                                jnp.concatenate([z, x[1::2]], axis=-1)], axis=1)

    def pair_lanes(x):
        x = x.transpose(0, 2, 1)
        return jnp.concatenate([x[0::2], x[1::2]], axis=-1)

    pre_op = pair_rows(p_re).astype(BF16)
    pim_op = pair_rows(p_im).astype(BF16)
    q_op = jnp.concatenate([pair_rows(q_re), pair_rows(q_im)], axis=1).astype(BF16)
    return m_op, pre_op, pim_op, q_op, pair_lanes(sr), pair_lanes(si)


def _merge_kernel(p_ref, y_ref, us_ref, gp_ref, gs_ref, h_ref, dv_ref, wglu_ref, bglu_ref,
                  wbp_ref, wbs_ref, wout_ref, nf_ref, wr_ref, br_ref, h1_ref, lg_ref):
    s = _gelu_tanh(y_ref[...] + dv_ref[...] * us_ref[...])
    gl = jnp.dot(s.astype(BF16), wglu_ref[...], preferred_element_type=F32) + bglu_ref[...]
    s = s * _sigmoid(gl)
    y_ssm = jnp.dot(s.astype(BF16), wbs_ref[...], preferred_element_type=F32)
    y_pool = jnp.dot(p_ref[...], wbp_ref[...], preferred_element_type=F32)
    merged = gp_ref[...].astype(F32) * y_pool + gs_ref[...].astype(F32) * y_ssm
    h1 = h_ref[...] + jnp.dot(merged.astype(BF16), wout_ref[...], preferred_element_type=F32)
    h1_ref[...] = h1
    v = _rms(h1, nf_ref[...]).astype(BF16)
    lg_ref[...] = jnp.dot(v, wr_ref[...], preferred_element_type=F32) + br_ref[...]


def merge(pool_y, ssm_y, zps, gates, h0, dvec, w_glu, b_glu, w_bp, w_bs, w_out, nf, w_r, b_r,
          *, tm=256):
    n, d = h0.shape
    q = pool_y.shape[1]
    tm = min(tm, n)
    const = lambda shape: pl.BlockSpec(shape, lambda i: (0,) * len(shape),
                                       pipeline_mode=pl.Buffered(1))
    return pl.pallas_call(
        _merge_kernel,
        grid=(n // tm,),
        in_specs=[
            pl.BlockSpec((tm, q), lambda i: (i, 0)),
            pl.BlockSpec((tm, q), lambda i: (i, 0)),
            pl.BlockSpec((tm, q), lambda i: (i, 1)),
            pl.BlockSpec((tm, d), lambda i: (i, 0)),
            pl.BlockSpec((tm, d), lambda i: (i, 1)),
            pl.BlockSpec((tm, d), lambda i: (i, 0)),
            const((1, q)), const((q, q)), const((1, q)),
            const((q, d)), const((q, d)), const((d, d)),
            const((1, d)), const((d, ROUTER_LANES)), const((1, ROUTER_LANES)),
        ],
        out_specs=[
            pl.BlockSpec((tm, d), lambda i: (i, 0)),
            pl.BlockSpec((tm, ROUTER_LANES), lambda i: (i, 0)),
        ],
        out_shape=[
            jax.ShapeDtypeStruct((n, d), F32),
            jax.ShapeDtypeStruct((n, ROUTER_LANES), F32),
        ],
        compiler_params=_cparams(("parallel",)),
        name="merge",
    )(pool_y, ssm_y, zps, gates, gates, h0, dvec, w_glu, b_glu, w_bp, w_bs, w_out, nf, w_r, b_r)


def _route_kernel(lg_ref, gate_ref, eid_ref, rank_ref, cnt_ref, tri_ref, carry_ref):
    step = pl.program_id(0)
    tm = lg_ref.shape[0]

    @pl.when(step == 0)
    def _():
        rr = lax.broadcasted_iota(jnp.int32, (tm, tm), 0)
        cc = lax.broadcasted_iota(jnp.int32, (tm, tm), 1)
        tri_ref[...] = jnp.where(cc < rr, 1.0, 0.0).astype(BF16)
        carry_ref[...] = jnp.zeros_like(carry_ref)

    l = lg_ref[...]
    lane = lax.broadcasted_iota(jnp.int32, l.shape, 1).astype(F32)
    neg = -jnp.inf
    big = float(ROUTER_LANES)
    is_g = lane < MOE_GROUPS
    gl = jnp.where(is_g, l, neg)
    gmax = jnp.max(gl, axis=1, keepdims=True)
    gstar = jnp.min(jnp.where(gl == gmax, lane, big), axis=1, keepdims=True)
    pg = 1.0 / jnp.sum(jnp.where(is_g, jnp.exp(gl - gmax), 0.0), axis=1, keepdims=True)
    lo = MOE_GROUPS + EXPERTS_PER_GROUP * gstar
    is_e = (lane >= lo) & (lane < lo + EXPERTS_PER_GROUP)
    el = jnp.where(is_e, l, neg)
    m1 = jnp.max(el, axis=1, keepdims=True)
    i1 = jnp.min(jnp.where(is_e & (el == m1), lane, big), axis=1, keepdims=True)
    is_e2 = is_e & (lane != i1)
    el2 = jnp.where(is_e2, l, neg)
    m2 = jnp.max(el2, axis=1, keepdims=True)
    i2 = jnp.min(jnp.where(is_e2 & (el2 == m2), lane, big), axis=1, keepdims=True)
    t = jnp.exp(m2 - m1)
    w1 = pg / (1.0 + t)
    w2 = pg * t / (1.0 + t)
    gate_ref[...] = jnp.where(lane == 0.0, w1, jnp.where(lane == 1.0, w2, 0.0))
    eid = jnp.where(lane == 0.0, i1, jnp.where(lane == 1.0, i2, float(MOE_GROUPS)))
    eid_ref[...] = (eid - MOE_GROUPS).astype(jnp.int32)
    oh1 = lane == i1
    oh2 = lane == i2
    both = jnp.where(oh1 | oh2, 1.0, 0.0)
    before = carry_ref[...] + jnp.dot(tri_ref[...], both.astype(BF16),
                                      preferred_element_type=F32)
    r1 = jnp.sum(jnp.where(oh1, before, 0.0), axis=1, keepdims=True)
    r2 = jnp.sum(jnp.where(oh2, before, 0.0), axis=1, keepdims=True)
    rank_ref[...] = jnp.where(lane == 0.0, r1, jnp.where(lane == 1.0, r2, 0.0)).astype(jnp.int32)
    carry_ref[...] = carry_ref[...] + jnp.sum(both, axis=0, keepdims=True)
    cnt_ref[...] = carry_ref[...]


def route(logits, *, tm=1024):
    n, w = logits.shape
    tm = min(tm, n)
    tok = pl.BlockSpec((tm, w), lambda i: (i, 0))
    return pl.pallas_call(
        _route_kernel,
        grid=(n // tm,),
        in_specs=[tok],
        out_specs=[tok, tok, tok, pl.BlockSpec((1, w), lambda i: (0, 0))],
        out_shape=[jax.ShapeDtypeStruct((n, w), F32), jax.ShapeDtypeStruct((n, w), jnp.int32),
                   jax.ShapeDtypeStruct((n, w), jnp.int32), jax.ShapeDtypeStruct((1, w), F32)],
        scratch_shapes=[pltpu.VMEM((tm, tm), BF16), pltpu.VMEM((1, w), F32)],
        compiler_params=_cparams(("arbitrary",)),
        name="route",
    )(logits)


def _expert_kernel(be_ref, first_ref, par_ref, nexte_ref, qb_ref, nu_ref, ord_ref, h_hbm, nf_ref,
                   wg_hbm, wu_hbm, wd_hbm, o_ref, xbuf, vbuf, sg, su, sd, wgb, wub, wdb, gsem,
                   wsem, *, rows):
    i = pl.program_id(0)
    nused = nu_ref[0]
    slot = lax.rem(i, 2)
    qmax = ord_ref.shape[0] - rows

    def first_pos(blk):
        return jnp.clip(qb_ref[blk], 0, qmax)

    def gather_row(q0, r, sl):
        tok = ord_ref[q0 + r] >> 1
        return pltpu.make_async_copy(h_hbm.at[pl.ds(tok, 1), :], xbuf.at[sl, pl.ds(r, 1), :],
                                     gsem.at[sl])

    def wait_gather(sl):
        pltpu.make_async_copy(h_hbm.at[pl.ds(0, rows), :], xbuf.at[sl], gsem.at[sl]).wait()

    @pl.when(i == 0)
    def _():
        q0 = first_pos(0)

        def body(r, _):
            gather_row(q0, r, 0).start()
            return 0
        lax.fori_loop(0, rows, body, 0)

    def weight_copies(e, p):
        return (pltpu.make_async_copy(wg_hbm.at[e], sg.at[p], wsem.at[p]),
                pltpu.make_async_copy(wu_hbm.at[e], su.at[p], wsem.at[p]),
                pltpu.make_async_copy(wd_hbm.at[e], sd.at[p], wsem.at[p]))

    @pl.when(i == 0)
    def _():
        for cp in weight_copies(be_ref[0], 0):
            cp.start()

    @pl.when(i < nused)
    def _():
        @pl.when(first_ref[i] == 1)
        def _():
            p = par_ref[i]
            for cp in weight_copies(be_ref[i], p):
                cp.wait()
            wgb[...] = sg[p].astype(BF16)
            wub[...] = su[p].astype(BF16)
            wdb[...] = sd[p].astype(BF16)

            @pl.when(nexte_ref[i] >= 0)
            def _():
                for cp in weight_copies(nexte_ref[i], 1 - p):
                    cp.start()

        wait_gather(slot)
        vbuf[...] = _rms(xbuf[slot], nf_ref[...]).astype(BF16)

        q_next = first_pos(jnp.minimum(i + 1, pl.num_programs(0) - 1))
        for r in range(rows):
            gather_row(q_next, r, 1 - slot).start(priority=r % 2)

        v = vbuf[...]
        hg = jnp.dot(v, wgb[...], preferred_element_type=F32)
        hu = jnp.dot(v, wub[...], preferred_element_type=F32)
        a = (hg * _sigmoid(hg) * hu).astype(BF16)
        o_ref[...] = jnp.dot(a, wdb[...], preferred_element_type=F32)

        @pl.when(i == nused - 1)
        def _():
            wait_gather(1 - slot)

    @pl.when(i >= nused)
    def _():
        o_ref[...] = jnp.zeros_like(o_ref)


def experts(plan, h1, nf, w_gate, w_up, w_down, *, rows=EXPERT_ROWS):
    n, d = h1.shape
    ne, _, ff = w_gate.shape
    nblk = plan[0].shape[0]
    hbm = pl.BlockSpec(memory_space=pl.ANY)
    grid_spec = pltpu.PrefetchScalarGridSpec(
        num_scalar_prefetch=len(plan),
        grid=(nblk,),
        in_specs=[hbm, pl.BlockSpec((1, d), lambda i, *_: (0, 0)), hbm, hbm, hbm],
        out_specs=pl.BlockSpec((rows, d), lambda i, *_: (i, 0)),
        scratch_shapes=[
            pltpu.VMEM((2, rows, d), F32),
            pltpu.VMEM((rows, d), BF16),
            pltpu.VMEM((2, d, ff), F32),
            pltpu.VMEM((2, d, ff), F32),
            pltpu.VMEM((2, ff, d), F32),
            pltpu.VMEM((d, ff), BF16),
            pltpu.VMEM((d, ff), BF16),
            pltpu.VMEM((ff, d), BF16),
            pltpu.SemaphoreType.DMA((2,)),
            pltpu.SemaphoreType.DMA((2,)),
        ],
    )
    return pl.pallas_call(
        functools.partial(_expert_kernel, rows=rows),
        grid_spec=grid_spec,
        out_shape=jax.ShapeDtypeStruct((nblk * rows, d), F32),
        compiler_params=_cparams(("arbitrary",)),
        name="experts",
    )(*plan, h1, nf, w_gate, w_up, w_down)


def _dispatch_plan(eid, rank, counts, rows):
    n_experts = counts.shape[0]
    m = eid.size
    e_flat = eid.reshape(m)
    order = jnp.pad(jnp.argsort(e_flat).astype(jnp.int32), (0, rows))
    nblk_e = (counts + rows - 1) // rows
    bend = jnp.cumsum(nblk_e)
    bstart = bend - nblk_e
    start = jnp.cumsum(counts) - counts
    nblk = (m + n_experts * (rows - 1) + rows - 1) // rows
    bi = jnp.arange(nblk, dtype=jnp.int32)
    block_e = jnp.minimum(jnp.sum((bi[:, None] >= bend[None, :]).astype(jnp.int32), axis=1),
                          n_experts - 1)
    qbase = (start[block_e] + (bi - bstart[block_e]) * rows).astype(jnp.int32)
    nused = bend[-1:].astype(jnp.int32)
    assert nblk <= 256
    first_blk = jnp.dot(jax.nn.one_hot(e_flat, n_experts, dtype=BF16),
                        bstart.astype(F32).astype(BF16)[:, None],
                        preferred_element_type=F32)[:, 0]
    dest = (first_blk.astype(jnp.int32) * rows + rank.reshape(m)).astype(jnp.int32)
    has = counts > 0
    first = ((bi == bstart[block_e]) & (bi < nused[0])).astype(jnp.int32)
    parity = ((jnp.cumsum(has.astype(jnp.int32)) - 1)[block_e] & 1).astype(jnp.int32)
    ids = jnp.arange(n_experts, dtype=jnp.int32)
    at_or_after = lax.cummin(jnp.where(has, ids, n_experts), reverse=True)
    after = jnp.concatenate([at_or_after[1:], jnp.full((1,), n_experts, jnp.int32)])
    next_expert = jnp.where(after < n_experts, after, -1)[block_e].astype(jnp.int32)
    return (block_e, first, parity, next_expert, qbase, nused, order), dest


def _combine_kernel(dest_ref, h_ref, y_hbm, gate_ref, g_ref, o_ref, ybuf, sem, *, tm):
    i = pl.program_id(0)
    slot = lax.rem(i, 2)

    def row_copy(tile, t, k, sl):
        row = dest_ref[(tile * tm + t) * 2 + k]
        return pltpu.make_async_copy(y_hbm.at[pl.ds(row, 1), :], ybuf.at[sl, k, pl.ds(t, 1), :],
                                     sem.at[sl])

    def wait_rows(sl):
        for k in range(2):
            pltpu.make_async_copy(y_hbm.at[pl.ds(0, tm), :], ybuf.at[sl, k], sem.at[sl]).wait()

    @pl.when(i == 0)
    def _():
        def body(t, _):
            row_copy(0, t, 0, 0).start()
            row_copy(0, t, 1, 0).start()
            return 0
        lax.fori_loop(0, tm, body, 0)

    wait_rows(slot)
    nxt = jnp.minimum(i + 1, pl.num_programs(0) - 1)
    for t in range(tm):
        for k in range(2):
            row_copy(nxt, t, k, 1 - slot).start(priority=k)
    gate = gate_ref[...]
    h = h_ref[...] + gate[:, 0:1] * ybuf[slot, 0] + gate[:, 1:2] * ybuf[slot, 1]
    o_ref[...] = _rms(h, g_ref[...])

    @pl.when(i == pl.num_programs(0) - 1)
    def _():
        wait_rows(1 - slot)


def combine(dest, h1, y, gate, g, *, tm=256):
    n, d = h1.shape
    tm = min(tm, n)
    grid_spec = pltpu.PrefetchScalarGridSpec(
        num_scalar_prefetch=1,
        grid=(n // tm,),
        in_specs=[
            pl.BlockSpec((tm, d), lambda i, ds: (i, 0)),
            pl.BlockSpec(memory_space=pl.ANY),
            pl.BlockSpec((tm, gate.shape[1]), lambda i, ds: (i, 0)),
            pl.BlockSpec((1, d), lambda i, ds: (0, 0)),
        ],
        out_specs=pl.BlockSpec((tm, d), lambda i, ds: (i, 0)),
        scratch_shapes=[pltpu.VMEM((2, 2, tm, d), F32), pltpu.SemaphoreType.DMA((2,))],
    )
    return pl.pallas_call(
        functools.partial(_combine_kernel, tm=tm),
        grid_spec=grid_spec,
        out_shape=jax.ShapeDtypeStruct((n, d), F32),
        compiler_params=_cparams(("arbitrary",)),
        name="combine",
    )(dest, h1, y, gate, g)


def kernel(x, meta, norm_mix, w_in, pool_w, pool_scale, ssm_a_re, ssm_a_im, ssm_log_dt, ssm_b_re,
           ssm_b_im, ssm_c_re, ssm_c_im, ssm_d, w_glu, b_glu, w_branch_pool, w_branch_ssm, w_out,
           norm_ffn, w_router_group, b_router_group, w_router_expert, b_router_expert, w_gate,
           w_up, w_down, norm_final):
    assert w_in.shape[0] == 1, "single layer"
    bsz, seq, d = x.shape
    n = bsz * seq
    q = pool_scale.shape[1]
    assert meta.shape[0] == N_META == SSM_CHUNK and N_META >= max(POOL_WINDOWS)
    nc = seq // SSM_CHUNK
    assert nc & (nc - 1) == 0 and nc % SSM_ROWS == 0

    xf = x.reshape(n, d)
    g_mix = norm_mix[0][None, :]
    w_in_b = w_in[0].astype(BF16)

    zps = rms_matmul(xf, g_mix, w_in_b, col0=0, cols=2 * q, act=False, out_dtype=F32)
    zps_meta = rms_matmul(meta.astype(F32), g_mix, w_in_b, col0=0, cols=2 * q, act=False,
                          out_dtype=F32)
    zps_meta = jnp.pad(zps_meta, ((0, SSM_ROWS * SSM_CHUNK - N_META), (0, 0)))
    gates = rms_matmul(xf, g_mix, w_in_b, col0=2 * q, cols=2 * d, act=True, out_dtype=BF16)

    pool_y = pool_mixer(zps.reshape(bsz, seq, 2 * q), zps_meta, pool_w[0].astype(BF16),
                        pool_scale[0][None, :]).reshape(n, q)

    ssm_ops = _ssm_operators(ssm_a_re[0], ssm_a_im[0], ssm_log_dt[0], ssm_b_re[0], ssm_b_im[0],
                             ssm_c_re[0], ssm_c_im[0], nc)
    ssm_y = ssm_chunked(zps, zps_meta, *ssm_ops, bsz=bsz, seq=seq, col0=q)

    n_groups = w_router_group.shape[2]
    n_exp = w_router_expert.shape[2]
    w_r = jnp.concatenate([w_router_group[0], w_router_expert[0]], axis=1)
    w_r = jnp.pad(w_r, ((0, 0), (0, ROUTER_LANES - n_groups - n_exp))).astype(BF16)
    b_r = jnp.concatenate([b_router_group[0], b_router_expert[0]])
    b_r = jnp.pad(b_r, (0, ROUTER_LANES - n_groups - n_exp))[None, :].astype(F32)
    h1, logits = merge(pool_y, ssm_y, zps, gates, xf, ssm_d[0][None, :], w_glu[0].astype(BF16),
                       b_glu[0][None, :], w_branch_pool[0].astype(BF16),
                       w_branch_ssm[0].astype(BF16), w_out[0].astype(BF16),
                       norm_ffn[0][None, :], w_r, b_r)

    gate, eid, rank, cnt = route(logits)
    counts = cnt[0, n_groups:n_groups + n_exp].astype(jnp.int32)
    plan, dest = _dispatch_plan(eid[:, :2], rank[:, :2], counts, EXPERT_ROWS)
    y = experts(plan, h1, norm_ffn[0][None, :], w_gate[0], w_up[0], w_down[0])
    out = combine(dest, h1, y, gate, norm_final[None, :])
    return out.reshape(bsz, seq, d)
```

```python
import functools
import math

import jax
import jax.numpy as jnp
from jax import lax
from jax.experimental import pallas as pl
from jax.experimental.pallas import tpu as pltpu

F32 = jnp.float32
BF16 = jnp.bfloat16

RMS_EPS = 1e-6
POOL_WINDOWS = (2, 4, 8, 16)
N_META = 16
LANES = 128
SSM_CHUNK = 16
SSM_GROUP_CH = 16
SSM_STATE = 64
SSM_OCTET = LANES // SSM_GROUP_CH
SSM_ROWS = 16
SSM_ROWS_BACK = 16
SSM_UNROLL_BACK = 2
SSM_UNROLL = 4
MOE_GROUPS = 8
EXPERTS_PER_GROUP = 8
ROUTER_LANES = LANES
EXPERT_ROWS = 256
VMEM_LIMIT = 56 * 1024 * 1024


def _cparams(sem, vmem=VMEM_LIMIT):
    return pltpu.CompilerParams(dimension_semantics=sem, vmem_limit_bytes=vmem)


def _sigmoid(x):
    return 1.0 / (1.0 + jnp.exp(-x))


def _gelu_tanh(x):
    c = math.sqrt(2.0 / math.pi)
    return 0.5 * x * (1.0 + jnp.tanh(c * (x + 0.044715 * (x * x * x))))


def _rms(x, g):
    ms = jnp.mean(x * x, axis=-1, keepdims=True)
    return x * lax.rsqrt(ms + RMS_EPS) * g


def _rms_matmul_kernel(x_ref, g_ref, w_ref, o_ref, u_ref, *, act):
    @pl.when(pl.program_id(1) == 0)
    def _():
        u_ref[...] = _rms(x_ref[...], g_ref[...]).astype(BF16)

    acc = jnp.dot(u_ref[...], w_ref[...], preferred_element_type=F32)
    if act:
        acc = _sigmoid(acc)
    o_ref[...] = acc.astype(o_ref.dtype)


def rms_matmul(x, g, w, *, col0, cols, act, out_dtype, tm=1024, tn=1024):
    n, d = x.shape
    tm = min(tm, n)
    cb = col0 // tn
    return pl.pallas_call(
        functools.partial(_rms_matmul_kernel, act=act),
        grid=(n // tm, cols // tn),
        in_specs=[
            pl.BlockSpec((tm, d), lambda i, j: (i, 0)),
            pl.BlockSpec((1, d), lambda i, j: (0, 0)),
            pl.BlockSpec((d, tn), lambda i, j: (0, j + cb)),
        ],
        out_specs=pl.BlockSpec((tm, tn), lambda i, j: (i, j)),
        out_shape=jax.ShapeDtypeStruct((n, cols), out_dtype),
        scratch_shapes=[pltpu.VMEM((tm, d), BF16)],
        compiler_params=_cparams(("parallel", "arbitrary")),
        name="rms_matmul",
    )(x, g, w)


def _pool_kernel(z_ref, zm_ref, w_ref, sc_ref, o_ref, ext_ref, *, tl, gw):
    h = N_META

    @pl.when(pl.program_id(1) == 0)
    def _():
        ext_ref[0:h, :] = zm_ref[...]

    ext_ref[h:h + tl, :] = z_ref[...]
    for g, w in enumerate(POOL_WINDOWS):
        cols = slice(g * gw, (g + 1) * gw)
        u = ext_ref[h:h + tl, cols]
        s = u
        for k in range(1, w):
            s = s + ext_ref[h - k:h - k + tl, cols]
        d = s * (1.0 / w) - u
        y = jnp.dot(d.astype(BF16), w_ref[g], preferred_element_type=F32)
        o_ref[:, cols] = (y * sc_ref[:, cols]).astype(o_ref.dtype)
    ext_ref[0:h, :] = ext_ref[tl:tl + h, :]


def pool_mixer(z, zmeta, pool_w, pool_scale, *, tl=512):
    b, l, _ = z.shape
    ng, gw, _ = pool_w.shape
    c = ng * gw
    tl = min(tl, l)
    return pl.pallas_call(
        functools.partial(_pool_kernel, tl=tl, gw=gw),
        grid=(b, l // tl),
        in_specs=[
            pl.BlockSpec((None, tl, c), lambda i, t: (i, t, 0)),
            pl.BlockSpec((N_META, c), lambda i, t: (0, 0)),
            pl.BlockSpec((ng, gw, gw), lambda i, t: (0, 0, 0)),
            pl.BlockSpec((1, c), lambda i, t: (0, 0)),
        ],
        out_specs=pl.BlockSpec((None, tl, c), lambda i, t: (i, t, 0)),
        out_shape=jax.ShapeDtypeStruct((b, l, c), BF16),
        scratch_shapes=[pltpu.VMEM((tl + N_META, c), F32)],
        compiler_params=_cparams(("arbitrary", "arbitrary")),
        name="pool_mixer",
    )(z, zmeta, pool_w, pool_scale)


def _ssm_toeplitz_kernel(bt_ref, cl_ref, m_ref):
    w = m_ref.shape[-1]
    lane = lax.broadcasted_iota(jnp.int32, (SSM_GROUP_CH, w), 1)
    for g in range(bt_ref.shape[0]):
        kt = jnp.dot(bt_ref[g], cl_ref[g], preferred_element_type=F32,
                     precision=lax.Precision.HIGHEST)
        for s in range(SSM_CHUNK):
            off = s * SSM_GROUP_CH
            blk = kt if s == 0 else jnp.where(lane >= off, pltpu.roll(kt, off, axis=1), 0.0)
            m_ref[g, off:off + SSM_GROUP_CH, :] = blk.astype(m_ref.dtype)


def ssm_toeplitz(bt, cl):
    g, hh, k2 = bt.shape
    w = cl.shape[2]
    gb = SSM_OCTET
    return pl.pallas_call(
        _ssm_toeplitz_kernel,
        grid=(g // gb,),
        in_specs=[pl.BlockSpec((gb, hh, k2), lambda i: (i, 0, 0)),
                  pl.BlockSpec((gb, k2, w), lambda i: (i, 0, 0))],
        out_specs=pl.BlockSpec((gb, w, w), lambda i: (i, 0, 0)),
        out_shape=jax.ShapeDtypeStruct((g, w, w), BF16),
        compiler_params=_cparams(("parallel",)),
        name="ssm_toeplitz",
    )(bt, cl)


def _block_transpose8(a, lane):
    for d in (4, 2, 1):
        w = d * SSM_GROUP_CH
        keep = (lane & w) == 0
        b = list(a)
        for k in range(SSM_OCTET):
            if k & d == 0:
                x, y = a[k], a[k + d]
                b[k] = jnp.where(keep, x, pltpu.roll(y, w, axis=1))
                b[k + d] = jnp.where(keep, pltpu.roll(x, LANES - w, axis=1), y)
        a = b
    return a


def _ssm_kernel(z_ref, zm_ref, m_ref, pre_ref, pim_ref, q_ref, ar_ref, ai_ref, o_ref, uscr, umscr,
                yscr, *, nc, nsteps):
    rt = SSM_ROWS
    t = SSM_CHUNK
    lane = lax.broadcasted_iota(jnp.int32, (rt, LANES), 1)

    def regroup(src_ref, dst_ref, it):
        base = pl.multiple_of(it * (rt * t), rt * t)
        row0 = pl.multiple_of(it * rt, rt)
        for half in range(2):
            a = [src_ref[pl.ds(base + half * SSM_OCTET + k, rt, stride=t), :]
                 for k in range(SSM_OCTET)]
            b = _block_transpose8(a, lane)
            for j in range(SSM_OCTET):
                dst_ref[j, pl.ds(row0, rt), half * LANES:(half + 1) * LANES] = b[j].astype(BF16)

    regroup(zm_ref, umscr, 0)

    def fwd(it, _):
        regroup(z_ref, uscr, it)
        return 0
    lax.fori_loop(0, nc // rt, fwd, 0, unroll=SSM_UNROLL)

    c = lax.broadcasted_iota(jnp.int32, (nc, 2 * SSM_STATE), 0)

    def per_pair(pi, _):
        j0 = 2 * pi
        j1 = j0 + 1
        u0 = uscr[j0]
        u1 = uscr[j1]
        up = jnp.concatenate([u0, u1], axis=1)
        ump = jnp.concatenate([umscr[j0], umscr[j1]], axis=1)
        pre = pre_ref[pi]
        pim = pim_ref[pi]
        s_re = jnp.dot(up, pre, preferred_element_type=F32)
        s_im = jnp.dot(up, pim, preferred_element_type=F32)
        x0_re = jnp.dot(ump, pre, preferred_element_type=F32)[0:1]
        x0_im = jnp.dot(ump, pim, preferred_element_type=F32)[0:1]
        xr = jnp.where(c == 0, x0_re, pltpu.roll(s_re, 1, axis=0))
        xi = jnp.where(c == 0, x0_im, pltpu.roll(s_im, 1, axis=0))
        d = 1
        for k in range(nsteps):
            zr = jnp.where(c >= d, pltpu.roll(xr, d, axis=0), 0.0)
            zi = jnp.where(c >= d, pltpu.roll(xi, d, axis=0), 0.0)
            a = ar_ref[pi, k:k + 1, :]
            b = ai_ref[pi, k:k + 1, :]
            xr, xi = xr + a * zr - b * zi, xi + a * zi + b * zr
            d *= 2
        xc = jnp.concatenate([xr, xi], axis=1).astype(BF16)
        y_inter = jnp.dot(xc, q_ref[pi], preferred_element_type=F32)
        w = u0.shape[1]
        y0 = jnp.dot(u0, m_ref[j0], preferred_element_type=F32) + y_inter[:, :w]
        y1 = jnp.dot(u1, m_ref[j1], preferred_element_type=F32) + y_inter[:, w:]
        yscr[j0] = y0
        yscr[j1] = y1
        return 0
    lax.fori_loop(0, SSM_OCTET // 2, per_pair, 0)

    rb = SSM_ROWS_BACK
    lane_b = lax.broadcasted_iota(jnp.int32, (rb, LANES), 1)

    def bwd(it, _):
        base = pl.multiple_of(it * (rb * t), rb * t)
        row0 = pl.multiple_of(it * rb, rb)
        for half in range(2):
            b = [yscr[j, pl.ds(row0, rb), half * LANES:(half + 1) * LANES]
                 for j in range(SSM_OCTET)]
            a = _block_transpose8(b, lane_b)
            for k in range(SSM_OCTET):
                o_ref[pl.ds(base + half * SSM_OCTET + k, rb, stride=t), :] = a[k]
        return 0
    lax.fori_loop(0, nc // rb, bwd, 0, unroll=SSM_UNROLL_BACK)


def ssm_chunked(z, zm, m, pre, pim, q, ar, ai, *, bsz, seq, col0):
    g, w, _ = m.shape
    ns2 = pre.shape[2]
    nsteps = ar.shape[1]
    nc = seq // SSM_CHUNK
    gb = SSM_OCTET
    pb = gb // 2
    cb = col0 // LANES
    return pl.pallas_call(
        functools.partial(_ssm_kernel, nc=nc, nsteps=nsteps),
        grid=(g // gb, bsz),
        in_specs=[
            pl.BlockSpec((seq, LANES), lambda j, b: (b, cb + j)),
            pl.BlockSpec((SSM_ROWS * SSM_CHUNK, LANES), lambda j, b: (0, cb + j)),
            pl.BlockSpec((gb, w, w), lambda j, b: (j, 0, 0)),
            pl.BlockSpec((pb, 2 * w, ns2), lambda j, b: (j, 0, 0)),
            pl.BlockSpec((pb, 2 * w, ns2), lambda j, b: (j, 0, 0)),
            pl.BlockSpec((pb, 2 * ns2, 2 * w), lambda j, b: (j, 0, 0)),
            pl.BlockSpec((pb, nsteps, ns2), lambda j, b: (j, 0, 0)),
            pl.BlockSpec((pb, nsteps, ns2), lambda j, b: (j, 0, 0)),
        ],
        out_specs=pl.BlockSpec((seq, LANES), lambda j, b: (b, j)),
        out_shape=jax.ShapeDtypeStruct((bsz * seq, g * SSM_GROUP_CH), F32),
        scratch_shapes=[
            pltpu.VMEM((gb, nc, w), BF16),
            pltpu.VMEM((gb, SSM_ROWS, w), BF16),
            pltpu.VMEM((gb, nc, w), F32),
        ],
        compiler_params=_cparams(("parallel", "arbitrary")),
        name="ssm_chunked",
    )(z, zm, m, pre, pim, q, ar, ai)


def _ssm_operators(a_re, a_im, log_dt, b_re, b_im, c_re, c_im, nc):
    t = SSM_CHUNK
    ar = a_re.astype(F32)
    ai = a_im.astype(F32)
    dt = jnp.exp(log_dt.astype(F32))[:, None]
    mag = jnp.exp(ar * dt)
    lam_re = mag * jnp.cos(ai * dt)
    lam_im = mag * jnp.sin(ai * dt)
    den = ar * ar + ai * ai
    nr = lam_re - 1.0
    coef_re = (nr * ar + lam_im * ai) / den
    coef_im = (lam_im * ar - nr * ai) / den
    br = b_re.astype(F32)
    bi = b_im.astype(F32)
    bb_re = coef_re[..., None] * br - coef_im[..., None] * bi
    bb_im = coef_re[..., None] * bi + coef_im[..., None] * br
    crt = c_re.astype(F32).transpose(0, 2, 1)
    cit = c_im.astype(F32).transpose(0, 2, 1)

    def lam_pow(k):
        k = k.astype(F32)
        m = jnp.exp(ar[..., None] * dt[..., None] * k)
        ang = ai[..., None] * dt[..., None] * k
        return m * jnp.cos(ang), m * jnp.sin(ang)

    g = ar.shape[0]
    hh = SSM_GROUP_CH
    pr, pi = lam_pow(jnp.arange(t + 1))

    def times_c(xr, xi):
        return (xr[..., None] * crt[:, :, None, :] - xi[..., None] * cit[:, :, None, :],
                xr[..., None] * cit[:, :, None, :] + xi[..., None] * crt[:, :, None, :])

    cl_re, cl_im = times_c(pr[:, :, :t], pi[:, :, :t])
    cl = jnp.concatenate([cl_re.reshape(g, -1, t * hh), cl_im.reshape(g, -1, t * hh)], axis=1)
    bt = jnp.concatenate([bb_re.transpose(0, 2, 1), -bb_im.transpose(0, 2, 1)], axis=-1)
    m_op = ssm_toeplitz(bt, cl)
    rr = pr[:, :, t - 1::-1][:, :, :t]
    ri = pi[:, :, t - 1::-1][:, :, :t]
    p_re = rr[..., None] * bb_re[:, :, None, :] - ri[..., None] * bb_im[:, :, None, :]
    p_im = rr[..., None] * bb_im[:, :, None, :] + ri[..., None] * bb_re[:, :, None, :]
    p_re = p_re.transpose(0, 2, 3, 1).reshape(g, t * hh, -1)
    p_im = p_im.transpose(0, 2, 3, 1).reshape(g, t * hh, -1)
    q_re, q_im = times_c(pr[:, :, 1:], pi[:, :, 1:])
    q_re = q_re.reshape(g, -1, t * hh)
    q_im = -q_im.reshape(g, -1, t * hh)
    nsteps = int(math.log2(nc))
    sr, si = lam_pow(t * (2 ** jnp.arange(nsteps)))

    def pair_rows(x):
        z = jnp.zeros_like(x[0::2])
        return jnp.concatenate([jnp.concatenate([x[0::2], z], axis=-1),
                                jnp.concatenate([z, x[1::2]], axis=-1)], axis=1)

    def pair_lanes(x):
        x = x.transpose(0, 2, 1)
        return jnp.concatenate([x[0::2], x[1::2]], axis=-1)

    pre_op = pair_rows(p_re).astype(BF16)
    pim_op = pair_rows(p_im).astype(BF16)
    q_op = jnp.concatenate([pair_rows(q_re), pair_rows(q_im)], axis=1).astype(BF16)
    return m_op, pre_op, pim_op, q_op, pair_lanes(sr), pair_lanes(si)


def _merge_kernel(p_ref, y_ref, us_ref, gp_ref, gs_ref, h_ref, dv_ref, wglu_ref, bglu_ref,
                  wbp_ref, wbs_ref, wout_ref, nf_ref, wr_ref, br_ref, h1_ref, lg_ref):
    s = _gelu_tanh(y_ref[...] + dv_ref[...] * us_ref[...])
    gl = jnp.dot(s.astype(BF16), wglu_ref[...], preferred_element_type=F32) + bglu_ref[...]
    s = s * _sigmoid(gl)
    y_ssm = jnp.dot(s.astype(BF16), wbs_ref[...], preferred_element_type=F32)
    y_pool = jnp.dot(p_ref[...], wbp_ref[...], preferred_element_type=F32)
    merged = gp_ref[...].astype(F32) * y_pool + gs_ref[...].astype(F32) * y_ssm
    h1 = h_ref[...] + jnp.dot(merged.astype(BF16), wout_ref[...], preferred_element_type=F32)
    h1_ref[...] = h1
    v = _rms(h1, nf_ref[...]).astype(BF16)
    lg_ref[...] = jnp.dot(v, wr_ref[...], preferred_element_type=F32) + br_ref[...]


def merge(pool_y, ssm_y, zps, gates, h0, dvec, w_glu, b_glu, w_bp, w_bs, w_out, nf, w_r, b_r,
          *, tm=256):
    n, d = h0.shape
    q = pool_y.shape[1]
    tm = min(tm, n)
    const = lambda shape: pl.BlockSpec(shape, lambda i: (0,) * len(shape),
                                       pipeline_mode=pl.Buffered(1))
    return pl.pallas_call(
        _merge_kernel,
        grid=(n // tm,),
        in_specs=[
            pl.BlockSpec((tm, q), lambda i: (i, 0)),
            pl.BlockSpec((tm, q), lambda i: (i, 0)),
            pl.BlockSpec((tm, q), lambda i: (i, 1)),
            pl.BlockSpec((tm, d), lambda i: (i, 0)),
            pl.BlockSpec((tm, d), lambda i: (i, 1)),
            pl.BlockSpec((tm, d), lambda i: (i, 0)),
            const((1, q)), const((q, q)), const((1, q)),
            const((q, d)), const((q, d)), const((d, d)),
            const((1, d)), const((d, ROUTER_LANES)), const((1, ROUTER_LANES)),
        ],
        out_specs=[
            pl.BlockSpec((tm, d), lambda i: (i, 0)),
            pl.BlockSpec((tm, ROUTER_LANES), lambda i: (i, 0)),
        ],
        out_shape=[
            jax.ShapeDtypeStruct((n, d), F32),
            jax.ShapeDtypeStruct((n, ROUTER_LANES), F32),
        ],
        compiler_params=_cparams(("parallel",)),
        name="merge",
    )(pool_y, ssm_y, zps, gates, gates, h0, dvec, w_glu, b_glu, w_bp, w_bs, w_out, nf, w_r, b_r)


def _route_kernel(lg_ref, gate_ref, eid_ref, rank_ref, cnt_ref, tri_ref, carry_ref):
    step = pl.program_id(0)
    tm = lg_ref.shape[0]

    @pl.when(step == 0)
    def _():
        rr = lax.broadcasted_iota(jnp.int32, (tm, tm), 0)
        cc = lax.broadcasted_iota(jnp.int32, (tm, tm), 1)
        tri_ref[...] = jnp.where(cc < rr, 1.0, 0.0).astype(BF16)
        carry_ref[...] = jnp.zeros_like(carry_ref)

    l = lg_ref[...]
    lane = lax.broadcasted_iota(jnp.int32, l.shape, 1).astype(F32)
    neg = -jnp.inf
    big = float(ROUTER_LANES)
    is_g = lane < MOE_GROUPS
    gl = jnp.where(is_g, l, neg)
    gmax = jnp.max(gl, axis=1, keepdims=True)
    gstar = jnp.min(jnp.where(gl == gmax, lane, big), axis=1, keepdims=True)
    pg = 1.0 / jnp.sum(jnp.where(is_g, jnp.exp(gl - gmax), 0.0), axis=1, keepdims=True)
    lo = MOE_GROUPS + EXPERTS_PER_GROUP * gstar
    is_e = (lane >= lo) & (lane < lo + EXPERTS_PER_GROUP)
    el = jnp.where(is_e, l, neg)
    m1 = jnp.max(el, axis=1, keepdims=True)
    i1 = jnp.min(jnp.where(is_e & (el == m1), lane, big), axis=1, keepdims=True)
    is_e2 = is_e & (lane != i1)
    el2 = jnp.where(is_e2, l, neg)
    m2 = jnp.max(el2, axis=1, keepdims=True)
    i2 = jnp.min(jnp.where(is_e2 & (el2 == m2), lane, big), axis=1, keepdims=True)
    t = jnp.exp(m2 - m1)
    w1 = pg / (1.0 + t)
    w2 = pg * t / (1.0 + t)
    gate_ref[...] = jnp.where(lane == 0.0, w1, jnp.where(lane == 1.0, w2, 0.0))
    eid = jnp.where(lane == 0.0, i1, jnp.where(lane == 1.0, i2, float(MOE_GROUPS)))
    eid_ref[...] = (eid - MOE_GROUPS).astype(jnp.int32)
    oh1 = lane == i1
    oh2 = lane == i2
    both = jnp.where(oh1 | oh2, 1.0, 0.0)
    before = carry_ref[...] + jnp.dot(tri_ref[...], both.astype(BF16),
                                      preferred_element_type=F32)
    r1 = jnp.sum(jnp.where(oh1, before, 0.0), axis=1, keepdims=True)
    r2 = jnp.sum(jnp.where(oh2, before, 0.0), axis=1, keepdims=True)
    rank_ref[...] = jnp.where(lane == 0.0, r1, jnp.where(lane == 1.0, r2, 0.0)).astype(jnp.int32)
    carry_ref[...] = carry_ref[...] + jnp.sum(both, axis=0, keepdims=True)
    cnt_ref[...] = carry_ref[...]


def route(logits, *, tm=1024):
    n, w = logits.shape
    tm = min(tm, n)
    tok = pl.BlockSpec((tm, w), lambda i: (i, 0))
    return pl.pallas_call(
        _route_kernel,
        grid=(n // tm,),
        in_specs=[tok],
        out_specs=[tok, tok, tok, pl.BlockSpec((1, w), lambda i: (0, 0))],
        out_shape=[jax.ShapeDtypeStruct((n, w), F32), jax.ShapeDtypeStruct((n, w), jnp.int32),
                   jax.ShapeDtypeStruct((n, w), jnp.int32), jax.ShapeDtypeStruct((1, w), F32)],
        scratch_shapes=[pltpu.VMEM((tm, tm), BF16), pltpu.VMEM((1, w), F32)],
        compiler_params=_cparams(("arbitrary",)),
        name="route",
    )(logits)


def _expert_kernel(be_ref, first_ref, par_ref, nexte_ref, qb_ref, nu_ref, ord_ref, h_hbm, nf_ref,
                   wg_hbm, wu_hbm, wd_hbm, o_ref, xbuf, vbuf, sg, su, sd, wgb, wub, wdb, gsem,
                   wsem, *, rows):
    i = pl.program_id(0)
    nused = nu_ref[0]
    slot = lax.rem(i, 2)
    qmax = ord_ref.shape[0] - rows

    def first_pos(blk):
        return jnp.clip(qb_ref[blk], 0, qmax)

    def gather_row(q0, r, sl):
        tok = ord_ref[q0 + r] >> 1
        return pltpu.make_async_copy(h_hbm.at[pl.ds(tok, 1), :], xbuf.at[sl, pl.ds(r, 1), :],
                                     gsem.at[sl])

    def wait_gather(sl):
        pltpu.make_async_copy(h_hbm.at[pl.ds(0, rows), :], xbuf.at[sl], gsem.at[sl]).wait()

    @pl.when(i == 0)
    def _():
        q0 = first_pos(0)

        def body(r, _):
            gather_row(q0, r, 0).start()
            return 0
        lax.fori_loop(0, rows, body, 0)

    def weight_copies(e, p):
        return (pltpu.make_async_copy(wg_hbm.at[e], sg.at[p], wsem.at[p]),
                pltpu.make_async_copy(wu_hbm.at[e], su.at[p], wsem.at[p]),
                pltpu.make_async_copy(wd_hbm.at[e], sd.at[p], wsem.at[p]))

    @pl.when(i == 0)
    def _():
        for cp in weight_copies(be_ref[0], 0):
            cp.start()

    @pl.when(i < nused)
    def _():
        @pl.when(first_ref[i] == 1)
        def _():
            p = par_ref[i]
            for cp in weight_copies(be_ref[i], p):
                cp.wait()
            wgb[...] = sg[p].astype(BF16)
            wub[...] = su[p].astype(BF16)
            wdb[...] = sd[p].astype(BF16)

            @pl.when(nexte_ref[i] >= 0)
            def _():
                for cp in weight_copies(nexte_ref[i], 1 - p):
                    cp.start(priority=1)

        wait_gather(slot)
        vbuf[...] = _rms(xbuf[slot], nf_ref[...]).astype(BF16)

        q_next = first_pos(jnp.minimum(i + 1, pl.num_programs(0) - 1))
        for r in range(rows):
            gather_row(q_next, r, 1 - slot).start()

        v = vbuf[...]
        hg = jnp.dot(v, wgb[...], preferred_element_type=F32)
        hu = jnp.dot(v, wub[...], preferred_element_type=F32)
        a = (hg * _sigmoid(hg) * hu).astype(BF16)
        o_ref[...] = jnp.dot(a, wdb[...], preferred_element_type=F32)

        @pl.when(i == nused - 1)
        def _():
            wait_gather(1 - slot)

    @pl.when(i >= nused)
    def _():
        o_ref[...] = jnp.zeros_like(o_ref)


def experts(plan, h1, nf, w_gate, w_up, w_down, *, rows=EXPERT_ROWS):
    n, d = h1.shape
    ne, _, ff = w_gate.shape
    nblk = plan[0].shape[0]
    hbm = pl.BlockSpec(memory_space=pl.ANY)
    grid_spec = pltpu.PrefetchScalarGridSpec(
        num_scalar_prefetch=len(plan),
        grid=(nblk,),
        in_specs=[hbm, pl.BlockSpec((1, d), lambda i, *_: (0, 0)), hbm, hbm, hbm],
        out_specs=pl.BlockSpec((rows, d), lambda i, *_: (i, 0)),
        scratch_shapes=[
            pltpu.VMEM((2, rows, d), F32),
            pltpu.VMEM((rows, d), BF16),
            pltpu.VMEM((2, d, ff), F32),
            pltpu.VMEM((2, d, ff), F32),
            pltpu.VMEM((2, ff, d), F32),
            pltpu.VMEM((d, ff), BF16),
            pltpu.VMEM((d, ff), BF16),
            pltpu.VMEM((ff, d), BF16),
            pltpu.SemaphoreType.DMA((2,)),
            pltpu.SemaphoreType.DMA((2,)),
        ],
    )
    return pl.pallas_call(
        functools.partial(_expert_kernel, rows=rows),
        grid_spec=grid_spec,
        out_shape=jax.ShapeDtypeStruct((nblk * rows, d), F32),
        compiler_params=_cparams(("arbitrary",)),
        name="experts",
    )(*plan, h1, nf, w_gate, w_up, w_down)


def _dispatch_plan(eid, rank, counts, rows):
    n_experts = counts.shape[0]
    m = eid.size
    e_flat = eid.reshape(m)
    order = jnp.pad(jnp.argsort(e_flat).astype(jnp.int32), (0, rows))
    nblk_e = (counts + rows - 1) // rows
    bend = jnp.cumsum(nblk_e)
    bstart = bend - nblk_e
    start = jnp.cumsum(counts) - counts
    nblk = (m + n_experts * (rows - 1) + rows - 1) // rows
    bi = jnp.arange(nblk, dtype=jnp.int32)
    block_e = jnp.minimum(jnp.sum((bi[:, None] >= bend[None, :]).astype(jnp.int32), axis=1),
                          n_experts - 1)
    qbase = (start[block_e] + (bi - bstart[block_e]) * rows).astype(jnp.int32)
    nused = bend[-1:].astype(jnp.int32)
    assert nblk <= 256
    first_blk = jnp.dot(jax.nn.one_hot(e_flat, n_experts, dtype=BF16),
                        bstart.astype(F32).astype(BF16)[:, None],
                        preferred_element_type=F32)[:, 0]
    dest = (first_blk.astype(jnp.int32) * rows + rank.reshape(m)).astype(jnp.int32)
    has = counts > 0
    first = ((bi == bstart[block_e]) & (bi < nused[0])).astype(jnp.int32)
    parity = ((jnp.cumsum(has.astype(jnp.int32)) - 1)[block_e] & 1).astype(jnp.int32)
    ids = jnp.arange(n_experts, dtype=jnp.int32)
    at_or_after = lax.cummin(jnp.where(has, ids, n_experts), reverse=True)
    after = jnp.concatenate([at_or_after[1:], jnp.full((1,), n_experts, jnp.int32)])
    next_expert = jnp.where(after < n_experts, after, -1)[block_e].astype(jnp.int32)
    return (block_e, first, parity, next_expert, qbase, nused, order), dest


def _combine_kernel(dest_ref, h_ref, y_hbm, gate_ref, g_ref, o_ref, ybuf, sem, *, tm):
    i = pl.program_id(0)
    slot = lax.rem(i, 2)

    def row_copy(tile, t, k, sl):
        row = dest_ref[(tile * tm + t) * 2 + k]
        return pltpu.make_async_copy(y_hbm.at[pl.ds(row, 1), :], ybuf.at[sl, k, pl.ds(t, 1), :],
                                     sem.at[sl])

    def wait_rows(sl):
        for k in range(2):
            pltpu.make_async_copy(y_hbm.at[pl.ds(0, tm), :], ybuf.at[sl, k], sem.at[sl]).wait()

    @pl.when(i == 0)
    def _():
        def body(t, _):
            row_copy(0, t, 0, 0).start()
            row_copy(0, t, 1, 0).start()
            return 0
        lax.fori_loop(0, tm, body, 0)

    wait_rows(slot)
    nxt = jnp.minimum(i + 1, pl.num_programs(0) - 1)
    for t in range(tm):
        for k in range(2):
            row_copy(nxt, t, k, 1 - slot).start()
    gate = gate_ref[...]
    h = h_ref[...] + gate[:, 0:1] * ybuf[slot, 0] + gate[:, 1:2] * ybuf[slot, 1]
    o_ref[...] = _rms(h, g_ref[...])

    @pl.when(i == pl.num_programs(0) - 1)
    def _():
        wait_rows(1 - slot)


def combine(dest, h1, y, gate, g, *, tm=256):
    n, d = h1.shape
    tm = min(tm, n)
    grid_spec = pltpu.PrefetchScalarGridSpec(
        num_scalar_prefetch=1,
        grid=(n // tm,),
        in_specs=[
            pl.BlockSpec((tm, d), lambda i, ds: (i, 0)),
            pl.BlockSpec(memory_space=pl.ANY),
            pl.BlockSpec((tm, gate.shape[1]), lambda i, ds: (i, 0)),
            pl.BlockSpec((1, d), lambda i, ds: (0, 0)),
        ],
        out_specs=pl.BlockSpec((tm, d), lambda i, ds: (i, 0)),
        scratch_shapes=[pltpu.VMEM((2, 2, tm, d), F32), pltpu.SemaphoreType.DMA((2,))],
    )
    return pl.pallas_call(
        functools.partial(_combine_kernel, tm=tm),
        grid_spec=grid_spec,
        out_shape=jax.ShapeDtypeStruct((n, d), F32),
        compiler_params=_cparams(("arbitrary",)),
        name="combine",
    )(dest, h1, y, gate, g)


def kernel(x, meta, norm_mix, w_in, pool_w, pool_scale, ssm_a_re, ssm_a_im, ssm_log_dt, ssm_b_re,
           ssm_b_im, ssm_c_re, ssm_c_im, ssm_d, w_glu, b_glu, w_branch_pool, w_branch_ssm, w_out,
           norm_ffn, w_router_group, b_router_group, w_router_expert, b_router_expert, w_gate,
           w_up, w_down, norm_final):
    assert w_in.shape[0] == 1, "single layer"
    bsz, seq, d = x.shape
    n = bsz * seq
    q = pool_scale.shape[1]
    assert meta.shape[0] == N_META == SSM_CHUNK and N_META >= max(POOL_WINDOWS)
    nc = seq // SSM_CHUNK
    assert nc & (nc - 1) == 0 and nc % SSM_ROWS == 0

    xf = x.reshape(n, d)
    g_mix = norm_mix[0][None, :]
    w_in_b = w_in[0].astype(BF16)

    zps = rms_matmul(xf, g_mix, w_in_b, col0=0, cols=2 * q, act=False, out_dtype=F32)
    zps_meta = rms_matmul(meta.astype(F32), g_mix, w_in_b, col0=0, cols=2 * q, act=False,
                          out_dtype=F32)
    zps_meta = jnp.pad(zps_meta, ((0, SSM_ROWS * SSM_CHUNK - N_META), (0, 0)))
    gates = rms_matmul(xf, g_mix, w_in_b, col0=2 * q, cols=2 * d, act=True, out_dtype=BF16)

    pool_y = pool_mixer(zps.reshape(bsz, seq, 2 * q), zps_meta, pool_w[0].astype(BF16),
                        pool_scale[0][None, :]).reshape(n, q)

    ssm_ops = _ssm_operators(ssm_a_re[0], ssm_a_im[0], ssm_log_dt[0], ssm_b_re[0], ssm_b_im[0],
                             ssm_c_re[0], ssm_c_im[0], nc)
    ssm_y = ssm_chunked(zps, zps_meta, *ssm_ops, bsz=bsz, seq=seq, col0=q)

    n_groups = w_router_group.shape[2]
    n_exp = w_router_expert.shape[2]
    w_r = jnp.concatenate([w_router_group[0], w_router_expert[0]], axis=1)
    w_r = jnp.pad(w_r, ((0, 0), (0, ROUTER_LANES - n_groups - n_exp))).astype(BF16)
    b_r = jnp.concatenate([b_router_group[0], b_router_expert[0]])
    b_r = jnp.pad(b_r, (0, ROUTER_LANES - n_groups - n_exp))[None, :].astype(F32)
    h1, logits = merge(pool_y, ssm_y, zps, gates, xf, ssm_d[0][None, :], w_glu[0].astype(BF16),
                       b_glu[0][None, :], w_branch_pool[0].astype(BF16),
                       w_branch_ssm[0].astype(BF16), w_out[0].astype(BF16),
                       norm_ffn[0][None, :], w_r, b_r)

    gate, eid, rank, cnt = route(logits)
    counts = cnt[0, n_groups:n_groups + n_exp].astype(jnp.int32)
    plan, dest = _dispatch_plan(eid[:, :2], rank[:, :2], counts, EXPERT_ROWS)
    y = experts(plan, h1, norm_ffn[0][None, :], w_gate[0], w_up[0], w_down[0])
    out = combine(dest, h1, y, gate, norm_final[None, :])
    return out.reshape(bsz, seq, d)
```

```python
import functools
import math

import jax
import jax.numpy as jnp
from jax import lax
from jax.experimental import pallas as pl
from jax.experimental.pallas import tpu as pltpu

F32 = jnp.float32
BF16 = jnp.bfloat16

RMS_EPS = 1e-6
POOL_WINDOWS = (2, 4, 8, 16)
N_META = 16
LANES = 128
SSM_CHUNK = 16
SSM_GROUP_CH = 16
SSM_STATE = 64
SSM_OCTET = LANES // SSM_GROUP_CH
SSM_ROWS = 16
SSM_ROWS_BACK = 16
SSM_UNROLL_BACK = 2
SSM_UNROLL = 4
MOE_GROUPS = 8
EXPERTS_PER_GROUP = 8
ROUTER_LANES = LANES
EXPERT_ROWS = 256
EXPERT_GATHER_BUFS = 3
COMBINE_GATHER_BUFS = 3
VMEM_LIMIT = 56 * 1024 * 1024


def _cparams(sem, vmem=VMEM_LIMIT):
    return pltpu.CompilerParams(dimension_semantics=sem, vmem_limit_bytes=vmem)


def _sigmoid(x):
    return 1.0 / (1.0 + jnp.exp(-x))


def _gelu_tanh(x):
    c = math.sqrt(2.0 / math.pi)
    return 0.5 * x * (1.0 + jnp.tanh(c * (x + 0.044715 * (x * x * x))))


def _rms(x, g):
    ms = jnp.mean(x * x, axis=-1, keepdims=True)
    return x * lax.rsqrt(ms + RMS_EPS) * g


def _rms_matmul_kernel(x_ref, g_ref, w_ref, o_ref, u_ref, *, act):
    @pl.when(pl.program_id(1) == 0)
    def _():
        u_ref[...] = _rms(x_ref[...], g_ref[...]).astype(BF16)

    acc = jnp.dot(u_ref[...], w_ref[...], preferred_element_type=F32)
    if act:
        acc = _sigmoid(acc)
    o_ref[...] = acc.astype(o_ref.dtype)


def rms_matmul(x, g, w, *, col0, cols, act, out_dtype, tm=1024, tn=1024):
    n, d = x.shape
    tm = min(tm, n)
    cb = col0 // tn
    return pl.pallas_call(
        functools.partial(_rms_matmul_kernel, act=act),
        grid=(n // tm, cols // tn),
        in_specs=[
            pl.BlockSpec((tm, d), lambda i, j: (i, 0)),
            pl.BlockSpec((1, d), lambda i, j: (0, 0)),
            pl.BlockSpec((d, tn), lambda i, j: (0, j + cb)),
        ],
        out_specs=pl.BlockSpec((tm, tn), lambda i, j: (i, j)),
        out_shape=jax.ShapeDtypeStruct((n, cols), out_dtype),
        scratch_shapes=[pltpu.VMEM((tm, d), BF16)],
        compiler_params=_cparams(("parallel", "arbitrary")),
        name="rms_matmul",
    )(x, g, w)


def _pool_kernel(z_ref, zm_ref, w_ref, sc_ref, o_ref, ext_ref, *, tl, gw):
    h = N_META

    @pl.when(pl.program_id(1) == 0)
    def _():
        ext_ref[0:h, :] = zm_ref[...]

    ext_ref[h:h + tl, :] = z_ref[...]
    for g, w in enumerate(POOL_WINDOWS):
        cols = slice(g * gw, (g + 1) * gw)
        u = ext_ref[h:h + tl, cols]
        s = u
        for k in range(1, w):
            s = s + ext_ref[h - k:h - k + tl, cols]
        d = s * (1.0 / w) - u
        y = jnp.dot(d.astype(BF16), w_ref[g], preferred_element_type=F32)
        o_ref[:, cols] = (y * sc_ref[:, cols]).astype(o_ref.dtype)
    ext_ref[0:h, :] = ext_ref[tl:tl + h, :]


def pool_mixer(z, zmeta, pool_w, pool_scale, *, tl=512):
    b, l, _ = z.shape
    ng, gw, _ = pool_w.shape
    c = ng * gw
    tl = min(tl, l)
    return pl.pallas_call(
        functools.partial(_pool_kernel, tl=tl, gw=gw),
        grid=(b, l // tl),
        in_specs=[
            pl.BlockSpec((None, tl, c), lambda i, t: (i, t, 0)),
            pl.BlockSpec((N_META, c), lambda i, t: (0, 0)),
            pl.BlockSpec((ng, gw, gw), lambda i, t: (0, 0, 0)),
            pl.BlockSpec((1, c), lambda i, t: (0, 0)),
        ],
        out_specs=pl.BlockSpec((None, tl, c), lambda i, t: (i, t, 0)),
        out_shape=jax.ShapeDtypeStruct((b, l, c), BF16),
        scratch_shapes=[pltpu.VMEM((tl + N_META, c), F32)],
        compiler_params=_cparams(("arbitrary", "arbitrary")),
        name="pool_mixer",
    )(z, zmeta, pool_w, pool_scale)


def _ssm_toeplitz_kernel(bt_ref, cl_ref, m_ref):
    w = m_ref.shape[-1]
    lane = lax.broadcasted_iota(jnp.int32, (SSM_GROUP_CH, w), 1)
    for g in range(bt_ref.shape[0]):
        kt = jnp.dot(bt_ref[g], cl_ref[g], preferred_element_type=F32,
                     precision=lax.Precision.HIGHEST)
        for s in range(SSM_CHUNK):
            off = s * SSM_GROUP_CH
            blk = kt if s == 0 else jnp.where(lane >= off, pltpu.roll(kt, off, axis=1), 0.0)
            m_ref[g, off:off + SSM_GROUP_CH, :] = blk.astype(m_ref.dtype)


def ssm_toeplitz(bt, cl):
    g, hh, k2 = bt.shape
    w = cl.shape[2]
    gb = SSM_OCTET
    return pl.pallas_call(
        _ssm_toeplitz_kernel,
        grid=(g // gb,),
        in_specs=[pl.BlockSpec((gb, hh, k2), lambda i: (i, 0, 0)),
                  pl.BlockSpec((gb, k2, w), lambda i: (i, 0, 0))],
        out_specs=pl.BlockSpec((gb, w, w), lambda i: (i, 0, 0)),
        out_shape=jax.ShapeDtypeStruct((g, w, w), BF16),
        compiler_params=_cparams(("parallel",)),
        name="ssm_toeplitz",
    )(bt, cl)


def _block_transpose8(a, lane):
    for d in (4, 2, 1):
        w = d * SSM_GROUP_CH
        keep = (lane & w) == 0
        b = list(a)
        for k in range(SSM_OCTET):
            if k & d == 0:
                x, y = a[k], a[k + d]
                b[k] = jnp.where(keep, x, pltpu.roll(y, w, axis=1))
                b[k + d] = jnp.where(keep, pltpu.roll(x, LANES - w, axis=1), y)
        a = b
    return a


def _ssm_kernel(z_ref, zm_ref, m_ref, pre_ref, pim_ref, q_ref, ar_ref, ai_ref, o_ref, uscr, umscr,
                yscr, *, nc, nsteps):
    rt = SSM_ROWS
    t = SSM_CHUNK
    lane = lax.broadcasted_iota(jnp.int32, (rt, LANES), 1)

    def regroup(src_ref, dst_ref, it):
        base = pl.multiple_of(it * (rt * t), rt * t)
        row0 = pl.multiple_of(it * rt, rt)
        for half in range(2):
            a = [src_ref[pl.ds(base + half * SSM_OCTET + k, rt, stride=t), :]
                 for k in range(SSM_OCTET)]
            b = _block_transpose8(a, lane)
            for j in range(SSM_OCTET):
                dst_ref[j, pl.ds(row0, rt), half * LANES:(half + 1) * LANES] = b[j].astype(BF16)

    regroup(zm_ref, umscr, 0)

    def fwd(it, _):
        regroup(z_ref, uscr, it)
        return 0
    lax.fori_loop(0, nc // rt, fwd, 0, unroll=SSM_UNROLL)

    c = lax.broadcasted_iota(jnp.int32, (nc, 2 * SSM_STATE), 0)

    def per_pair(pi, _):
        j0 = 2 * pi
        j1 = j0 + 1
        u0 = uscr[j0]
        u1 = uscr[j1]
        up = jnp.concatenate([u0, u1], axis=1)
        ump = jnp.concatenate([umscr[j0], umscr[j1]], axis=1)
        pre = pre_ref[pi]
        pim = pim_ref[pi]
        s_re = jnp.dot(up, pre, preferred_element_type=F32)
        s_im = jnp.dot(up, pim, preferred_element_type=F32)
        x0_re = jnp.dot(ump, pre, preferred_element_type=F32)[0:1]
        x0_im = jnp.dot(ump, pim, preferred_element_type=F32)[0:1]
        xr = jnp.where(c == 0, x0_re, pltpu.roll(s_re, 1, axis=0))
        xi = jnp.where(c == 0, x0_im, pltpu.roll(s_im, 1, axis=0))
        d = 1
        for k in range(nsteps):
            zr = jnp.where(c >= d, pltpu.roll(xr, d, axis=0), 0.0)
            zi = jnp.where(c >= d, pltpu.roll(xi, d, axis=0), 0.0)
            a = ar_ref[pi, k:k + 1, :]
            b = ai_ref[pi, k:k + 1, :]
            xr, xi = xr + a * zr - b * zi, xi + a * zi + b * zr
            d *= 2
        xc = jnp.concatenate([xr, xi], axis=1).astype(BF16)
        y_inter = jnp.dot(xc, q_ref[pi], preferred_element_type=F32)
        w = u0.shape[1]
        y0 = jnp.dot(u0, m_ref[j0], preferred_element_type=F32) + y_inter[:, :w]
        y1 = jnp.dot(u1, m_ref[j1], preferred_element_type=F32) + y_inter[:, w:]
        yscr[j0] = y0
        yscr[j1] = y1
        return 0
    lax.fori_loop(0, SSM_OCTET // 2, per_pair, 0)

    rb = SSM_ROWS_BACK
    lane_b = lax.broadcasted_iota(jnp.int32, (rb, LANES), 1)

    def bwd(it, _):
        base = pl.multiple_of(it * (rb * t), rb * t)
        row0 = pl.multiple_of(it * rb, rb)
        for half in range(2):
            b = [yscr[j, pl.ds(row0, rb), half * LANES:(half + 1) * LANES]
                 for j in range(SSM_OCTET)]
            a = _block_transpose8(b, lane_b)
            for k in range(SSM_OCTET):
                o_ref[pl.ds(base + half * SSM_OCTET + k, rb, stride=t), :] = a[k]
        return 0
    lax.fori_loop(0, nc // rb, bwd, 0, unroll=SSM_UNROLL_BACK)


def ssm_chunked(z, zm, m, pre, pim, q, ar, ai, *, bsz, seq, col0):
    g, w, _ = m.shape
    ns2 = pre.shape[2]
    nsteps = ar.shape[1]
    nc = seq // SSM_CHUNK
    gb = SSM_OCTET
    pb = gb // 2
    cb = col0 // LANES
    return pl.pallas_call(
        functools.partial(_ssm_kernel, nc=nc, nsteps=nsteps),
        grid=(g // gb, bsz),
        in_specs=[
            pl.BlockSpec((seq, LANES), lambda j, b: (b, cb + j)),
            pl.BlockSpec((SSM_ROWS * SSM_CHUNK, LANES), lambda j, b: (0, cb + j)),
            pl.BlockSpec((gb, w, w), lambda j, b: (j, 0, 0)),
            pl.BlockSpec((pb, 2 * w, ns2), lambda j, b: (j, 0, 0)),
            pl.BlockSpec((pb, 2 * w, ns2), lambda j, b: (j, 0, 0)),
            pl.BlockSpec((pb, 2 * ns2, 2 * w), lambda j, b: (j, 0, 0)),
            pl.BlockSpec((pb, nsteps, ns2), lambda j, b: (j, 0, 0)),
            pl.BlockSpec((pb, nsteps, ns2), lambda j, b: (j, 0, 0)),
        ],
        out_specs=pl.BlockSpec((seq, LANES), lambda j, b: (b, j)),
        out_shape=jax.ShapeDtypeStruct((bsz * seq, g * SSM_GROUP_CH), F32),
        scratch_shapes=[
            pltpu.VMEM((gb, nc, w), BF16),
            pltpu.VMEM((gb, SSM_ROWS, w), BF16),
            pltpu.VMEM((gb, nc, w), F32),
        ],
        compiler_params=_cparams(("parallel", "arbitrary")),
        name="ssm_chunked",
    )(z, zm, m, pre, pim, q, ar, ai)


def _ssm_operators(a_re, a_im, log_dt, b_re, b_im, c_re, c_im, nc):
    t = SSM_CHUNK
    ar = a_re.astype(F32)
    ai = a_im.astype(F32)
    dt = jnp.exp(log_dt.astype(F32))[:, None]
    mag = jnp.exp(ar * dt)
    lam_re = mag * jnp.cos(ai * dt)
    lam_im = mag * jnp.sin(ai * dt)
    den = ar * ar + ai * ai
    nr = lam_re - 1.0
    coef_re = (nr * ar + lam_im * ai) / den
    coef_im = (lam_im * ar - nr * ai) / den
    br = b_re.astype(F32)
    bi = b_im.astype(F32)
    bb_re = coef_re[..., None] * br - coef_im[..., None] * bi
    bb_im = coef_re[..., None] * bi + coef_im[..., None] * br
    crt = c_re.astype(F32).transpose(0, 2, 1)
    cit = c_im.astype(F32).transpose(0, 2, 1)

    def lam_pow(k):
        k = k.astype(F32)
        m = jnp.exp(ar[..., None] * dt[..., None] * k)
        ang = ai[..., None] * dt[..., None] * k
        return m * jnp.cos(ang), m * jnp.sin(ang)

    g = ar.shape[0]
    hh = SSM_GROUP_CH
    pr, pi = lam_pow(jnp.arange(t + 1))

    def times_c(xr, xi):
        return (xr[..., None] * crt[:, :, None, :] - xi[..., None] * cit[:, :, None, :],
                xr[..., None] * cit[:, :, None, :] + xi[..., None] * crt[:, :, None, :])

    cl_re, cl_im = times_c(pr[:, :, :t], pi[:, :, :t])
    cl = jnp.concatenate([cl_re.reshape(g, -1, t * hh), cl_im.reshape(g, -1, t * hh)], axis=1)
    bt = jnp.concatenate([bb_re.transpose(0, 2, 1), -bb_im.transpose(0, 2, 1)], axis=-1)
    m_op = ssm_toeplitz(bt, cl)
    rr = pr[:, :, t - 1::-1][:, :, :t]
    ri = pi[:, :, t - 1::-1][:, :, :t]
    p_re = rr[..., None] * bb_re[:, :, None, :] - ri[..., None] * bb_im[:, :, None, :]
    p_im = rr[..., None] * bb_im[:, :, None, :] + ri[..., None] * bb_re[:, :, None, :]
    p_re = p_re.transpose(0, 2, 3, 1).reshape(g, t * hh, -1)
    p_im = p_im.transpose(0, 2, 3, 1).reshape(g, t * hh, -1)
    q_re, q_im = times_c(pr[:, :, 1:], pi[:, :, 1:])
    q_re = q_re.reshape(g, -1, t * hh)
    q_im = -q_im.reshape(g, -1, t * hh)
    nsteps = int(math.log2(nc))
    sr, si = lam_pow(t * (2 ** jnp.arange(nsteps)))

    def pair_rows(x):
        z = jnp.zeros_like(x[0::2])
        return jnp.concatenate([jnp.concatenate([x[0::2], z], axis=-1),
                                jnp.concatenate([z, x[1::2]], axis=-1)], axis=1)

    def pair_lanes(x):
        x = x.transpose(0, 2, 1)
        return jnp.concatenate([x[0::2], x[1::2]], axis=-1)

    pre_op = pair_rows(p_re).astype(BF16)
    pim_op = pair_rows(p_im).astype(BF16)
    q_op = jnp.concatenate([pair_rows(q_re), pair_rows(q_im)], axis=1).astype(BF16)
    return m_op, pre_op, pim_op, q_op, pair_lanes(sr), pair_lanes(si)


def _merge_kernel(p_ref, y_ref, us_ref, gp_ref, gs_ref, h_ref, dv_ref, wglu_ref, bglu_ref,
                  wbp_ref, wbs_ref, wout_ref, nf_ref, wr_ref, br_ref, h1_ref, lg_ref):
    s = _gelu_tanh(y_ref[...] + dv_ref[...] * us_ref[...])
    gl = jnp.dot(s.astype(BF16), wglu_ref[...], preferred_element_type=F32) + bglu_ref[...]
    s = s * _sigmoid(gl)
    y_ssm = jnp.dot(s.astype(BF16), wbs_ref[...], preferred_element_type=F32)
    y_pool = jnp.dot(p_ref[...], wbp_ref[...], preferred_element_type=F32)
    merged = gp_ref[...].astype(F32) * y_pool + gs_ref[...].astype(F32) * y_ssm
    h1 = h_ref[...] + jnp.dot(merged.astype(BF16), wout_ref[...], preferred_element_type=F32)
    h1_ref[...] = h1
    v = _rms(h1, nf_ref[...]).astype(BF16)
    lg_ref[...] = jnp.dot(v, wr_ref[...], preferred_element_type=F32) + br_ref[...]


def merge(pool_y, ssm_y, zps, gates, h0, dvec, w_glu, b_glu, w_bp, w_bs, w_out, nf, w_r, b_r,
          *, tm=256):
    n, d = h0.shape
    q = pool_y.shape[1]
    tm = min(tm, n)
    const = lambda shape: pl.BlockSpec(shape, lambda i: (0,) * len(shape),
                                       pipeline_mode=pl.Buffered(1))
    return pl.pallas_call(
        _merge_kernel,
        grid=(n // tm,),
        in_specs=[
            pl.BlockSpec((tm, q), lambda i: (i, 0)),
            pl.BlockSpec((tm, q), lambda i: (i, 0)),
            pl.BlockSpec((tm, q), lambda i: (i, 1)),
            pl.BlockSpec((tm, d), lambda i: (i, 0)),
            pl.BlockSpec((tm, d), lambda i: (i, 1)),
            pl.BlockSpec((tm, d), lambda i: (i, 0)),
            const((1, q)), const((q, q)), const((1, q)),
            const((q, d)), const((q, d)), const((d, d)),
            const((1, d)), const((d, ROUTER_LANES)), const((1, ROUTER_LANES)),
        ],
        out_specs=[
            pl.BlockSpec((tm, d), lambda i: (i, 0)),
            pl.BlockSpec((tm, ROUTER_LANES), lambda i: (i, 0)),
        ],
        out_shape=[
            jax.ShapeDtypeStruct((n, d), F32),
            jax.ShapeDtypeStruct((n, ROUTER_LANES), F32),
        ],
        compiler_params=_cparams(("parallel",)),
        name="merge",
    )(pool_y, ssm_y, zps, gates, gates, h0, dvec, w_glu, b_glu, w_bp, w_bs, w_out, nf, w_r, b_r)


def _route_kernel(lg_ref, gate_ref, eid_ref, rank_ref, cnt_ref, tri_ref, carry_ref):
    step = pl.program_id(0)
    tm = lg_ref.shape[0]

    @pl.when(step == 0)
    def _():
        rr = lax.broadcasted_iota(jnp.int32, (tm, tm), 0)
        cc = lax.broadcasted_iota(jnp.int32, (tm, tm), 1)
        tri_ref[...] = jnp.where(cc < rr, 1.0, 0.0).astype(BF16)
        carry_ref[...] = jnp.zeros_like(carry_ref)

    l = lg_ref[...]
    lane = lax.broadcasted_iota(jnp.int32, l.shape, 1).astype(F32)
    neg = -jnp.inf
    big = float(ROUTER_LANES)
    is_g = lane < MOE_GROUPS
    gl = jnp.where(is_g, l, neg)
    gmax = jnp.max(gl, axis=1, keepdims=True)
    gstar = jnp.min(jnp.where(gl == gmax, lane, big), axis=1, keepdims=True)
    pg = 1.0 / jnp.sum(jnp.where(is_g, jnp.exp(gl - gmax), 0.0), axis=1, keepdims=True)
    lo = MOE_GROUPS + EXPERTS_PER_GROUP * gstar
    is_e = (lane >= lo) & (lane < lo + EXPERTS_PER_GROUP)
    el = jnp.where(is_e, l, neg)
    m1 = jnp.max(el, axis=1, keepdims=True)
    i1 = jnp.min(jnp.where(is_e & (el == m1), lane, big), axis=1, keepdims=True)
    is_e2 = is_e & (lane != i1)
    el2 = jnp.where(is_e2, l, neg)
    m2 = jnp.max(el2, axis=1, keepdims=True)
    i2 = jnp.min(jnp.where(is_e2 & (el2 == m2), lane, big), axis=1, keepdims=True)
    t = jnp.exp(m2 - m1)
    w1 = pg / (1.0 + t)
    w2 = pg * t / (1.0 + t)
    gate_ref[...] = jnp.where(lane == 0.0, w1, jnp.where(lane == 1.0, w2, 0.0))
    eid = jnp.where(lane == 0.0, i1, jnp.where(lane == 1.0, i2, float(MOE_GROUPS)))
    eid_ref[...] = (eid - MOE_GROUPS).astype(jnp.int32)
    oh1 = lane == i1
    oh2 = lane == i2
    both = jnp.where(oh1 | oh2, 1.0, 0.0)
    before = carry_ref[...] + jnp.dot(tri_ref[...], both.astype(BF16),
                                      preferred_element_type=F32)
    r1 = jnp.sum(jnp.where(oh1, before, 0.0), axis=1, keepdims=True)
    r2 = jnp.sum(jnp.where(oh2, before, 0.0), axis=1, keepdims=True)
    rank_ref[...] = jnp.where(lane == 0.0, r1, jnp.where(lane == 1.0, r2, 0.0)).astype(jnp.int32)
    carry_ref[...] = carry_ref[...] + jnp.sum(both, axis=0, keepdims=True)
    cnt_ref[...] = carry_ref[...]


def route(logits, *, tm=1024):
    n, w = logits.shape
    tm = min(tm, n)
    tok = pl.BlockSpec((tm, w), lambda i: (i, 0))
    return pl.pallas_call(
        _route_kernel,
        grid=(n // tm,),
        in_specs=[tok],
        out_specs=[tok, tok, tok, pl.BlockSpec((1, w), lambda i: (0, 0))],
        out_shape=[jax.ShapeDtypeStruct((n, w), F32), jax.ShapeDtypeStruct((n, w), jnp.int32),
                   jax.ShapeDtypeStruct((n, w), jnp.int32), jax.ShapeDtypeStruct((1, w), F32)],
        scratch_shapes=[pltpu.VMEM((tm, tm), BF16), pltpu.VMEM((1, w), F32)],
        compiler_params=_cparams(("arbitrary",)),
        name="route",
    )(logits)


def _expert_kernel(be_ref, first_ref, par_ref, nexte_ref, qb_ref, nu_ref, ord_ref, h_hbm, nf_ref,
                   wg_hbm, wu_hbm, wd_hbm, o_ref, xbuf, vbuf, sg, su, sd, wgb, wub, wdb, gsem,
                   wsem, *, rows):
    i = pl.program_id(0)
    nused = nu_ref[0]
    nbuf = xbuf.shape[0]
    last_blk = pl.num_programs(0) - 1
    slot = lax.rem(i, nbuf)
    qmax = ord_ref.shape[0] - rows

    def first_pos(blk):
        return jnp.clip(qb_ref[blk], 0, qmax)

    def gather_row(q0, r, sl):
        tok = ord_ref[q0 + r] >> 1
        return pltpu.make_async_copy(h_hbm.at[pl.ds(tok, 1), :], xbuf.at[sl, pl.ds(r, 1), :],
                                     gsem.at[sl])

    def wait_gather(sl):
        pltpu.make_async_copy(h_hbm.at[pl.ds(0, rows), :], xbuf.at[sl], gsem.at[sl]).wait()

    @pl.when(i == 0)
    def _():
        for b in range(nbuf - 1):
            q0 = first_pos(jnp.minimum(b, last_blk))

            def body(r, _, q0=q0, b=b):
                gather_row(q0, r, b).start()
                return 0
            lax.fori_loop(0, rows, body, 0)

    def weight_copies(e, p):
        return (pltpu.make_async_copy(wg_hbm.at[e], sg.at[p], wsem.at[p]),
                pltpu.make_async_copy(wu_hbm.at[e], su.at[p], wsem.at[p]),
                pltpu.make_async_copy(wd_hbm.at[e], sd.at[p], wsem.at[p]))

    @pl.when(i == 0)
    def _():
        for cp in weight_copies(be_ref[0], 0):
            cp.start()

    @pl.when(i < nused)
    def _():
        @pl.when(first_ref[i] == 1)
        def _():
            p = par_ref[i]
            for cp in weight_copies(be_ref[i], p):
                cp.wait()
            wgb[...] = sg[p].astype(BF16)
            wub[...] = su[p].astype(BF16)
            wdb[...] = sd[p].astype(BF16)

            @pl.when(nexte_ref[i] >= 0)
            def _():
                for cp in weight_copies(nexte_ref[i], 1 - p):
                    cp.start(priority=1)

        wait_gather(slot)
        vbuf[...] = _rms(xbuf[slot], nf_ref[...]).astype(BF16)

        ahead = lax.rem(i + nbuf - 1, nbuf)
        q_next = first_pos(jnp.minimum(i + nbuf - 1, last_blk))
        for r in range(rows):
            gather_row(q_next, r, ahead).start()

        v = vbuf[...]
        hg = jnp.dot(v, wgb[...], preferred_element_type=F32)
        hu = jnp.dot(v, wub[...], preferred_element_type=F32)
        a = (hg * _sigmoid(hg) * hu).astype(BF16)
        o_ref[...] = jnp.dot(a, wdb[...], preferred_element_type=F32)

        @pl.when(i == nused - 1)
        def _():
            for b in range(1, nbuf):
                wait_gather(lax.rem(i + b, nbuf))

    @pl.when(i >= nused)
    def _():
        o_ref[...] = jnp.zeros_like(o_ref)


def experts(plan, h1, nf, w_gate, w_up, w_down, *, rows=EXPERT_ROWS):
    n, d = h1.shape
    ne, _, ff = w_gate.shape
    nblk = plan[0].shape[0]
    hbm = pl.BlockSpec(memory_space=pl.ANY)
    grid_spec = pltpu.PrefetchScalarGridSpec(
        num_scalar_prefetch=len(plan),
        grid=(nblk,),
        in_specs=[hbm, pl.BlockSpec((1, d), lambda i, *_: (0, 0)), hbm, hbm, hbm],
        out_specs=pl.BlockSpec((rows, d), lambda i, *_: (i, 0)),
        scratch_shapes=[
            pltpu.VMEM((EXPERT_GATHER_BUFS, rows, d), F32),
            pltpu.VMEM((rows, d), BF16),
            pltpu.VMEM((2, d, ff), F32),
            pltpu.VMEM((2, d, ff), F32),
            pltpu.VMEM((2, ff, d), F32),
            pltpu.VMEM((d, ff), BF16),
            pltpu.VMEM((d, ff), BF16),
            pltpu.VMEM((ff, d), BF16),
            pltpu.SemaphoreType.DMA((EXPERT_GATHER_BUFS,)),
            pltpu.SemaphoreType.DMA((2,)),
        ],
    )
    return pl.pallas_call(
        functools.partial(_expert_kernel, rows=rows),
        grid_spec=grid_spec,
        out_shape=jax.ShapeDtypeStruct((nblk * rows, d), F32),
        compiler_params=_cparams(("arbitrary",)),
        name="experts",
    )(*plan, h1, nf, w_gate, w_up, w_down)


def _dispatch_plan(eid, rank, counts, rows):
    n_experts = counts.shape[0]
    m = eid.size
    e_flat = eid.reshape(m)
    order = jnp.pad(jnp.argsort(e_flat).astype(jnp.int32), (0, rows))
    nblk_e = (counts + rows - 1) // rows
    bend = jnp.cumsum(nblk_e)
    bstart = bend - nblk_e
    start = jnp.cumsum(counts) - counts
    nblk = (m + n_experts * (rows - 1) + rows - 1) // rows
    bi = jnp.arange(nblk, dtype=jnp.int32)
    block_e = jnp.minimum(jnp.sum((bi[:, None] >= bend[None, :]).astype(jnp.int32), axis=1),
                          n_experts - 1)
    qbase = (start[block_e] + (bi - bstart[block_e]) * rows).astype(jnp.int32)
    nused = bend[-1:].astype(jnp.int32)
    assert nblk <= 256
    first_blk = jnp.dot(jax.nn.one_hot(e_flat, n_experts, dtype=BF16),
                        bstart.astype(F32).astype(BF16)[:, None],
                        preferred_element_type=F32)[:, 0]
    dest = (first_blk.astype(jnp.int32) * rows + rank.reshape(m)).astype(jnp.int32)
    has = counts > 0
    first = ((bi == bstart[block_e]) & (bi < nused[0])).astype(jnp.int32)
    parity = ((jnp.cumsum(has.astype(jnp.int32)) - 1)[block_e] & 1).astype(jnp.int32)
    ids = jnp.arange(n_experts, dtype=jnp.int32)
    at_or_after = lax.cummin(jnp.where(has, ids, n_experts), reverse=True)
    after = jnp.concatenate([at_or_after[1:], jnp.full((1,), n_experts, jnp.int32)])
    next_expert = jnp.where(after < n_experts, after, -1)[block_e].astype(jnp.int32)
    return (block_e, first, parity, next_expert, qbase, nused, order), dest


def _combine_kernel(dest_ref, h_ref, y_hbm, gate_ref, g_ref, o_ref, ybuf, sem, *, tm):
    i = pl.program_id(0)
    nbuf = ybuf.shape[0]
    last = pl.num_programs(0) - 1
    slot = lax.rem(i, nbuf)

    def row_copy(tile, t, k, sl):
        row = dest_ref[(tile * tm + t) * 2 + k]
        return pltpu.make_async_copy(y_hbm.at[pl.ds(row, 1), :], ybuf.at[sl, k, pl.ds(t, 1), :],
                                     sem.at[sl])

    def wait_rows(sl):
        for k in range(2):
            pltpu.make_async_copy(y_hbm.at[pl.ds(0, tm), :], ybuf.at[sl, k], sem.at[sl]).wait()

    @pl.when(i == 0)
    def _():
        for b in range(nbuf - 1):
            tile = jnp.minimum(b, last)

            def body(t, _, tile=tile, b=b):
                row_copy(tile, t, 0, b).start()
                row_copy(tile, t, 1, b).start()
                return 0
            lax.fori_loop(0, tm, body, 0)

    wait_rows(slot)
    ahead = lax.rem(i + nbuf - 1, nbuf)
    nxt = jnp.minimum(i + nbuf - 1, last)
    for t in range(tm):
        for k in range(2):
            row_copy(nxt, t, k, ahead).start()
    gate = gate_ref[...]
    h = h_ref[...] + gate[:, 0:1] * ybuf[slot, 0] + gate[:, 1:2] * ybuf[slot, 1]
    o_ref[...] = _rms(h, g_ref[...])

    @pl.when(i == last)
    def _():
        for b in range(1, nbuf):
            wait_rows(lax.rem(i + b, nbuf))


def combine(dest, h1, y, gate, g, *, tm=256):
    n, d = h1.shape
    tm = min(tm, n)
    grid_spec = pltpu.PrefetchScalarGridSpec(
        num_scalar_prefetch=1,
        grid=(n // tm,),
        in_specs=[
            pl.BlockSpec((tm, d), lambda i, ds: (i, 0)),
            pl.BlockSpec(memory_space=pl.ANY),
            pl.BlockSpec((tm, gate.shape[1]), lambda i, ds: (i, 0)),
            pl.BlockSpec((1, d), lambda i, ds: (0, 0)),
        ],
        out_specs=pl.BlockSpec((tm, d), lambda i, ds: (i, 0)),
        scratch_shapes=[pltpu.VMEM((COMBINE_GATHER_BUFS, 2, tm, d), F32),
                        pltpu.SemaphoreType.DMA((COMBINE_GATHER_BUFS,))],
    )
    return pl.pallas_call(
        functools.partial(_combine_kernel, tm=tm),
        grid_spec=grid_spec,
        out_shape=jax.ShapeDtypeStruct((n, d), F32),
        compiler_params=_cparams(("arbitrary",)),
        name="combine",
    )(dest, h1, y, gate, g)


def kernel(x, meta, norm_mix, w_in, pool_w, pool_scale, ssm_a_re, ssm_a_im, ssm_log_dt, ssm_b_re,
           ssm_b_im, ssm_c_re, ssm_c_im, ssm_d, w_glu, b_glu, w_branch_pool, w_branch_ssm, w_out,
           norm_ffn, w_router_group, b_router_group, w_router_expert, b_router_expert, w_gate,
           w_up, w_down, norm_final):
    assert w_in.shape[0] == 1, "single layer"
    bsz, seq, d = x.shape
    n = bsz * seq
    q = pool_scale.shape[1]
    assert meta.shape[0] == N_META == SSM_CHUNK and N_META >= max(POOL_WINDOWS)
    nc = seq // SSM_CHUNK
    assert nc & (nc - 1) == 0 and nc % SSM_ROWS == 0

    xf = x.reshape(n, d)
    g_mix = norm_mix[0][None, :]
    w_in_b = w_in[0].astype(BF16)

    zps = rms_matmul(xf, g_mix, w_in_b, col0=0, cols=2 * q, act=False, out_dtype=F32)
    zps_meta = rms_matmul(meta.astype(F32), g_mix, w_in_b, col0=0, cols=2 * q, act=False,
                          out_dtype=F32)
    zps_meta = jnp.pad(zps_meta, ((0, SSM_ROWS * SSM_CHUNK - N_META), (0, 0)))
    gates = rms_matmul(xf, g_mix, w_in_b, col0=2 * q, cols=2 * d, act=True, out_dtype=BF16)

    pool_y = pool_mixer(zps.reshape(bsz, seq, 2 * q), zps_meta, pool_w[0].astype(BF16),
                        pool_scale[0][None, :]).reshape(n, q)

    ssm_ops = _ssm_operators(ssm_a_re[0], ssm_a_im[0], ssm_log_dt[0], ssm_b_re[0], ssm_b_im[0],
                             ssm_c_re[0], ssm_c_im[0], nc)
    ssm_y = ssm_chunked(zps, zps_meta, *ssm_ops, bsz=bsz, seq=seq, col0=q)

    n_groups = w_router_group.shape[2]
    n_exp = w_router_expert.shape[2]
    w_r = jnp.concatenate([w_router_group[0], w_router_expert[0]], axis=1)
    w_r = jnp.pad(w_r, ((0, 0), (0, ROUTER_LANES - n_groups - n_exp))).astype(BF16)
    b_r = jnp.concatenate([b_router_group[0], b_router_expert[0]])
    b_r = jnp.pad(b_r, (0, ROUTER_LANES - n_groups - n_exp))[None, :].astype(F32)
    h1, logits = merge(pool_y, ssm_y, zps, gates, xf, ssm_d[0][None, :], w_glu[0].astype(BF16),
                       b_glu[0][None, :], w_branch_pool[0].astype(BF16),
                       w_branch_ssm[0].astype(BF16), w_out[0].astype(BF16),
                       norm_ffn[0][None, :], w_r, b_r)

    gate, eid, rank, cnt = route(logits)
    counts = cnt[0, n_groups:n_groups + n_exp].astype(jnp.int32)
    plan, dest = _dispatch_plan(eid[:, :2], rank[:, :2], counts, EXPERT_ROWS)
    y = experts(plan, h1, norm_ffn[0][None, :], w_gate[0], w_up[0], w_down[0])
    out = combine(dest, h1, y, gate, norm_final[None, :])
    return out.reshape(bsz, seq, d)
```

```python
import functools
import math

import jax
import jax.numpy as jnp
from jax import lax
from jax.experimental import pallas as pl
from jax.experimental.pallas import tpu as pltpu

F32 = jnp.float32
BF16 = jnp.bfloat16

RMS_EPS = 1e-6
POOL_WINDOWS = (2, 4, 8, 16)
N_META = 16
LANES = 128
SSM_CHUNK = 16
SSM_GROUP_CH = 16
SSM_STATE = 64
SSM_OCTET = LANES // SSM_GROUP_CH
SSM_ROWS = 16
SSM_ROWS_BACK = 16
SSM_UNROLL_BACK = 2
SSM_UNROLL = 4
MOE_GROUPS = 8
EXPERTS_PER_GROUP = 8
ROUTER_LANES = LANES
EXPERT_ROWS = 256
EXPERT_GATHER_BUFS = 3
COMBINE_GATHER_BUFS = 3
VMEM_LIMIT = 56 * 1024 * 1024


def _cparams(sem, vmem=VMEM_LIMIT):
    return pltpu.CompilerParams(dimension_semantics=sem, vmem_limit_bytes=vmem)


def _sigmoid(x):
    return 1.0 / (1.0 + jnp.exp(-x))


def _gelu_tanh(x):
    c = math.sqrt(2.0 / math.pi)
    return 0.5 * x * (1.0 + jnp.tanh(c * (x + 0.044715 * (x * x * x))))


def _rms(x, g):
    ms = jnp.mean(x * x, axis=-1, keepdims=True)
    return x * lax.rsqrt(ms + RMS_EPS) * g


def _rms_proj_kernel(x_ref, g_ref, w_ref, o_ref):
    u = _rms(x_ref[...], g_ref[...]).astype(BF16)
    o_ref[...] = jnp.dot(u, w_ref[...], preferred_element_type=F32)


def _in_proj_pool_kernel(x_ref, g_ref, w_ref, zm_ref, pw_ref, sc_ref, pool_ref, z_ref, gate_ref,
                         u_ref, ext_ref, *, tm, gw, blocks_per_seq):
    i = pl.program_id(0)
    j = pl.program_id(1)
    h = N_META

    @pl.when(j == 0)
    def _():
        u_ref[...] = _rms(x_ref[...], g_ref[...]).astype(BF16)

        @pl.when(lax.rem(i, blocks_per_seq) == 0)
        def _():
            ext_ref[0:h, :] = zm_ref[...]

        ext_ref[h:h + tm, :] = jnp.dot(u_ref[...], w_ref[...], preferred_element_type=F32)

    @pl.when(j == 1)
    def _():
        z_ref[...] = jnp.dot(u_ref[...], w_ref[...], preferred_element_type=F32)
        for g, w in enumerate(POOL_WINDOWS):
            cols = slice(g * gw, (g + 1) * gw)
            u = ext_ref[h:h + tm, cols]
            s = u
            for k in range(1, w):
                s = s + ext_ref[h - k:h - k + tm, cols]
            d = s * (1.0 / w) - u
            y = jnp.dot(d.astype(BF16), pw_ref[g], preferred_element_type=F32)
            pool_ref[:, cols] = (y * sc_ref[:, cols]).astype(pool_ref.dtype)
        ext_ref[0:h, :] = ext_ref[tm:tm + h, :]

    @pl.when(j >= 2)
    def _():
        acc = jnp.dot(u_ref[...], w_ref[...], preferred_element_type=F32)
        gate_ref[...] = _sigmoid(acc).astype(gate_ref.dtype)


def in_proj_pool(x, g, w, zmeta, pool_w, pool_scale, *, seq, tm=1024):
    n, d = x.shape
    cols = w.shape[1]
    ng, gw, _ = pool_w.shape
    q = ng * gw
    tm = min(tm, seq)
    assert seq % tm == 0
    return pl.pallas_call(
        functools.partial(_in_proj_pool_kernel, tm=tm, gw=gw, blocks_per_seq=seq // tm),
        grid=(n // tm, cols // q),
        in_specs=[
            pl.BlockSpec((tm, d), lambda i, j: (i, 0)),
            pl.BlockSpec((1, d), lambda i, j: (0, 0)),
            pl.BlockSpec((d, q), lambda i, j: (0, j)),
            pl.BlockSpec((N_META, q), lambda i, j: (0, 0)),
            pl.BlockSpec((ng, gw, gw), lambda i, j: (0, 0, 0)),
            pl.BlockSpec((1, q), lambda i, j: (0, 0)),
        ],
        out_specs=[
            pl.BlockSpec((tm, q), lambda i, j: (i, 0)),
            pl.BlockSpec((tm, q), lambda i, j: (i, 0)),
            pl.BlockSpec((tm, q), lambda i, j: (i, jnp.maximum(j - 2, 0))),
        ],
        out_shape=[jax.ShapeDtypeStruct((n, q), BF16),
                   jax.ShapeDtypeStruct((n, q), F32),
                   jax.ShapeDtypeStruct((n, cols - 2 * q), BF16)],
        scratch_shapes=[pltpu.VMEM((tm, d), BF16), pltpu.VMEM((tm + N_META, q), F32)],
        compiler_params=_cparams(("arbitrary", "arbitrary")),
        name="in_proj_pool",
    )(x, g, w, zmeta, pool_w, pool_scale)


def rms_proj(x, g, w, *, cols, tn=1024):
    n, d = x.shape
    return pl.pallas_call(
        _rms_proj_kernel,
        grid=(cols // tn,),
        in_specs=[
            pl.BlockSpec((n, d), lambda j: (0, 0)),
            pl.BlockSpec((1, d), lambda j: (0, 0)),
            pl.BlockSpec((d, tn), lambda j: (0, j)),
        ],
        out_specs=pl.BlockSpec((n, tn), lambda j: (0, j)),
        out_shape=jax.ShapeDtypeStruct((n, cols), F32),
        compiler_params=_cparams(("parallel",)),
        name="rms_proj",
    )(x, g, w)


def _ssm_toeplitz_kernel(bt_ref, cl_ref, m_ref):
    w = m_ref.shape[-1]
    lane = lax.broadcasted_iota(jnp.int32, (SSM_GROUP_CH, w), 1)
    for g in range(bt_ref.shape[0]):
        kt = jnp.dot(bt_ref[g], cl_ref[g], preferred_element_type=F32,
                     precision=lax.Precision.HIGHEST)
        for s in range(SSM_CHUNK):
            off = s * SSM_GROUP_CH
            blk = kt if s == 0 else jnp.where(lane >= off, pltpu.roll(kt, off, axis=1), 0.0)
            m_ref[g, off:off + SSM_GROUP_CH, :] = blk.astype(m_ref.dtype)


def ssm_toeplitz(bt, cl):
    g, hh, k2 = bt.shape
    w = cl.shape[2]
    gb = SSM_OCTET
    return pl.pallas_call(
        _ssm_toeplitz_kernel,
        grid=(g // gb,),
        in_specs=[pl.BlockSpec((gb, hh, k2), lambda i: (i, 0, 0)),
                  pl.BlockSpec((gb, k2, w), lambda i: (i, 0, 0))],
        out_specs=pl.BlockSpec((gb, w, w), lambda i: (i, 0, 0)),
        out_shape=jax.ShapeDtypeStruct((g, w, w), BF16),
        compiler_params=_cparams(("parallel",)),
        name="ssm_toeplitz",
    )(bt, cl)


def _block_transpose8(a, lane):
    for d in (4, 2, 1):
        w = d * SSM_GROUP_CH
        keep = (lane & w) == 0
        b = list(a)
        for k in range(SSM_OCTET):
            if k & d == 0:
                x, y = a[k], a[k + d]
                b[k] = jnp.where(keep, x, pltpu.roll(y, w, axis=1))
                b[k + d] = jnp.where(keep, pltpu.roll(x, LANES - w, axis=1), y)
        a = b
    return a


def _ssm_kernel(z_ref, zm_ref, m_ref, pre_ref, pim_ref, q_ref, ar_ref, ai_ref, o_ref, uscr, umscr,
                yscr, *, nc, nsteps):
    rt = SSM_ROWS
    t = SSM_CHUNK
    lane = lax.broadcasted_iota(jnp.int32, (rt, LANES), 1)

    def regroup(src_ref, dst_ref, it):
        base = pl.multiple_of(it * (rt * t), rt * t)
        row0 = pl.multiple_of(it * rt, rt)
        for half in range(2):
            a = [src_ref[pl.ds(base + half * SSM_OCTET + k, rt, stride=t), :]
                 for k in range(SSM_OCTET)]
            b = _block_transpose8(a, lane)
            for j in range(SSM_OCTET):
                dst_ref[j, pl.ds(row0, rt), half * LANES:(half + 1) * LANES] = b[j].astype(BF16)

    regroup(zm_ref, umscr, 0)

    def fwd(it, _):
        regroup(z_ref, uscr, it)
        return 0
    lax.fori_loop(0, nc // rt, fwd, 0, unroll=SSM_UNROLL)

    c = lax.broadcasted_iota(jnp.int32, (nc, 2 * SSM_STATE), 0)

    def per_pair(pi, _):
        j0 = 2 * pi
        j1 = j0 + 1
        u0 = uscr[j0]
        u1 = uscr[j1]
        up = jnp.concatenate([u0, u1], axis=1)
        ump = jnp.concatenate([umscr[j0], umscr[j1]], axis=1)
        pre = pre_ref[pi]
        pim = pim_ref[pi]
        s_re = jnp.dot(up, pre, preferred_element_type=F32)
        s_im = jnp.dot(up, pim, preferred_element_type=F32)
        x0_re = jnp.dot(ump, pre, preferred_element_type=F32)[0:1]
        x0_im = jnp.dot(ump, pim, preferred_element_type=F32)[0:1]
        xr = jnp.where(c == 0, x0_re, pltpu.roll(s_re, 1, axis=0))
        xi = jnp.where(c == 0, x0_im, pltpu.roll(s_im, 1, axis=0))
        d = 1
        for k in range(nsteps):
            a = ar_ref[pi, k:k + 1, :]
            b = ai_ref[pi, k:k + 1, :]
            if d % 8:
                zr = jnp.where(c >= d, pltpu.roll(xr, d, axis=0), 0.0)
                zi = jnp.where(c >= d, pltpu.roll(xi, d, axis=0), 0.0)
                xr, xi = xr + a * zr - b * zi, xi + a * zi + b * zr
            else:
                zr, zi = xr[:nc - d], xi[:nc - d]
                xr = jnp.concatenate([xr[:d], xr[d:] + a * zr - b * zi], axis=0)
                xi = jnp.concatenate([xi[:d], xi[d:] + a * zi + b * zr], axis=0)
            d *= 2
        xc = jnp.concatenate([xr, xi], axis=1).astype(BF16)
        y_inter = jnp.dot(xc, q_ref[pi], preferred_element_type=F32)
        w = u0.shape[1]
        y0 = jnp.dot(u0, m_ref[j0], preferred_element_type=F32) + y_inter[:, :w]
        y1 = jnp.dot(u1, m_ref[j1], preferred_element_type=F32) + y_inter[:, w:]
        yscr[j0] = y0
        yscr[j1] = y1
        return 0
    lax.fori_loop(0, SSM_OCTET // 2, per_pair, 0)

    rb = SSM_ROWS_BACK
    lane_b = lax.broadcasted_iota(jnp.int32, (rb, LANES), 1)

    def bwd(it, _):
        base = pl.multiple_of(it * (rb * t), rb * t)
        row0 = pl.multiple_of(it * rb, rb)
        for half in range(2):
            b = [yscr[j, pl.ds(row0, rb), half * LANES:(half + 1) * LANES]
                 for j in range(SSM_OCTET)]
            a = _block_transpose8(b, lane_b)
            for k in range(SSM_OCTET):
                o_ref[pl.ds(base + half * SSM_OCTET + k, rb, stride=t), :] = a[k]
        return 0
    lax.fori_loop(0, nc // rb, bwd, 0, unroll=SSM_UNROLL_BACK)


def ssm_chunked(z, zm, m, pre, pim, q, ar, ai, *, bsz, seq, meta_col0):
    g, w, _ = m.shape
    ns2 = pre.shape[2]
    nsteps = ar.shape[1]
    nc = seq // SSM_CHUNK
    gb = SSM_OCTET
    pb = gb // 2
    cb = meta_col0 // LANES
    return pl.pallas_call(
        functools.partial(_ssm_kernel, nc=nc, nsteps=nsteps),
        grid=(g // gb, bsz),
        in_specs=[
            pl.BlockSpec((seq, LANES), lambda j, b: (b, j)),
            pl.BlockSpec((SSM_ROWS * SSM_CHUNK, LANES), lambda j, b: (0, cb + j)),
            pl.BlockSpec((gb, w, w), lambda j, b: (j, 0, 0)),
            pl.BlockSpec((pb, 2 * w, ns2), lambda j, b: (j, 0, 0)),
            pl.BlockSpec((pb, 2 * w, ns2), lambda j, b: (j, 0, 0)),
            pl.BlockSpec((pb, 2 * ns2, 2 * w), lambda j, b: (j, 0, 0)),
            pl.BlockSpec((pb, nsteps, ns2), lambda j, b: (j, 0, 0)),
            pl.BlockSpec((pb, nsteps, ns2), lambda j, b: (j, 0, 0)),
        ],
        out_specs=pl.BlockSpec((seq, LANES), lambda j, b: (b, j)),
        out_shape=jax.ShapeDtypeStruct((bsz * seq, g * SSM_GROUP_CH), F32),
        scratch_shapes=[
            pltpu.VMEM((gb, nc, w), BF16),
            pltpu.VMEM((gb, SSM_ROWS, w), BF16),
            pltpu.VMEM((gb, nc, w), F32),
        ],
        compiler_params=_cparams(("parallel", "arbitrary")),
        name="ssm_chunked",
    )(z, zm, m, pre, pim, q, ar, ai)


def _ssm_operators(a_re, a_im, log_dt, b_re, b_im, c_re, c_im, nc):
    t = SSM_CHUNK
    ar = a_re.astype(F32)
    ai = a_im.astype(F32)
    dt = jnp.exp(log_dt.astype(F32))[:, None]
    mag = jnp.exp(ar * dt)
    lam_re = mag * jnp.cos(ai * dt)
    lam_im = mag * jnp.sin(ai * dt)
    den = ar * ar + ai * ai
    nr = lam_re - 1.0
    coef_re = (nr * ar + lam_im * ai) / den
    coef_im = (lam_im * ar - nr * ai) / den
    br = b_re.astype(F32)
    bi = b_im.astype(F32)
    bb_re = coef_re[..., None] * br - coef_im[..., None] * bi
    bb_im = coef_re[..., None] * bi + coef_im[..., None] * br
    crt = c_re.astype(F32).transpose(0, 2, 1)
    cit = c_im.astype(F32).transpose(0, 2, 1)

    def lam_pow(k):
        k = k.astype(F32)
        m = jnp.exp(ar[..., None] * dt[..., None] * k)
        ang = ai[..., None] * dt[..., None] * k
        return m * jnp.cos(ang), m * jnp.sin(ang)

    g = ar.shape[0]
    hh = SSM_GROUP_CH
    pr, pi = lam_pow(jnp.arange(t + 1))

    def times_c(xr, xi):
        return (xr[..., None] * crt[:, :, None, :] - xi[..., None] * cit[:, :, None, :],
                xr[..., None] * cit[:, :, None, :] + xi[..., None] * crt[:, :, None, :])

    cl_re, cl_im = times_c(pr[:, :, :t], pi[:, :, :t])
    cl = jnp.concatenate([cl_re.reshape(g, -1, t * hh), cl_im.reshape(g, -1, t * hh)], axis=1)
    bt = jnp.concatenate([bb_re.transpose(0, 2, 1), -bb_im.transpose(0, 2, 1)], axis=-1)
    m_op = ssm_toeplitz(bt, cl)
    rr = pr[:, :, t - 1::-1][:, :, :t]
    ri = pi[:, :, t - 1::-1][:, :, :t]
    p_re = rr[..., None] * bb_re[:, :, None, :] - ri[..., None] * bb_im[:, :, None, :]
    p_im = rr[..., None] * bb_im[:, :, None, :] + ri[..., None] * bb_re[:, :, None, :]
    p_re = p_re.transpose(0, 2, 3, 1).reshape(g, t * hh, -1)
    p_im = p_im.transpose(0, 2, 3, 1).reshape(g, t * hh, -1)
    q_re, q_im = times_c(pr[:, :, 1:], pi[:, :, 1:])
    q_re = q_re.reshape(g, -1, t * hh)
    q_im = -q_im.reshape(g, -1, t * hh)
    nsteps = int(math.log2(nc))
    sr, si = lam_pow(t * (2 ** jnp.arange(nsteps)))

    def pair_rows(x):
        z = jnp.zeros_like(x[0::2])
        return jnp.concatenate([jnp.concatenate([x[0::2], z], axis=-1),
                                jnp.concatenate([z, x[1::2]], axis=-1)], axis=1)

    def pair_lanes(x):
        x = x.transpose(0, 2, 1)
        return jnp.concatenate([x[0::2], x[1::2]], axis=-1)

    pre_op = pair_rows(p_re).astype(BF16)
    pim_op = pair_rows(p_im).astype(BF16)
    q_op = jnp.concatenate([pair_rows(q_re), pair_rows(q_im)], axis=1).astype(BF16)
    return m_op, pre_op, pim_op, q_op, pair_lanes(sr), pair_lanes(si)


def _merge_kernel(p_ref, y_ref, us_ref, gp_ref, gs_ref, h_ref, dv_ref, wglu_ref, bglu_ref,
                  wbp_ref, wbs_ref, wout_ref, nf_ref, wr_ref, br_ref, h1_ref, lg_ref):
    s = _gelu_tanh(y_ref[...] + dv_ref[...] * us_ref[...])
    gl = jnp.dot(s.astype(BF16), wglu_ref[...], preferred_element_type=F32) + bglu_ref[...]
    s = s * _sigmoid(gl)
    y_ssm = jnp.dot(s.astype(BF16), wbs_ref[...], preferred_element_type=F32)
    y_pool = jnp.dot(p_ref[...], wbp_ref[...], preferred_element_type=F32)
    merged = gp_ref[...].astype(F32) * y_pool + gs_ref[...].astype(F32) * y_ssm
    h1 = h_ref[...] + jnp.dot(merged.astype(BF16), wout_ref[...], preferred_element_type=F32)
    h1_ref[...] = h1
    v = _rms(h1, nf_ref[...]).astype(BF16)
    lg_ref[...] = jnp.dot(v, wr_ref[...], preferred_element_type=F32) + br_ref[...]


def merge(pool_y, ssm_y, z_ssm, gates, h0, dvec, w_glu, b_glu, w_bp, w_bs, w_out, nf, w_r, b_r,
          *, tm=256):
    n, d = h0.shape
    q = pool_y.shape[1]
    tm = min(tm, n)
    const = lambda shape: pl.BlockSpec(shape, lambda i: (0,) * len(shape),
                                       pipeline_mode=pl.Buffered(1))
    return pl.pallas_call(
        _merge_kernel,
        grid=(n // tm,),
        in_specs=[
            pl.BlockSpec((tm, q), lambda i: (i, 0)),
            pl.BlockSpec((tm, q), lambda i: (i, 0)),
            pl.BlockSpec((tm, q), lambda i: (i, 0)),
            pl.BlockSpec((tm, d), lambda i: (i, 0)),
            pl.BlockSpec((tm, d), lambda i: (i, 1)),
            pl.BlockSpec((tm, d), lambda i: (i, 0)),
            const((1, q)), const((q, q)), const((1, q)),
            const((q, d)), const((q, d)), const((d, d)),
            const((1, d)), const((d, ROUTER_LANES)), const((1, ROUTER_LANES)),
        ],
        out_specs=[
            pl.BlockSpec((tm, d), lambda i: (i, 0)),
            pl.BlockSpec((tm, ROUTER_LANES), lambda i: (i, 0)),
        ],
        out_shape=[
            jax.ShapeDtypeStruct((n, d), F32),
            jax.ShapeDtypeStruct((n, ROUTER_LANES), F32),
        ],
        compiler_params=_cparams(("parallel",)),
        name="merge",
    )(pool_y, ssm_y, z_ssm, gates, gates, h0, dvec, w_glu, b_glu, w_bp, w_bs, w_out, nf, w_r, b_r)


def _route_kernel(lg_ref, gate_ref, eid_ref, rank_ref, cnt_ref, tri_ref, carry_ref):
    step = pl.program_id(0)
    tm = lg_ref.shape[0]

    @pl.when(step == 0)
    def _():
        rr = lax.broadcasted_iota(jnp.int32, (tm, tm), 0)
        cc = lax.broadcasted_iota(jnp.int32, (tm, tm), 1)
        tri_ref[...] = jnp.where(cc < rr, 1.0, 0.0).astype(BF16)
        carry_ref[...] = jnp.zeros_like(carry_ref)

    l = lg_ref[...]
    lane = lax.broadcasted_iota(jnp.int32, l.shape, 1).astype(F32)
    neg = -jnp.inf
    big = float(ROUTER_LANES)
    is_g = lane < MOE_GROUPS
    gl = jnp.where(is_g, l, neg)
    gmax = jnp.max(gl, axis=1, keepdims=True)
    gstar = jnp.min(jnp.where(gl == gmax, lane, big), axis=1, keepdims=True)
    pg = 1.0 / jnp.sum(jnp.where(is_g, jnp.exp(gl - gmax), 0.0), axis=1, keepdims=True)
    lo = MOE_GROUPS + EXPERTS_PER_GROUP * gstar
    is_e = (lane >= lo) & (lane < lo + EXPERTS_PER_GROUP)
    el = jnp.where(is_e, l, neg)
    m1 = jnp.max(el, axis=1, keepdims=True)
    i1 = jnp.min(jnp.where(is_e & (el == m1), lane, big), axis=1, keepdims=True)
    is_e2 = is_e & (lane != i1)
    el2 = jnp.where(is_e2, l, neg)
    m2 = jnp.max(el2, axis=1, keepdims=True)
    i2 = jnp.min(jnp.where(is_e2 & (el2 == m2), lane, big), axis=1, keepdims=True)
    t = jnp.exp(m2 - m1)
    w1 = pg / (1.0 + t)
    w2 = pg * t / (1.0 + t)
    gate_ref[...] = jnp.where(lane == 0.0, w1, jnp.where(lane == 1.0, w2, 0.0))
    eid = jnp.where(lane == 0.0, i1, jnp.where(lane == 1.0, i2, float(MOE_GROUPS)))
    eid_ref[...] = (eid - MOE_GROUPS).astype(jnp.int32)
    oh1 = lane == i1
    oh2 = lane == i2
    both = jnp.where(oh1 | oh2, 1.0, 0.0)
    before = carry_ref[...] + jnp.dot(tri_ref[...], both.astype(BF16),
                                      preferred_element_type=F32)
    r1 = jnp.sum(jnp.where(oh1, before, 0.0), axis=1, keepdims=True)
    r2 = jnp.sum(jnp.where(oh2, before, 0.0), axis=1, keepdims=True)
    rank_ref[...] = jnp.where(lane == 0.0, r1, jnp.where(lane == 1.0, r2, 0.0)).astype(jnp.int32)
    carry_ref[...] = carry_ref[...] + jnp.sum(both, axis=0, keepdims=True)
    cnt_ref[...] = carry_ref[...]


def route(logits, *, tm=1024):
    n, w = logits.shape
    tm = min(tm, n)
    tok = pl.BlockSpec((tm, w), lambda i: (i, 0))
    return pl.pallas_call(
        _route_kernel,
        grid=(n // tm,),
        in_specs=[tok],
        out_specs=[tok, tok, tok, pl.BlockSpec((1, w), lambda i: (0, 0))],
        out_shape=[jax.ShapeDtypeStruct((n, w), F32), jax.ShapeDtypeStruct((n, w), jnp.int32),
                   jax.ShapeDtypeStruct((n, w), jnp.int32), jax.ShapeDtypeStruct((1, w), F32)],
        scratch_shapes=[pltpu.VMEM((tm, tm), BF16), pltpu.VMEM((1, w), F32)],
        compiler_params=_cparams(("arbitrary",)),
        name="route",
    )(logits)


def _expert_kernel(be_ref, first_ref, par_ref, nexte_ref, qb_ref, nu_ref, ord_ref, h_hbm, nf_ref,
                   wg_hbm, wu_hbm, wd_hbm, o_ref, xbuf, vbuf, sg, su, sd, wgb, wub, wdb, gsem,
                   wsem, *, rows):
    i = pl.program_id(0)
    nused = nu_ref[0]
    nbuf = xbuf.shape[0]
    last_blk = pl.num_programs(0) - 1
    slot = lax.rem(i, nbuf)
    qmax = ord_ref.shape[0] - rows

    def first_pos(blk):
        return jnp.clip(qb_ref[blk], 0, qmax)

    def gather_row(q0, r, sl):
        tok = ord_ref[q0 + r] >> 1
        return pltpu.make_async_copy(h_hbm.at[pl.ds(tok, 1), :], xbuf.at[sl, pl.ds(r, 1), :],
                                     gsem.at[sl])

    def wait_gather(sl):
        pltpu.make_async_copy(h_hbm.at[pl.ds(0, rows), :], xbuf.at[sl], gsem.at[sl]).wait()

    @pl.when(i == 0)
    def _():
        for b in range(nbuf - 1):
            q0 = first_pos(jnp.minimum(b, last_blk))

            def body(r, _, q0=q0, b=b):
                gather_row(q0, r, b).start()
                return 0
            lax.fori_loop(0, rows, body, 0)

    def weight_copies(e, p):
        return (pltpu.make_async_copy(wg_hbm.at[e], sg.at[p], wsem.at[p]),
                pltpu.make_async_copy(wu_hbm.at[e], su.at[p], wsem.at[p]),
                pltpu.make_async_copy(wd_hbm.at[e], sd.at[p], wsem.at[p]))

    @pl.when(i == 0)
    def _():
        for cp in weight_copies(be_ref[0], 0):
            cp.start()

    @pl.when(i < nused)
    def _():
        @pl.when(first_ref[i] == 1)
        def _():
            p = par_ref[i]
            for cp in weight_copies(be_ref[i], p):
                cp.wait()
            wgb[...] = sg[p].astype(BF16)
            wub[...] = su[p].astype(BF16)
            wdb[...] = sd[p].astype(BF16)

            @pl.when(nexte_ref[i] >= 0)
            def _():
                for cp in weight_copies(nexte_ref[i], 1 - p):
                    cp.start(priority=1)

        wait_gather(slot)
        vbuf[...] = _rms(xbuf[slot], nf_ref[...]).astype(BF16)

        ahead = lax.rem(i + nbuf - 1, nbuf)
        q_next = first_pos(jnp.minimum(i + nbuf - 1, last_blk))
        for r in range(rows):
            gather_row(q_next, r, ahead).start()

        v = vbuf[...]
        hg = jnp.dot(v, wgb[...], preferred_element_type=F32)
        hu = jnp.dot(v, wub[...], preferred_element_type=F32)
        a = (hg * _sigmoid(hg) * hu).astype(BF16)
        o_ref[...] = jnp.dot(a, wdb[...], preferred_element_type=F32)

        @pl.when(i == nused - 1)
        def _():
            for b in range(1, nbuf):
                wait_gather(lax.rem(i + b, nbuf))

    @pl.when(i >= nused)
    def _():
        o_ref[...] = jnp.zeros_like(o_ref)


def experts(plan, h1, nf, w_gate, w_up, w_down, *, rows=EXPERT_ROWS):
    n, d = h1.shape
    ne, _, ff = w_gate.shape
    nblk = plan[0].shape[0]
    hbm = pl.BlockSpec(memory_space=pl.ANY)
    grid_spec = pltpu.PrefetchScalarGridSpec(
        num_scalar_prefetch=len(plan),
        grid=(nblk,),
        in_specs=[hbm, pl.BlockSpec((1, d), lambda i, *_: (0, 0)), hbm, hbm, hbm],
        out_specs=pl.BlockSpec((rows, d), lambda i, *_: (i, 0)),
        scratch_shapes=[
            pltpu.VMEM((EXPERT_GATHER_BUFS, rows, d), F32),
            pltpu.VMEM((rows, d), BF16),
            pltpu.VMEM((2, d, ff), F32),
            pltpu.VMEM((2, d, ff), F32),
            pltpu.VMEM((2, ff, d), F32),
            pltpu.VMEM((d, ff), BF16),
            pltpu.VMEM((d, ff), BF16),
            pltpu.VMEM((ff, d), BF16),
            pltpu.SemaphoreType.DMA((EXPERT_GATHER_BUFS,)),
            pltpu.SemaphoreType.DMA((2,)),
        ],
    )
    return pl.pallas_call(
        functools.partial(_expert_kernel, rows=rows),
        grid_spec=grid_spec,
        out_shape=jax.ShapeDtypeStruct((nblk * rows, d), F32),
        compiler_params=_cparams(("arbitrary",)),
        name="experts",
    )(*plan, h1, nf, w_gate, w_up, w_down)


def _dispatch_plan(eid, rank, counts, rows):
    n_experts = counts.shape[0]
    m = eid.size
    e_flat = eid.reshape(m)
    order = jnp.pad(jnp.argsort(e_flat).astype(jnp.int32), (0, rows))
    nblk_e = (counts + rows - 1) // rows
    bend = jnp.cumsum(nblk_e)
    bstart = bend - nblk_e
    start = jnp.cumsum(counts) - counts
    nblk = (m + n_experts * (rows - 1) + rows - 1) // rows
    bi = jnp.arange(nblk, dtype=jnp.int32)
    block_e = jnp.minimum(jnp.sum((bi[:, None] >= bend[None, :]).astype(jnp.int32), axis=1),
                          n_experts - 1)
    qbase = (start[block_e] + (bi - bstart[block_e]) * rows).astype(jnp.int32)
    nused = bend[-1:].astype(jnp.int32)
    assert nblk <= 256
    first_blk = jnp.dot(jax.nn.one_hot(e_flat, n_experts, dtype=BF16),
                        bstart.astype(F32).astype(BF16)[:, None],
                        preferred_element_type=F32)[:, 0]
    dest = (first_blk.astype(jnp.int32) * rows + rank.reshape(m)).astype(jnp.int32)
    has = counts > 0
    first = ((bi == bstart[block_e]) & (bi < nused[0])).astype(jnp.int32)
    parity = ((jnp.cumsum(has.astype(jnp.int32)) - 1)[block_e] & 1).astype(jnp.int32)
    ids = jnp.arange(n_experts, dtype=jnp.int32)
    at_or_after = lax.cummin(jnp.where(has, ids, n_experts), reverse=True)
    after = jnp.concatenate([at_or_after[1:], jnp.full((1,), n_experts, jnp.int32)])
    next_expert = jnp.where(after < n_experts, after, -1)[block_e].astype(jnp.int32)
    return (block_e, first, parity, next_expert, qbase, nused, order), dest


def _combine_kernel(dest_ref, h_ref, y_hbm, gate_ref, g_ref, o_ref, ybuf, sem, *, tm):
    i = pl.program_id(0)
    nbuf = ybuf.shape[0]
    last = pl.num_programs(0) - 1
    slot = lax.rem(i, nbuf)

    def row_copy(tile, t, k, sl):
        row = dest_ref[(tile * tm + t) * 2 + k]
        return pltpu.make_async_copy(y_hbm.at[pl.ds(row, 1), :], ybuf.at[sl, k, pl.ds(t, 1), :],
                                     sem.at[sl])

    def wait_rows(sl):
        for k in range(2):
            pltpu.make_async_copy(y_hbm.at[pl.ds(0, tm), :], ybuf.at[sl, k], sem.at[sl]).wait()

    @pl.when(i == 0)
    def _():
        for b in range(nbuf - 1):
            tile = jnp.minimum(b, last)

            def body(t, _, tile=tile, b=b):
                row_copy(tile, t, 0, b).start()
                row_copy(tile, t, 1, b).start()
                return 0
            lax.fori_loop(0, tm, body, 0)

    wait_rows(slot)
    ahead = lax.rem(i + nbuf - 1, nbuf)
    nxt = jnp.minimum(i + nbuf - 1, last)
    for t in range(tm):
        for k in range(2):
            row_copy(nxt, t, k, ahead).start()
    gate = gate_ref[...]
    h = h_ref[...] + gate[:, 0:1] * ybuf[slot, 0] + gate[:, 1:2] * ybuf[slot, 1]
    o_ref[...] = _rms(h, g_ref[...])

    @pl.when(i == last)
    def _():
        for b in range(1, nbuf):
            wait_rows(lax.rem(i + b, nbuf))


def combine(dest, h1, y, gate, g, *, tm=256):
    n, d = h1.shape
    tm = min(tm, n)
    grid_spec = pltpu.PrefetchScalarGridSpec(
        num_scalar_prefetch=1,
        grid=(n // tm,),
        in_specs=[
            pl.BlockSpec((tm, d), lambda i, ds: (i, 0)),
            pl.BlockSpec(memory_space=pl.ANY),
            pl.BlockSpec((tm, gate.shape[1]), lambda i, ds: (i, 0)),
            pl.BlockSpec((1, d), lambda i, ds: (0, 0)),
        ],
        out_specs=pl.BlockSpec((tm, d), lambda i, ds: (i, 0)),
        scratch_shapes=[pltpu.VMEM((COMBINE_GATHER_BUFS, 2, tm, d), F32),
                        pltpu.SemaphoreType.DMA((COMBINE_GATHER_BUFS,))],
    )
    return pl.pallas_call(
        functools.partial(_combine_kernel, tm=tm),
        grid_spec=grid_spec,
        out_shape=jax.ShapeDtypeStruct((n, d), F32),
        compiler_params=_cparams(("arbitrary",)),
        name="combine",
    )(dest, h1, y, gate, g)


def kernel(x, meta, norm_mix, w_in, pool_w, pool_scale, ssm_a_re, ssm_a_im, ssm_log_dt, ssm_b_re,
           ssm_b_im, ssm_c_re, ssm_c_im, ssm_d, w_glu, b_glu, w_branch_pool, w_branch_ssm, w_out,
           norm_ffn, w_router_group, b_router_group, w_router_expert, b_router_expert, w_gate,
           w_up, w_down, norm_final):
    assert w_in.shape[0] == 1, "single layer"
    bsz, seq, d = x.shape
    n = bsz * seq
    q = pool_scale.shape[1]
    assert meta.shape[0] == N_META == SSM_CHUNK and N_META >= max(POOL_WINDOWS)
    nc = seq // SSM_CHUNK
    assert nc & (nc - 1) == 0 and nc % SSM_ROWS == 0

    xf = x.reshape(n, d)
    g_mix = norm_mix[0][None, :]
    w_in_b = w_in[0].astype(BF16)

    zps_meta = rms_proj(meta.astype(F32), g_mix, w_in_b, cols=2 * q)
    zps_meta = jnp.pad(zps_meta, ((0, SSM_ROWS * SSM_CHUNK - N_META), (0, 0)))
    pool_y, z_ssm, gates = in_proj_pool(xf, g_mix, w_in_b, zps_meta, pool_w[0].astype(BF16),
                                        pool_scale[0][None, :], seq=seq)

    ssm_ops = _ssm_operators(ssm_a_re[0], ssm_a_im[0], ssm_log_dt[0], ssm_b_re[0], ssm_b_im[0],
                             ssm_c_re[0], ssm_c_im[0], nc)
    ssm_y = ssm_chunked(z_ssm, zps_meta, *ssm_ops, bsz=bsz, seq=seq, meta_col0=q)

    n_groups = w_router_group.shape[2]
    n_exp = w_router_expert.shape[2]
    w_r = jnp.concatenate([w_router_group[0], w_router_expert[0]], axis=1)
    w_r = jnp.pad(w_r, ((0, 0), (0, ROUTER_LANES - n_groups - n_exp))).astype(BF16)
    b_r = jnp.concatenate([b_router_group[0], b_router_expert[0]])
    b_r = jnp.pad(b_r, (0, ROUTER_LANES - n_groups - n_exp))[None, :].astype(F32)
    h1, logits = merge(pool_y, ssm_y, z_ssm, gates, xf, ssm_d[0][None, :], w_glu[0].astype(BF16),
                       b_glu[0][None, :], w_branch_pool[0].astype(BF16),
                       w_branch_ssm[0].astype(BF16), w_out[0].astype(BF16),
                       norm_ffn[0][None, :], w_r, b_r)

    gate, eid, rank, cnt = route(logits)
    counts = cnt[0, n_groups:n_groups + n_exp].astype(jnp.int32)
    plan, dest = _dispatch_plan(eid[:, :2], rank[:, :2], counts, EXPERT_ROWS)
    y = experts(plan, h1, norm_ffn[0][None, :], w_gate[0], w_up[0], w_down[0])
    out = combine(dest, h1, y, gate, norm_final[None, :])
    return out.reshape(bsz, seq, d)
```

```python
import functools
import math

import jax
import jax.numpy as jnp
from jax import lax
from jax.experimental import pallas as pl
from jax.experimental.pallas import tpu as pltpu

F32 = jnp.float32
BF16 = jnp.bfloat16

RMS_EPS = 1e-6
POOL_WINDOWS = (2, 4, 8, 16)
N_META = 16
LANES = 128
SSM_CHUNK = 16
SSM_GROUP_CH = 16
SSM_STATE = 64
SSM_OCTET = LANES // SSM_GROUP_CH
SSM_ROWS = 16
SSM_UNROLL = 4
MOE_GROUPS = 8
EXPERTS_PER_GROUP = 8
ROUTER_LANES = LANES
EXPERT_ROWS = 256
EXPERT_GATHER_BUFS = 3
COMBINE_GATHER_BUFS = 3
VMEM_LIMIT = 56 * 1024 * 1024


def _cparams(sem, vmem=VMEM_LIMIT):
    return pltpu.CompilerParams(dimension_semantics=sem, vmem_limit_bytes=vmem)


def _sigmoid(x):
    return 1.0 / (1.0 + jnp.exp(-x))


def _gelu_tanh(x):
    c = math.sqrt(2.0 / math.pi)
    return 0.5 * x * (1.0 + jnp.tanh(c * (x + 0.044715 * (x * x * x))))


def _rms(x, g):
    ms = jnp.mean(x * x, axis=-1, keepdims=True)
    return x * lax.rsqrt(ms + RMS_EPS) * g


def _rms_proj_kernel(x_ref, g_ref, w_ref, o_ref):
    u = _rms(x_ref[...], g_ref[...]).astype(BF16)
    o_ref[...] = jnp.dot(u, w_ref[...], preferred_element_type=F32)


def _in_proj_pool_kernel(x_ref, g_ref, w_ref, zm_ref, pw_ref, sc_ref, pool_ref, z_ref, gate_ref,
                         u_ref, ext_ref, *, tm, gw, blocks_per_seq):
    i = pl.program_id(0)
    j = pl.program_id(1)
    h = N_META

    @pl.when(j == 0)
    def _():
        u_ref[...] = _rms(x_ref[...], g_ref[...]).astype(BF16)

        @pl.when(lax.rem(i, blocks_per_seq) == 0)
        def _():
            ext_ref[0:h, :] = zm_ref[...]

        ext_ref[h:h + tm, :] = jnp.dot(u_ref[...], w_ref[...], preferred_element_type=F32)

    @pl.when(j == 1)
    def _():
        z_ref[...] = jnp.dot(u_ref[...], w_ref[...], preferred_element_type=F32)
        for g, w in enumerate(POOL_WINDOWS):
            cols = slice(g * gw, (g + 1) * gw)
            u = ext_ref[h:h + tm, cols]
            s = u
            for k in range(1, w):
                s = s + ext_ref[h - k:h - k + tm, cols]
            d = s * (1.0 / w) - u
            y = jnp.dot(d.astype(BF16), pw_ref[g], preferred_element_type=F32)
            pool_ref[:, cols] = (y * sc_ref[:, cols]).astype(pool_ref.dtype)
        ext_ref[0:h, :] = ext_ref[tm:tm + h, :]

    @pl.when(j >= 2)
    def _():
        acc = jnp.dot(u_ref[...], w_ref[...], preferred_element_type=F32)
        gate_ref[...] = _sigmoid(acc).astype(gate_ref.dtype)


def in_proj_pool(x, g, w, zmeta, pool_w, pool_scale, *, seq, tm=1024):
    n, d = x.shape
    cols = w.shape[1]
    ng, gw, _ = pool_w.shape
    q = ng * gw
    tm = min(tm, seq)
    assert seq % tm == 0
    return pl.pallas_call(
        functools.partial(_in_proj_pool_kernel, tm=tm, gw=gw, blocks_per_seq=seq // tm),
        grid=(n // tm, cols // q),
        in_specs=[
            pl.BlockSpec((tm, d), lambda i, j: (i, 0)),
            pl.BlockSpec((1, d), lambda i, j: (0, 0)),
            pl.BlockSpec((d, q), lambda i, j: (0, j)),
            pl.BlockSpec((N_META, q), lambda i, j: (0, 0)),
            pl.BlockSpec((ng, gw, gw), lambda i, j: (0, 0, 0)),
            pl.BlockSpec((1, q), lambda i, j: (0, 0)),
        ],
        out_specs=[
            pl.BlockSpec((tm, q), lambda i, j: (i, 0)),
            pl.BlockSpec((tm, q), lambda i, j: (i, 0)),
            pl.BlockSpec((tm, q), lambda i, j: (i, jnp.maximum(j - 2, 0))),
        ],
        out_shape=[jax.ShapeDtypeStruct((n, q), BF16),
                   jax.ShapeDtypeStruct((n, q), F32),
                   jax.ShapeDtypeStruct((n, cols - 2 * q), BF16)],
        scratch_shapes=[pltpu.VMEM((tm, d), BF16), pltpu.VMEM((tm + N_META, q), F32)],
        compiler_params=_cparams(("arbitrary", "arbitrary")),
        name="in_proj_pool",
    )(x, g, w, zmeta, pool_w, pool_scale)


def rms_proj(x, g, w, *, cols, tn=1024):
    n, d = x.shape
    return pl.pallas_call(
        _rms_proj_kernel,
        grid=(cols // tn,),
        in_specs=[
            pl.BlockSpec((n, d), lambda j: (0, 0)),
            pl.BlockSpec((1, d), lambda j: (0, 0)),
            pl.BlockSpec((d, tn), lambda j: (0, j)),
        ],
        out_specs=pl.BlockSpec((n, tn), lambda j: (0, j)),
        out_shape=jax.ShapeDtypeStruct((n, cols), F32),
        compiler_params=_cparams(("parallel",)),
        name="rms_proj",
    )(x, g, w)


def _ssm_toeplitz_kernel(bt_ref, cl_ref, m_ref):
    w = m_ref.shape[-1]
    lane = lax.broadcasted_iota(jnp.int32, (SSM_GROUP_CH, w), 1)
    for g in range(bt_ref.shape[0]):
        kt = jnp.dot(bt_ref[g], cl_ref[g], preferred_element_type=F32,
                     precision=lax.Precision.HIGHEST)
        for s in range(SSM_CHUNK):
            off = s * SSM_GROUP_CH
            blk = kt if s == 0 else jnp.where(lane >= off, pltpu.roll(kt, off, axis=1), 0.0)
            m_ref[g, off:off + SSM_GROUP_CH, :] = blk.astype(m_ref.dtype)


def ssm_toeplitz(bt, cl):
    g, hh, k2 = bt.shape
    w = cl.shape[2]
    gb = SSM_OCTET
    return pl.pallas_call(
        _ssm_toeplitz_kernel,
        grid=(g // gb,),
        in_specs=[pl.BlockSpec((gb, hh, k2), lambda i: (i, 0, 0)),
                  pl.BlockSpec((gb, k2, w), lambda i: (i, 0, 0))],
        out_specs=pl.BlockSpec((gb, w, w), lambda i: (i, 0, 0)),
        out_shape=jax.ShapeDtypeStruct((g, w, w), BF16),
        compiler_params=_cparams(("parallel",)),
        name="ssm_toeplitz",
    )(bt, cl)


def _block_transpose8(a, lane):
    for d in (4, 2, 1):
        w = d * SSM_GROUP_CH
        keep = (lane & w) == 0
        b = list(a)
        for k in range(SSM_OCTET):
            if k & d == 0:
                x, y = a[k], a[k + d]
                b[k] = jnp.where(keep, x, pltpu.roll(y, w, axis=1))
                b[k + d] = jnp.where(keep, pltpu.roll(x, LANES - w, axis=1), y)
        a = b
    return a


def _ssm_kernel(z_ref, zm_ref, m_ref, pre_ref, pim_ref, q_ref, ar_ref, ai_ref, o_ref, uscr, umscr,
                *, nc, nsteps):
    rt = SSM_ROWS
    t = SSM_CHUNK
    lane = lax.broadcasted_iota(jnp.int32, (rt, LANES), 1)

    def regroup(src_ref, dst_ref, it):
        base = pl.multiple_of(it * (rt * t), rt * t)
        row0 = pl.multiple_of(it * rt, rt)
        for half in range(2):
            a = [src_ref[pl.ds(base + half * SSM_OCTET + k, rt, stride=t), :]
                 for k in range(SSM_OCTET)]
            b = _block_transpose8(a, lane)
            for j in range(SSM_OCTET):
                dst_ref[j, pl.ds(row0, rt), half * LANES:(half + 1) * LANES] = b[j].astype(BF16)

    regroup(zm_ref, umscr, 0)

    def fwd(it, _):
        regroup(z_ref, uscr, it)
        return 0
    lax.fori_loop(0, nc // rt, fwd, 0, unroll=SSM_UNROLL)

    c = lax.broadcasted_iota(jnp.int32, (nc, 2 * SSM_STATE), 0)

    def per_pair(pi, _):
        j0 = 2 * pi
        j1 = j0 + 1
        u0 = uscr[j0]
        u1 = uscr[j1]
        up = jnp.concatenate([u0, u1], axis=1)
        ump = jnp.concatenate([umscr[j0], umscr[j1]], axis=1)
        pre = pre_ref[pi]
        pim = pim_ref[pi]
        s_re = jnp.dot(up, pre, preferred_element_type=F32)
        s_im = jnp.dot(up, pim, preferred_element_type=F32)
        x0_re = jnp.dot(ump, pre, preferred_element_type=F32)[0:1]
        x0_im = jnp.dot(ump, pim, preferred_element_type=F32)[0:1]
        xr = jnp.where(c == 0, x0_re, pltpu.roll(s_re, 1, axis=0))
        xi = jnp.where(c == 0, x0_im, pltpu.roll(s_im, 1, axis=0))
        d = 1
        for k in range(nsteps):
            a = ar_ref[pi, k:k + 1, :]
            b = ai_ref[pi, k:k + 1, :]
            if d % 8:
                zr = jnp.where(c >= d, pltpu.roll(xr, d, axis=0), 0.0)
                zi = jnp.where(c >= d, pltpu.roll(xi, d, axis=0), 0.0)
                xr, xi = xr + a * zr - b * zi, xi + a * zi + b * zr
            else:
                zr, zi = xr[:nc - d], xi[:nc - d]
                xr = jnp.concatenate([xr[:d], xr[d:] + a * zr - b * zi], axis=0)
                xi = jnp.concatenate([xi[:d], xi[d:] + a * zi + b * zr], axis=0)
            d *= 2
        xc = jnp.concatenate([xr, xi], axis=1).astype(BF16)
        y_inter = jnp.dot(xc, q_ref[pi], preferred_element_type=F32)
        w = u0.shape[1]
        y0 = jnp.dot(u0, m_ref[j0], preferred_element_type=F32) + y_inter[:, :w]
        y1 = jnp.dot(u1, m_ref[j1], preferred_element_type=F32) + y_inter[:, w:]
        o_ref[j0] = y0
        o_ref[j1] = y1
        return 0
    lax.fori_loop(0, SSM_OCTET // 2, per_pair, 0)


def ssm_chunked(z, zm, m, pre, pim, q, ar, ai, *, bsz, seq, meta_col0):
    g, w, _ = m.shape
    ns2 = pre.shape[2]
    nsteps = ar.shape[1]
    nc = seq // SSM_CHUNK
    gb = SSM_OCTET
    pb = gb // 2
    cb = meta_col0 // LANES
    return pl.pallas_call(
        functools.partial(_ssm_kernel, nc=nc, nsteps=nsteps),
        grid=(g // gb, bsz),
        in_specs=[
            pl.BlockSpec((seq, LANES), lambda j, b: (b, j)),
            pl.BlockSpec((SSM_ROWS * SSM_CHUNK, LANES), lambda j, b: (0, cb + j)),
            pl.BlockSpec((gb, w, w), lambda j, b: (j, 0, 0)),
            pl.BlockSpec((pb, 2 * w, ns2), lambda j, b: (j, 0, 0)),
            pl.BlockSpec((pb, 2 * w, ns2), lambda j, b: (j, 0, 0)),
            pl.BlockSpec((pb, 2 * ns2, 2 * w), lambda j, b: (j, 0, 0)),
            pl.BlockSpec((pb, nsteps, ns2), lambda j, b: (j, 0, 0)),
            pl.BlockSpec((pb, nsteps, ns2), lambda j, b: (j, 0, 0)),
        ],
        out_specs=pl.BlockSpec((gb, nc, w), lambda j, b: (j, b, 0)),
        out_shape=jax.ShapeDtypeStruct((g, bsz * nc, w), F32),
        scratch_shapes=[
            pltpu.VMEM((gb, nc, w), BF16),
            pltpu.VMEM((gb, SSM_ROWS, w), BF16),
        ],
        compiler_params=_cparams(("parallel", "arbitrary")),
        name="ssm_chunked",
    )(z, zm, m, pre, pim, q, ar, ai)


def _ssm_operators(a_re, a_im, log_dt, b_re, b_im, c_re, c_im, nc):
    t = SSM_CHUNK
    ar = a_re.astype(F32)
    ai = a_im.astype(F32)
    dt = jnp.exp(log_dt.astype(F32))[:, None]
    mag = jnp.exp(ar * dt)
    lam_re = mag * jnp.cos(ai * dt)
    lam_im = mag * jnp.sin(ai * dt)
    den = ar * ar + ai * ai
    nr = lam_re - 1.0
    coef_re = (nr * ar + lam_im * ai) / den
    coef_im = (lam_im * ar - nr * ai) / den
    br = b_re.astype(F32)
    bi = b_im.astype(F32)
    bb_re = coef_re[..., None] * br - coef_im[..., None] * bi
    bb_im = coef_re[..., None] * bi + coef_im[..., None] * br
    crt = c_re.astype(F32).transpose(0, 2, 1)
    cit = c_im.astype(F32).transpose(0, 2, 1)

    def lam_pow(k):
        k = k.astype(F32)
        m = jnp.exp(ar[..., None] * dt[..., None] * k)
        ang = ai[..., None] * dt[..., None] * k
        return m * jnp.cos(ang), m * jnp.sin(ang)

    g = ar.shape[0]
    hh = SSM_GROUP_CH
    pr, pi = lam_pow(jnp.arange(t + 1))

    def times_c(xr, xi):
        return (xr[..., None] * crt[:, :, None, :] - xi[..., None] * cit[:, :, None, :],
                xr[..., None] * cit[:, :, None, :] + xi[..., None] * crt[:, :, None, :])

    cl_re, cl_im = times_c(pr[:, :, :t], pi[:, :, :t])
    cl = jnp.concatenate([cl_re.reshape(g, -1, t * hh), cl_im.reshape(g, -1, t * hh)], axis=1)
    bt = jnp.concatenate([bb_re.transpose(0, 2, 1), -bb_im.transpose(0, 2, 1)], axis=-1)
    m_op = ssm_toeplitz(bt, cl)
    rr = pr[:, :, t - 1::-1][:, :, :t]
    ri = pi[:, :, t - 1::-1][:, :, :t]
    p_re = rr[..., None] * bb_re[:, :, None, :] - ri[..., None] * bb_im[:, :, None, :]
    p_im = rr[..., None] * bb_im[:, :, None, :] + ri[..., None] * bb_re[:, :, None, :]
    p_re = p_re.transpose(0, 2, 3, 1).reshape(g, t * hh, -1)
    p_im = p_im.transpose(0, 2, 3, 1).reshape(g, t * hh, -1)
    q_re, q_im = times_c(pr[:, :, 1:], pi[:, :, 1:])
    q_re = q_re.reshape(g, -1, t * hh)
    q_im = -q_im.reshape(g, -1, t * hh)
    nsteps = int(math.log2(nc))
    sr, si = lam_pow(t * (2 ** jnp.arange(nsteps)))

    def pair_rows(x):
        z = jnp.zeros_like(x[0::2])
        return jnp.concatenate([jnp.concatenate([x[0::2], z], axis=-1),
                                jnp.concatenate([z, x[1::2]], axis=-1)], axis=1)

    def pair_lanes(x):
        x = x.transpose(0, 2, 1)
        return jnp.concatenate([x[0::2], x[1::2]], axis=-1)

    pre_op = pair_rows(p_re).astype(BF16)
    pim_op = pair_rows(p_im).astype(BF16)
    q_op = jnp.concatenate([pair_rows(q_re), pair_rows(q_im)], axis=1).astype(BF16)
    return m_op, pre_op, pim_op, q_op, pair_lanes(sr), pair_lanes(si)


def _merge_kernel(p_ref, yg0_ref, ygn_ref, us_ref, gp_ref, gs_ref, h_ref, dv_ref, wglu_ref,
                  bglu_ref, wbp_ref, wbs_ref, wout_ref, nf_ref, wr_ref, br_ref, h1_ref, lg_ref,
                  ynat):
    i = pl.program_id(0)
    slot = lax.rem(i, 2)
    nch = ygn_ref.shape[1]
    lane = lax.broadcasted_iota(jnp.int32, (nch, LANES), 1)

    def regroup(src_ref, sl):
        for o in range(src_ref.shape[0] // SSM_OCTET):
            for half in range(2):
                b = [src_ref[o * SSM_OCTET + j, :, half * LANES:(half + 1) * LANES]
                     for j in range(SSM_OCTET)]
                a = _block_transpose8(b, lane)
                for k in range(SSM_OCTET):
                    ynat[sl, o, pl.ds(half * SSM_OCTET + k, nch, stride=SSM_CHUNK), :] = a[k]

    @pl.when(i == 0)
    def _():
        regroup(yg0_ref, 0)

    y = jnp.concatenate([ynat[slot, o] for o in range(ynat.shape[1])], axis=1)
    regroup(ygn_ref, 1 - slot)
    s = _gelu_tanh(y + dv_ref[...] * us_ref[...])
    gl = jnp.dot(s.astype(BF16), wglu_ref[...], preferred_element_type=F32) + bglu_ref[...]
    s = s * _sigmoid(gl)
    y_ssm = jnp.dot(s.astype(BF16), wbs_ref[...], preferred_element_type=F32)
    y_pool = jnp.dot(p_ref[...], wbp_ref[...], preferred_element_type=F32)
    merged = gp_ref[...].astype(F32) * y_pool + gs_ref[...].astype(F32) * y_ssm
    h1 = h_ref[...] + jnp.dot(merged.astype(BF16), wout_ref[...], preferred_element_type=F32)
    h1_ref[...] = h1
    v = _rms(h1, nf_ref[...]).astype(BF16)
    lg_ref[...] = jnp.dot(v, wr_ref[...], preferred_element_type=F32) + br_ref[...]


def merge(pool_y, ssm_y, z_ssm, gates, h0, dvec, w_glu, b_glu, w_bp, w_bs, w_out, nf, w_r, b_r,
          *, tm=256):
    n, d = h0.shape
    q = pool_y.shape[1]
    tm = min(tm, n)
    groups, _, cw = ssm_y.shape
    nch = tm // SSM_CHUNK
    last = n // tm - 1
    const = lambda shape: pl.BlockSpec(shape, lambda i: (0,) * len(shape),
                                       pipeline_mode=pl.Buffered(1))
    return pl.pallas_call(
        _merge_kernel,
        grid=(n // tm,),
        in_specs=[
            pl.BlockSpec((tm, q), lambda i: (i, 0)),
            pl.BlockSpec((groups, nch, cw), lambda i: (0, 0, 0)),
            pl.BlockSpec((groups, nch, cw),
                         lambda i: (0, jnp.minimum(i + 1, last), 0)),
            pl.BlockSpec((tm, q), lambda i: (i, 0)),
            pl.BlockSpec((tm, d), lambda i: (i, 0)),
            pl.BlockSpec((tm, d), lambda i: (i, 1)),
            pl.BlockSpec((tm, d), lambda i: (i, 0)),
            const((1, q)), const((q, q)), const((1, q)),
            const((q, d)), const((q, d)), const((d, d)),
            const((1, d)), const((d, ROUTER_LANES)), const((1, ROUTER_LANES)),
        ],
        out_specs=[
            pl.BlockSpec((tm, d), lambda i: (i, 0)),
            pl.BlockSpec((tm, ROUTER_LANES), lambda i: (i, 0)),
        ],
        out_shape=[
            jax.ShapeDtypeStruct((n, d), F32),
            jax.ShapeDtypeStruct((n, ROUTER_LANES), F32),
        ],
        scratch_shapes=[pltpu.VMEM((2, groups // SSM_OCTET, tm, LANES), F32)],
        compiler_params=_cparams(("arbitrary",)),
        name="merge",
    )(pool_y, ssm_y, ssm_y, z_ssm, gates, gates, h0, dvec, w_glu, b_glu, w_bp, w_bs, w_out, nf,
      w_r, b_r)


def _route_kernel(lg_ref, gate_ref, eid_ref, rank_ref, cnt_ref, tri_ref, carry_ref):
    step = pl.program_id(0)
    tm = lg_ref.shape[0]

    @pl.when(step == 0)
    def _():
        rr = lax.broadcasted_iota(jnp.int32, (tm, tm), 0)
        cc = lax.broadcasted_iota(jnp.int32, (tm, tm), 1)
        tri_ref[...] = jnp.where(cc < rr, 1.0, 0.0).astype(BF16)
        carry_ref[...] = jnp.zeros_like(carry_ref)

    l = lg_ref[...]
    lane = lax.broadcasted_iota(jnp.int32, l.shape, 1).astype(F32)
    neg = -jnp.inf
    big = float(ROUTER_LANES)
    is_g = lane < MOE_GROUPS
    gl = jnp.where(is_g, l, neg)
    gmax = jnp.max(gl, axis=1, keepdims=True)
    gstar = jnp.min(jnp.where(gl == gmax, lane, big), axis=1, keepdims=True)
    pg = 1.0 / jnp.sum(jnp.where(is_g, jnp.exp(gl - gmax), 0.0), axis=1, keepdims=True)
    lo = MOE_GROUPS + EXPERTS_PER_GROUP * gstar
    is_e = (lane >= lo) & (lane < lo + EXPERTS_PER_GROUP)
    el = jnp.where(is_e, l, neg)
    m1 = jnp.max(el, axis=1, keepdims=True)
    i1 = jnp.min(jnp.where(is_e & (el == m1), lane, big), axis=1, keepdims=True)
    is_e2 = is_e & (lane != i1)
    el2 = jnp.where(is_e2, l, neg)
    m2 = jnp.max(el2, axis=1, keepdims=True)
    i2 = jnp.min(jnp.where(is_e2 & (el2 == m2), lane, big), axis=1, keepdims=True)
    t = jnp.exp(m2 - m1)
    w1 = pg / (1.0 + t)
    w2 = pg * t / (1.0 + t)
    gate_ref[...] = jnp.where(lane == 0.0, w1, jnp.where(lane == 1.0, w2, 0.0))
    eid = jnp.where(lane == 0.0, i1, jnp.where(lane == 1.0, i2, float(MOE_GROUPS)))
    eid_ref[...] = (eid - MOE_GROUPS).astype(jnp.int32)
    oh1 = lane == i1
    oh2 = lane == i2
    both = jnp.where(oh1 | oh2, 1.0, 0.0)
    before = carry_ref[...] + jnp.dot(tri_ref[...], both.astype(BF16),
                                      preferred_element_type=F32)
    r1 = jnp.sum(jnp.where(oh1, before, 0.0), axis=1, keepdims=True)
    r2 = jnp.sum(jnp.where(oh2, before, 0.0), axis=1, keepdims=True)
    rank_ref[...] = jnp.where(lane == 0.0, r1, jnp.where(lane == 1.0, r2, 0.0)).astype(jnp.int32)
    carry_ref[...] = carry_ref[...] + jnp.sum(both, axis=0, keepdims=True)
    cnt_ref[...] = carry_ref[...]


def route(logits, *, tm=1024):
    n, w = logits.shape
    tm = min(tm, n)
    tok = pl.BlockSpec((tm, w), lambda i: (i, 0))
    return pl.pallas_call(
        _route_kernel,
        grid=(n // tm,),
        in_specs=[tok],
        out_specs=[tok, tok, tok, pl.BlockSpec((1, w), lambda i: (0, 0))],
        out_shape=[jax.ShapeDtypeStruct((n, w), F32), jax.ShapeDtypeStruct((n, w), jnp.int32),
                   jax.ShapeDtypeStruct((n, w), jnp.int32), jax.ShapeDtypeStruct((1, w), F32)],
        scratch_shapes=[pltpu.VMEM((tm, tm), BF16), pltpu.VMEM((1, w), F32)],
        compiler_params=_cparams(("arbitrary",)),
        name="route",
    )(logits)


def _expert_kernel(be_ref, first_ref, par_ref, nexte_ref, qb_ref, nu_ref, ord_ref, h_hbm, nf_ref,
                   wg_hbm, wu_hbm, wd_hbm, o_ref, xbuf, vbuf, sg, su, sd, wgb, wub, wdb, gsem,
                   wsem, *, rows):
    i = pl.program_id(0)
    nused = nu_ref[0]
    nbuf = xbuf.shape[0]
    last_blk = pl.num_programs(0) - 1
    slot = lax.rem(i, nbuf)
    qmax = ord_ref.shape[0] - rows

    def first_pos(blk):
        return jnp.clip(qb_ref[blk], 0, qmax)

    def gather_row(q0, r, sl):
        tok = ord_ref[q0 + r] >> 1
        return pltpu.make_async_copy(h_hbm.at[pl.ds(tok, 1), :], xbuf.at[sl, pl.ds(r, 1), :],
                                     gsem.at[sl])

    def wait_gather(sl):
        pltpu.make_async_copy(h_hbm.at[pl.ds(0, rows), :], xbuf.at[sl], gsem.at[sl]).wait()

    @pl.when(i == 0)
    def _():
        for b in range(nbuf - 1):
            q0 = first_pos(jnp.minimum(b, last_blk))

            def body(r, _, q0=q0, b=b):
                gather_row(q0, r, b).start()
                return 0
            lax.fori_loop(0, rows, body, 0)

    def weight_copies(e, p):
        return (pltpu.make_async_copy(wg_hbm.at[e], sg.at[p], wsem.at[p]),
                pltpu.make_async_copy(wu_hbm.at[e], su.at[p], wsem.at[p]),
                pltpu.make_async_copy(wd_hbm.at[e], sd.at[p], wsem.at[p]))

    @pl.when(i == 0)
    def _():
        for cp in weight_copies(be_ref[0], 0):
            cp.start()

    @pl.when(i < nused)
    def _():
        @pl.when(first_ref[i] == 1)
        def _():
            p = par_ref[i]
            for cp in weight_copies(be_ref[i], p):
                cp.wait()
            wgb[...] = sg[p].astype(BF16)
            wub[...] = su[p].astype(BF16)
            wdb[...] = sd[p].astype(BF16)

            @pl.when(nexte_ref[i] >= 0)
            def _():
                for cp in weight_copies(nexte_ref[i], 1 - p):
                    cp.start(priority=1)

        wait_gather(slot)
        vbuf[...] = _rms(xbuf[slot], nf_ref[...]).astype(BF16)

        ahead = lax.rem(i + nbuf - 1, nbuf)
        q_next = first_pos(jnp.minimum(i + nbuf - 1, last_blk))
        for r in range(rows):
            gather_row(q_next, r, ahead).start()

        v = vbuf[...]
        hg = jnp.dot(v, wgb[...], preferred_element_type=F32)
        hu = jnp.dot(v, wub[...], preferred_element_type=F32)
        a = (hg * _sigmoid(hg) * hu).astype(BF16)
        o_ref[...] = jnp.dot(a, wdb[...], preferred_element_type=F32)

        @pl.when(i == nused - 1)
        def _():
            for b in range(1, nbuf):
                wait_gather(lax.rem(i + b, nbuf))

    @pl.when(i >= nused)
    def _():
        o_ref[...] = jnp.zeros_like(o_ref)


def experts(plan, h1, nf, w_gate, w_up, w_down, *, rows=EXPERT_ROWS):
    n, d = h1.shape
    ne, _, ff = w_gate.shape
    nblk = plan[0].shape[0]
    hbm = pl.BlockSpec(memory_space=pl.ANY)
    grid_spec = pltpu.PrefetchScalarGridSpec(
        num_scalar_prefetch=len(plan),
        grid=(nblk,),
        in_specs=[hbm, pl.BlockSpec((1, d), lambda i, *_: (0, 0)), hbm, hbm, hbm],
        out_specs=pl.BlockSpec((rows, d), lambda i, *_: (i, 0)),
        scratch_shapes=[
            pltpu.VMEM((EXPERT_GATHER_BUFS, rows, d), F32),
            pltpu.VMEM((rows, d), BF16),
            pltpu.VMEM((2, d, ff), F32),
            pltpu.VMEM((2, d, ff), F32),
            pltpu.VMEM((2, ff, d), F32),
            pltpu.VMEM((d, ff), BF16),
            pltpu.VMEM((d, ff), BF16),
            pltpu.VMEM((ff, d), BF16),
            pltpu.SemaphoreType.DMA((EXPERT_GATHER_BUFS,)),
            pltpu.SemaphoreType.DMA((2,)),
        ],
    )
    return pl.pallas_call(
        functools.partial(_expert_kernel, rows=rows),
        grid_spec=grid_spec,
        out_shape=jax.ShapeDtypeStruct((nblk * rows, d), F32),
        compiler_params=_cparams(("arbitrary",)),
        name="experts",
    )(*plan, h1, nf, w_gate, w_up, w_down)


def _dispatch_plan(eid, rank, counts, rows):
    n_experts = counts.shape[0]
    m = eid.size
    e_flat = eid.reshape(m)
    order = jnp.pad(jnp.argsort(e_flat).astype(jnp.int32), (0, rows))
    nblk_e = (counts + rows - 1) // rows
    bend = jnp.cumsum(nblk_e)
    bstart = bend - nblk_e
    start = jnp.cumsum(counts) - counts
    nblk = (m + n_experts * (rows - 1) + rows - 1) // rows
    bi = jnp.arange(nblk, dtype=jnp.int32)
    block_e = jnp.minimum(jnp.sum((bi[:, None] >= bend[None, :]).astype(jnp.int32), axis=1),
                          n_experts - 1)
    qbase = (start[block_e] + (bi - bstart[block_e]) * rows).astype(jnp.int32)
    nused = bend[-1:].astype(jnp.int32)
    assert nblk <= 256
    first_blk = jnp.dot(jax.nn.one_hot(e_flat, n_experts, dtype=BF16),
                        bstart.astype(F32).astype(BF16)[:, None],
                        preferred_element_type=F32)[:, 0]
    dest = (first_blk.astype(jnp.int32) * rows + rank.reshape(m)).astype(jnp.int32)
    has = counts > 0
    first = ((bi == bstart[block_e]) & (bi < nused[0])).astype(jnp.int32)
    parity = ((jnp.cumsum(has.astype(jnp.int32)) - 1)[block_e] & 1).astype(jnp.int32)
    ids = jnp.arange(n_experts, dtype=jnp.int32)
    at_or_after = lax.cummin(jnp.where(has, ids, n_experts), reverse=True)
    after = jnp.concatenate([at_or_after[1:], jnp.full((1,), n_experts, jnp.int32)])
    next_expert = jnp.where(after < n_experts, after, -1)[block_e].astype(jnp.int32)
    return (block_e, first, parity, next_expert, qbase, nused, order), dest


def _combine_kernel(dest_ref, h_ref, y_hbm, gate_ref, g_ref, o_ref, ybuf, sem, *, tm):
    i = pl.program_id(0)
    nbuf = ybuf.shape[0]
    last = pl.num_programs(0) - 1
    slot = lax.rem(i, nbuf)

    def row_copy(tile, t, k, sl):
        row = dest_ref[(tile * tm + t) * 2 + k]
        return pltpu.make_async_copy(y_hbm.at[pl.ds(row, 1), :], ybuf.at[sl, k, pl.ds(t, 1), :],
                                     sem.at[sl])

    def wait_rows(sl):
        for k in range(2):
            pltpu.make_async_copy(y_hbm.at[pl.ds(0, tm), :], ybuf.at[sl, k], sem.at[sl]).wait()

    @pl.when(i == 0)
    def _():
        for b in range(nbuf - 1):
            tile = jnp.minimum(b, last)

            def body(t, _, tile=tile, b=b):
                row_copy(tile, t, 0, b).start()
                row_copy(tile, t, 1, b).start()
                return 0
            lax.fori_loop(0, tm, body, 0)

    wait_rows(slot)
    ahead = lax.rem(i + nbuf - 1, nbuf)
    nxt = jnp.minimum(i + nbuf - 1, last)
    for t in range(tm):
        for k in range(2):
            row_copy(nxt, t, k, ahead).start()
    gate = gate_ref[...]
    h = h_ref[...] + gate[:, 0:1] * ybuf[slot, 0] + gate[:, 1:2] * ybuf[slot, 1]
    o_ref[...] = _rms(h, g_ref[...])

    @pl.when(i == last)
    def _():
        for b in range(1, nbuf):
            wait_rows(lax.rem(i + b, nbuf))


def combine(dest, h1, y, gate, g, *, tm=256):
    n, d = h1.shape
    tm = min(tm, n)
    grid_spec = pltpu.PrefetchScalarGridSpec(
        num_scalar_prefetch=1,
        grid=(n // tm,),
        in_specs=[
            pl.BlockSpec((tm, d), lambda i, ds: (i, 0)),
            pl.BlockSpec(memory_space=pl.ANY),
            pl.BlockSpec((tm, gate.shape[1]), lambda i, ds: (i, 0)),
            pl.BlockSpec((1, d), lambda i, ds: (0, 0)),
        ],
        out_specs=pl.BlockSpec((tm, d), lambda i, ds: (i, 0)),
        scratch_shapes=[pltpu.VMEM((COMBINE_GATHER_BUFS, 2, tm, d), F32),
                        pltpu.SemaphoreType.DMA((COMBINE_GATHER_BUFS,))],
    )
    return pl.pallas_call(
        functools.partial(_combine_kernel, tm=tm),
        grid_spec=grid_spec,
        out_shape=jax.ShapeDtypeStruct((n, d), F32),
        compiler_params=_cparams(("arbitrary",)),
        name="combine",
    )(dest, h1, y, gate, g)


def kernel(x, meta, norm_mix, w_in, pool_w, pool_scale, ssm_a_re, ssm_a_im, ssm_log_dt, ssm_b_re,
           ssm_b_im, ssm_c_re, ssm_c_im, ssm_d, w_glu, b_glu, w_branch_pool, w_branch_ssm, w_out,
           norm_ffn, w_router_group, b_router_group, w_router_expert, b_router_expert, w_gate,
           w_up, w_down, norm_final):
    assert w_in.shape[0] == 1, "single layer"
    bsz, seq, d = x.shape
    n = bsz * seq
    q = pool_scale.shape[1]
    assert meta.shape[0] == N_META == SSM_CHUNK and N_META >= max(POOL_WINDOWS)
    nc = seq // SSM_CHUNK
    assert nc & (nc - 1) == 0 and nc % SSM_ROWS == 0

    xf = x.reshape(n, d)
    g_mix = norm_mix[0][None, :]
    w_in_b = w_in[0].astype(BF16)

    zps_meta = rms_proj(meta.astype(F32), g_mix, w_in_b, cols=2 * q)
    zps_meta = jnp.pad(zps_meta, ((0, SSM_ROWS * SSM_CHUNK - N_META), (0, 0)))
    pool_y, z_ssm, gates = in_proj_pool(xf, g_mix, w_in_b, zps_meta, pool_w[0].astype(BF16),
                                        pool_scale[0][None, :], seq=seq)

    ssm_ops = _ssm_operators(ssm_a_re[0], ssm_a_im[0], ssm_log_dt[0], ssm_b_re[0], ssm_b_im[0],
                             ssm_c_re[0], ssm_c_im[0], nc)
    ssm_y = ssm_chunked(z_ssm, zps_meta, *ssm_ops, bsz=bsz, seq=seq, meta_col0=q)

    n_groups = w_router_group.shape[2]
    n_exp = w_router_expert.shape[2]
    w_r = jnp.concatenate([w_router_group[0], w_router_expert[0]], axis=1)
    w_r = jnp.pad(w_r, ((0, 0), (0, ROUTER_LANES - n_groups - n_exp))).astype(BF16)
    b_r = jnp.concatenate([b_router_group[0], b_router_expert[0]])
    b_r = jnp.pad(b_r, (0, ROUTER_LANES - n_groups - n_exp))[None, :].astype(F32)
    h1, logits = merge(pool_y, ssm_y, z_ssm, gates, xf, ssm_d[0][None, :], w_glu[0].astype(BF16),
                       b_glu[0][None, :], w_branch_pool[0].astype(BF16),
                       w_branch_ssm[0].astype(BF16), w_out[0].astype(BF16),
                       norm_ffn[0][None, :], w_r, b_r)

    gate, eid, rank, cnt = route(logits)
    counts = cnt[0, n_groups:n_groups + n_exp].astype(jnp.int32)
    plan, dest = _dispatch_plan(eid[:, :2], rank[:, :2], counts, EXPERT_ROWS)
    y = experts(plan, h1, norm_ffn[0][None, :], w_gate[0], w_up[0], w_down[0])
    out = combine(dest, h1, y, gate, norm_final[None, :])
    return out.reshape(bsz, seq, d)
```

```python
import functools
import math

import jax
import jax.numpy as jnp
from jax import lax
from jax.experimental import pallas as pl
from jax.experimental.pallas import tpu as pltpu

F32 = jnp.float32
BF16 = jnp.bfloat16

RMS_EPS = 1e-6
POOL_WINDOWS = (2, 4, 8, 16)
N_META = 16
LANES = 128
SSM_CHUNK = 16
SSM_GROUP_CH = 16
SSM_STATE = 64
SSM_OCTET = LANES // SSM_GROUP_CH
SSM_ROWS = 16
MOE_GROUPS = 8
EXPERTS_PER_GROUP = 8
ROUTER_LANES = LANES
EXPERT_ROWS = 256
EXPERT_GATHER_BUFS = 3
COMBINE_GATHER_BUFS = 3
VMEM_LIMIT = 56 * 1024 * 1024
IN_PROJ_VMEM_LIMIT = 62 * 1024 * 1024


def _cparams(sem, vmem=VMEM_LIMIT):
    return pltpu.CompilerParams(dimension_semantics=sem, vmem_limit_bytes=vmem)


def _sigmoid(x):
    return 1.0 / (1.0 + jnp.exp(-x))


def _gelu_tanh(x):
    c = math.sqrt(2.0 / math.pi)
    return 0.5 * x * (1.0 + jnp.tanh(c * (x + 0.044715 * (x * x * x))))


def _rms(x, g):
    ms = jnp.mean(x * x, axis=-1, keepdims=True)
    return x * lax.rsqrt(ms + RMS_EPS) * g


def _rms_proj_kernel(x_ref, g_ref, w_ref, o_ref):
    u = _rms(x_ref[...], g_ref[...]).astype(BF16)
    o_ref[...] = jnp.dot(u, w_ref[...], preferred_element_type=F32)


def _block_transpose8(a, lane):
    for d in (4, 2, 1):
        w = d * SSM_GROUP_CH
        keep = (lane & w) == 0
        b = list(a)
        for k in range(SSM_OCTET):
            if k & d == 0:
                x, y = a[k], a[k + d]
                b[k] = jnp.where(keep, x, pltpu.roll(y, w, axis=1))
                b[k + d] = jnp.where(keep, pltpu.roll(x, LANES - w, axis=1), y)
        a = b
    return a


def _regroup_tokens(src_ref, dst_ref, it):
    rt = SSM_ROWS
    t = SSM_CHUNK
    lane = lax.broadcasted_iota(jnp.int32, (rt, LANES), 1)
    base = it * (rt * t)
    row0 = it * rt
    for half in range(2):
        a = [src_ref[pl.ds(base + half * SSM_OCTET + k, rt, stride=t), :]
             for k in range(SSM_OCTET)]
        b = _block_transpose8(a, lane)
        for j in range(SSM_OCTET):
            dst_ref[j, pl.ds(row0, rt), half * LANES:(half + 1) * LANES] = b[j].astype(BF16)


def _in_proj_pool_kernel(x_ref, g_ref, w_ref, zm_ref, pw_ref, sc_ref, pool_ref, z_ref, ug_ref,
                         gate_ref, u_ref, ext_ref, *, tm, gw, blocks_per_seq, octets_per_step):
    i = pl.program_id(0)
    j = pl.program_id(1)
    h = N_META

    @pl.when(j == 0)
    def _():
        u_ref[...] = _rms(x_ref[...], g_ref[...]).astype(BF16)

        @pl.when(lax.rem(i, blocks_per_seq) == 0)
        def _():
            ext_ref[0:h, :] = zm_ref[...]

        ext_ref[h:h + tm, :] = jnp.dot(u_ref[...], w_ref[...], preferred_element_type=F32)

    @pl.when(j == 1)
    def _():
        z = jnp.dot(u_ref[...], w_ref[...], preferred_element_type=F32)
        for o in range(z_ref.shape[0]):
            z_ref[o] = z[:, o * LANES:(o + 1) * LANES]
        for g, w in enumerate(POOL_WINDOWS):
            cols = slice(g * gw, (g + 1) * gw)
            u = ext_ref[h:h + tm, cols]
            s = u
            for k in range(1, w):
                s = s + ext_ref[h - k:h - k + tm, cols]
            d = s * (1.0 / w) - u
            y = jnp.dot(d.astype(BF16), pw_ref[g], preferred_element_type=F32)
            pool_ref[:, cols] = (y * sc_ref[:, cols]).astype(pool_ref.dtype)
        ext_ref[0:h, :] = ext_ref[tm:tm + h, :]

    @pl.when(j >= 2)
    def _():
        acc = jnp.dot(u_ref[...], w_ref[...], preferred_element_type=F32)
        gate_ref[...] = _sigmoid(acc).astype(gate_ref.dtype)
        for s in range(octets_per_step):
            o = (j - 2) * octets_per_step + s
            dst = ug_ref.at[pl.ds(pl.multiple_of(o * SSM_OCTET, SSM_OCTET), SSM_OCTET)]
            for it in range(tm // (SSM_CHUNK * SSM_ROWS)):
                _regroup_tokens(z_ref.at[o], dst, it)


def in_proj_pool(x, g, w, zmeta, pool_w, pool_scale, *, seq, tm=1024):
    n, d = x.shape
    cols = w.shape[1]
    ng, gw, _ = pool_w.shape
    q = ng * gw
    tm = min(tm, seq)
    assert seq % tm == 0 and tm % (SSM_CHUNK * SSM_ROWS) == 0
    slabs = q // LANES
    gate_steps = cols // q - 2
    assert slabs % gate_steps == 0
    groups = q // SSM_GROUP_CH
    cw = SSM_CHUNK * SSM_GROUP_CH
    return pl.pallas_call(
        functools.partial(_in_proj_pool_kernel, tm=tm, gw=gw, blocks_per_seq=seq // tm,
                          octets_per_step=slabs // gate_steps),
        grid=(n // tm, cols // q),
        in_specs=[
            pl.BlockSpec((tm, d), lambda i, j: (i, 0)),
            pl.BlockSpec((1, d), lambda i, j: (0, 0)),
            pl.BlockSpec((d, q), lambda i, j: (0, j)),
            pl.BlockSpec((N_META, q), lambda i, j: (0, 0)),
            pl.BlockSpec((ng, gw, gw), lambda i, j: (0, 0, 0)),
            pl.BlockSpec((1, q), lambda i, j: (0, 0)),
        ],
        out_specs=[
            pl.BlockSpec((tm, q), lambda i, j: (i, 0)),
            pl.BlockSpec((slabs, tm, LANES), lambda i, j: (0, i, 0)),
            pl.BlockSpec((groups, tm // SSM_CHUNK, cw), lambda i, j: (0, i, 0)),
            pl.BlockSpec((tm, q), lambda i, j: (i, jnp.maximum(j - 2, 0))),
        ],
        out_shape=[jax.ShapeDtypeStruct((n, q), BF16),
                   jax.ShapeDtypeStruct((slabs, n, LANES), F32),
                   jax.ShapeDtypeStruct((groups, n // SSM_CHUNK, cw), BF16),
                   jax.ShapeDtypeStruct((n, cols - 2 * q), BF16)],
        scratch_shapes=[pltpu.VMEM((tm, d), BF16), pltpu.VMEM((tm + N_META, q), F32)],
        compiler_params=_cparams(("arbitrary", "arbitrary"), vmem=IN_PROJ_VMEM_LIMIT),
        name="in_proj_pool",
    )(x, g, w, zmeta, pool_w, pool_scale)


def rms_proj(x, g, w, *, cols, tn=1024):
    n, d = x.shape
    return pl.pallas_call(
        _rms_proj_kernel,
        grid=(cols // tn,),
        in_specs=[
            pl.BlockSpec((n, d), lambda j: (0, 0)),
            pl.BlockSpec((1, d), lambda j: (0, 0)),
            pl.BlockSpec((d, tn), lambda j: (0, j)),
        ],
        out_specs=pl.BlockSpec((n, tn), lambda j: (0, j)),
        out_shape=jax.ShapeDtypeStruct((n, cols), F32),
        compiler_params=_cparams(("parallel",)),
        name="rms_proj",
    )(x, g, w)


def _ssm_toeplitz_kernel(bt_ref, cl_ref, m_ref):
    w = m_ref.shape[-1]
    lane = lax.broadcasted_iota(jnp.int32, (SSM_GROUP_CH, w), 1)
    for g in range(bt_ref.shape[0]):
        kt = jnp.dot(bt_ref[g], cl_ref[g], preferred_element_type=F32,
                     precision=lax.Precision.HIGHEST)
        for s in range(SSM_CHUNK):
            off = s * SSM_GROUP_CH
            blk = kt if s == 0 else jnp.where(lane >= off, pltpu.roll(kt, off, axis=1), 0.0)
            m_ref[g, off:off + SSM_GROUP_CH, :] = blk.astype(m_ref.dtype)


def ssm_toeplitz(bt, cl):
    g, hh, k2 = bt.shape
    w = cl.shape[2]
    gb = SSM_OCTET
    return pl.pallas_call(
        _ssm_toeplitz_kernel,
        grid=(g // gb,),
        in_specs=[pl.BlockSpec((gb, hh, k2), lambda i: (i, 0, 0)),
                  pl.BlockSpec((gb, k2, w), lambda i: (i, 0, 0))],
        out_specs=pl.BlockSpec((gb, w, w), lambda i: (i, 0, 0)),
        out_shape=jax.ShapeDtypeStruct((g, w, w), BF16),
        compiler_params=_cparams(("parallel",)),
        name="ssm_toeplitz",
    )(bt, cl)


def _ssm_kernel(u_ref, zm_ref, m_ref, pre_ref, pim_ref, q_ref, ar_ref, ai_ref, o_ref, umscr,
                *, nc, nsteps):
    _regroup_tokens(zm_ref, umscr, 0)

    c = lax.broadcasted_iota(jnp.int32, (nc, 2 * SSM_STATE), 0)

    def per_pair(pi, _):
        j0 = 2 * pi
        j1 = j0 + 1
        u0 = u_ref[j0]
        u1 = u_ref[j1]
        up = jnp.concatenate([u0, u1], axis=1)
        ump = jnp.concatenate([umscr[j0], umscr[j1]], axis=1)
        pre = pre_ref[pi]
        pim = pim_ref[pi]
        s_re = jnp.dot(up, pre, preferred_element_type=F32)
        s_im = jnp.dot(up, pim, preferred_element_type=F32)
        x0_re = jnp.dot(ump, pre, preferred_element_type=F32)[0:1]
        x0_im = jnp.dot(ump, pim, preferred_element_type=F32)[0:1]
        xr = jnp.where(c == 0, x0_re, pltpu.roll(s_re, 1, axis=0))
        xi = jnp.where(c == 0, x0_im, pltpu.roll(s_im, 1, axis=0))
        d = 1
        for k in range(nsteps):
            a = ar_ref[pi, k:k + 1, :]
            b = ai_ref[pi, k:k + 1, :]
            if d % 8:
                zr = jnp.where(c >= d, pltpu.roll(xr, d, axis=0), 0.0)
                zi = jnp.where(c >= d, pltpu.roll(xi, d, axis=0), 0.0)
                xr, xi = xr + a * zr - b * zi, xi + a * zi + b * zr
            else:
                zr, zi = xr[:nc - d], xi[:nc - d]
                xr = jnp.concatenate([xr[:d], xr[d:] + a * zr - b * zi], axis=0)
                xi = jnp.concatenate([xi[:d], xi[d:] + a * zi + b * zr], axis=0)
            d *= 2
        xc = jnp.concatenate([xr, xi], axis=1).astype(BF16)
        y_inter = jnp.dot(xc, q_ref[pi], preferred_element_type=F32)
        w = u0.shape[1]
        y0 = jnp.dot(u0, m_ref[j0], preferred_element_type=F32) + y_inter[:, :w]
        y1 = jnp.dot(u1, m_ref[j1], preferred_element_type=F32) + y_inter[:, w:]
        o_ref[j0] = y0
        o_ref[j1] = y1
        return 0
    lax.fori_loop(0, SSM_OCTET // 2, per_pair, 0, unroll=True)


def ssm_chunked(ug, zm, m, pre, pim, q, ar, ai, *, bsz, seq, meta_col0):
    g, w, _ = m.shape
    ns2 = pre.shape[2]
    nsteps = ar.shape[1]
    nc = seq // SSM_CHUNK
    gb = SSM_OCTET
    pb = gb // 2
    cb = meta_col0 // LANES
    return pl.pallas_call(
        functools.partial(_ssm_kernel, nc=nc, nsteps=nsteps),
        grid=(g // gb, bsz),
        in_specs=[
            pl.BlockSpec((gb, nc, w), lambda j, b: (j, b, 0)),
            pl.BlockSpec((SSM_ROWS * SSM_CHUNK, LANES), lambda j, b: (0, cb + j)),
            pl.BlockSpec((gb, w, w), lambda j, b: (j, 0, 0)),
            pl.BlockSpec((pb, 2 * w, ns2), lambda j, b: (j, 0, 0)),
            pl.BlockSpec((pb, 2 * w, ns2), lambda j, b: (j, 0, 0)),
            pl.BlockSpec((pb, 2 * ns2, 2 * w), lambda j, b: (j, 0, 0)),
            pl.BlockSpec((pb, nsteps, ns2), lambda j, b: (j, 0, 0)),
            pl.BlockSpec((pb, nsteps, ns2), lambda j, b: (j, 0, 0)),
        ],
        out_specs=pl.BlockSpec((gb, nc, w), lambda j, b: (j, b, 0)),
        out_shape=jax.ShapeDtypeStruct((g, bsz * nc, w), F32),
        scratch_shapes=[pltpu.VMEM((gb, SSM_ROWS, w), BF16)],
        compiler_params=_cparams(("parallel", "arbitrary")),
        name="ssm_chunked",
    )(ug, zm, m, pre, pim, q, ar, ai)


def _ssm_operators(a_re, a_im, log_dt, b_re, b_im, c_re, c_im, nc):
    t = SSM_CHUNK
    ar = a_re.astype(F32)
    ai = a_im.astype(F32)
    dt = jnp.exp(log_dt.astype(F32))[:, None]
    mag = jnp.exp(ar * dt)
    lam_re = mag * jnp.cos(ai * dt)
    lam_im = mag * jnp.sin(ai * dt)
    den = ar * ar + ai * ai
    nr = lam_re - 1.0
    coef_re = (nr * ar + lam_im * ai) / den
    coef_im = (lam_im * ar - nr * ai) / den
    br = b_re.astype(F32)
    bi = b_im.astype(F32)
    bb_re = coef_re[..., None] * br - coef_im[..., None] * bi
    bb_im = coef_re[..., None] * bi + coef_im[..., None] * br
    crt = c_re.astype(F32).transpose(0, 2, 1)
    cit = c_im.astype(F32).transpose(0, 2, 1)

    def lam_pow(k):
        k = k.astype(F32)
        m = jnp.exp(ar[..., None] * dt[..., None] * k)
        ang = ai[..., None] * dt[..., None] * k
        return m * jnp.cos(ang), m * jnp.sin(ang)

    g = ar.shape[0]
    hh = SSM_GROUP_CH
    pr, pi = lam_pow(jnp.arange(t + 1))

    def times_c(xr, xi):
        return (xr[..., None] * crt[:, :, None, :] - xi[..., None] * cit[:, :, None, :],
                xr[..., None] * cit[:, :, None, :] + xi[..., None] * crt[:, :, None, :])

    cl_re, cl_im = times_c(pr[:, :, :t], pi[:, :, :t])
    cl = jnp.concatenate([cl_re.reshape(g, -1, t * hh), cl_im.reshape(g, -1, t * hh)], axis=1)
    bt = jnp.concatenate([bb_re.transpose(0, 2, 1), -bb_im.transpose(0, 2, 1)], axis=-1)
    m_op = ssm_toeplitz(bt, cl)
    rr = pr[:, :, t - 1::-1][:, :, :t]
    ri = pi[:, :, t - 1::-1][:, :, :t]
    p_re = rr[..., None] * bb_re[:, :, None, :] - ri[..., None] * bb_im[:, :, None, :]
    p_im = rr[..., None] * bb_im[:, :, None, :] + ri[..., None] * bb_re[:, :, None, :]
    p_re = p_re.transpose(0, 2, 3, 1).reshape(g, t * hh, -1)
    p_im = p_im.transpose(0, 2, 3, 1).reshape(g, t * hh, -1)
    q_re, q_im = times_c(pr[:, :, 1:], pi[:, :, 1:])
    q_re = q_re.reshape(g, -1, t * hh)
    q_im = -q_im.reshape(g, -1, t * hh)
    nsteps = int(math.log2(nc))
    sr, si = lam_pow(t * (2 ** jnp.arange(nsteps)))

    def pair_rows(x):
        z = jnp.zeros_like(x[0::2])
        return jnp.concatenate([jnp.concatenate([x[0::2], z], axis=-1),
                                jnp.concatenate([z, x[1::2]], axis=-1)], axis=1)

    def pair_lanes(x):
        x = x.transpose(0, 2, 1)
        return jnp.concatenate([x[0::2], x[1::2]], axis=-1)

    pre_op = pair_rows(p_re).astype(BF16)
    pim_op = pair_rows(p_im).astype(BF16)
    q_op = jnp.concatenate([pair_rows(q_re), pair_rows(q_im)], axis=1).astype(BF16)
    return m_op, pre_op, pim_op, q_op, pair_lanes(sr), pair_lanes(si)


def _merge_kernel(p_ref, yg0_ref, ygn_ref, us_ref, gp_ref, gs_ref, h_ref, dv_ref, wglu_ref,
                  bglu_ref, wbp_ref, wbs_ref, wout_ref, nf_ref, wr_ref, br_ref, h1_ref, lg_ref,
                  ynat):
    i = pl.program_id(0)
    slot = lax.rem(i, 2)
    nch = ygn_ref.shape[1]
    lane = lax.broadcasted_iota(jnp.int32, (nch, LANES), 1)

    def regroup(src_ref, sl):
        for o in range(src_ref.shape[0] // SSM_OCTET):
            for half in range(2):
                b = [src_ref[o * SSM_OCTET + j, :, half * LANES:(half + 1) * LANES]
                     for j in range(SSM_OCTET)]
                a = _block_transpose8(b, lane)
                for k in range(SSM_OCTET):
                    ynat[sl, o, pl.ds(half * SSM_OCTET + k, nch, stride=SSM_CHUNK), :] = a[k]

    @pl.when(i == 0)
    def _():
        regroup(yg0_ref, 0)

    y = jnp.concatenate([ynat[slot, o] for o in range(ynat.shape[1])], axis=1)
    regroup(ygn_ref, 1 - slot)
    us = jnp.concatenate([us_ref[o] for o in range(us_ref.shape[0])], axis=1)
    s = _gelu_tanh(y + dv_ref[...] * us)
    gl = jnp.dot(s.astype(BF16), wglu_ref[...], preferred_element_type=F32) + bglu_ref[...]
    s = s * _sigmoid(gl)
    y_ssm = jnp.dot(s.astype(BF16), wbs_ref[...], preferred_element_type=F32)
    y_pool = jnp.dot(p_ref[...], wbp_ref[...], preferred_element_type=F32)
    merged = gp_ref[...].astype(F32) * y_pool + gs_ref[...].astype(F32) * y_ssm
    h1 = h_ref[...] + jnp.dot(merged.astype(BF16), wout_ref[...], preferred_element_type=F32)
    h1_ref[...] = h1
    v = _rms(h1, nf_ref[...]).astype(BF16)
    lg_ref[...] = jnp.dot(v, wr_ref[...], preferred_element_type=F32) + br_ref[...]


def merge(pool_y, ssm_y, z_ssm, gates, h0, dvec, w_glu, b_glu, w_bp, w_bs, w_out, nf, w_r, b_r,
          *, tm=256):
    n, d = h0.shape
    q = pool_y.shape[1]
    tm = min(tm, n)
    groups, _, cw = ssm_y.shape
    nch = tm // SSM_CHUNK
    last = n // tm - 1
    const = lambda shape: pl.BlockSpec(shape, lambda i: (0,) * len(shape),
                                       pipeline_mode=pl.Buffered(1))
    return pl.pallas_call(
        _merge_kernel,
        grid=(n // tm,),
        in_specs=[
            pl.BlockSpec((tm, q), lambda i: (i, 0)),
            pl.BlockSpec((groups, nch, cw), lambda i: (0, 0, 0)),
            pl.BlockSpec((groups, nch, cw),
                         lambda i: (0, jnp.minimum(i + 1, last), 0)),
            pl.BlockSpec((q // LANES, tm, LANES), lambda i: (0, i, 0)),
            pl.BlockSpec((tm, d), lambda i: (i, 0)),
            pl.BlockSpec((tm, d), lambda i: (i, 1)),
            pl.BlockSpec((tm, d), lambda i: (i, 0)),
            const((1, q)), const((q, q)), const((1, q)),
            const((q, d)), const((q, d)), const((d, d)),
            const((1, d)), const((d, ROUTER_LANES)), const((1, ROUTER_LANES)),
        ],
        out_specs=[
            pl.BlockSpec((tm, d), lambda i: (i, 0)),
            pl.BlockSpec((tm, ROUTER_LANES), lambda i: (i, 0)),
        ],
        out_shape=[
            jax.ShapeDtypeStruct((n, d), F32),
            jax.ShapeDtypeStruct((n, ROUTER_LANES), F32),
        ],
        scratch_shapes=[pltpu.VMEM((2, groups // SSM_OCTET, tm, LANES), F32)],
        compiler_params=_cparams(("arbitrary",)),
        name="merge",
    )(pool_y, ssm_y, ssm_y, z_ssm, gates, gates, h0, dvec, w_glu, b_glu, w_bp, w_bs, w_out, nf,
      w_r, b_r)


def _route_kernel(lg_ref, gate_ref, eid_ref, rank_ref, cnt_ref, tri_ref, carry_ref):
    step = pl.program_id(0)
    tm = lg_ref.shape[0]

    @pl.when(step == 0)
    def _():
        rr = lax.broadcasted_iota(jnp.int32, (tm, tm), 0)
        cc = lax.broadcasted_iota(jnp.int32, (tm, tm), 1)
        tri_ref[...] = jnp.where(cc < rr, 1.0, 0.0).astype(BF16)
        carry_ref[...] = jnp.zeros_like(carry_ref)

    l = lg_ref[...]
    lane = lax.broadcasted_iota(jnp.int32, l.shape, 1).astype(F32)
    neg = -jnp.inf
    big = float(ROUTER_LANES)
    is_g = lane < MOE_GROUPS
    gl = jnp.where(is_g, l, neg)
    gmax = jnp.max(gl, axis=1, keepdims=True)
    gstar = jnp.min(jnp.where(gl == gmax, lane, big), axis=1, keepdims=True)
    pg = 1.0 / jnp.sum(jnp.where(is_g, jnp.exp(gl - gmax), 0.0), axis=1, keepdims=True)
    lo = MOE_GROUPS + EXPERTS_PER_GROUP * gstar
    is_e = (lane >= lo) & (lane < lo + EXPERTS_PER_GROUP)
    el = jnp.where(is_e, l, neg)
    m1 = jnp.max(el, axis=1, keepdims=True)
    i1 = jnp.min(jnp.where(is_e & (el == m1), lane, big), axis=1, keepdims=True)
    is_e2 = is_e & (lane != i1)
    el2 = jnp.where(is_e2, l, neg)
    m2 = jnp.max(el2, axis=1, keepdims=True)
    i2 = jnp.min(jnp.where(is_e2 & (el2 == m2), lane, big), axis=1, keepdims=True)
    t = jnp.exp(m2 - m1)
    w1 = pg / (1.0 + t)
    w2 = pg * t / (1.0 + t)
    gate_ref[...] = jnp.where(lane == 0.0, w1, jnp.where(lane == 1.0, w2, 0.0))
    eid = jnp.where(lane == 0.0, i1, jnp.where(lane == 1.0, i2, float(MOE_GROUPS)))
    eid_ref[...] = (eid - MOE_GROUPS).astype(jnp.int32)
    oh1 = lane == i1
    oh2 = lane == i2
    both = jnp.where(oh1 | oh2, 1.0, 0.0)
    before = carry_ref[...] + jnp.dot(tri_ref[...], both.astype(BF16),
                                      preferred_element_type=F32)
    r1 = jnp.sum(jnp.where(oh1, before, 0.0), axis=1, keepdims=True)
    r2 = jnp.sum(jnp.where(oh2, before, 0.0), axis=1, keepdims=True)
    rank_ref[...] = jnp.where(lane == 0.0, r1, jnp.where(lane == 1.0, r2, 0.0)).astype(jnp.int32)
    carry_ref[...] = carry_ref[...] + jnp.sum(both, axis=0, keepdims=True)
    cnt_ref[...] = carry_ref[...]


def route(logits, *, tm=1024):
    n, w = logits.shape
    tm = min(tm, n)
    tok = pl.BlockSpec((tm, w), lambda i: (i, 0))
    return pl.pallas_call(
        _route_kernel,
        grid=(n // tm,),
        in_specs=[tok],
        out_specs=[tok, tok, tok, pl.BlockSpec((1, w), lambda i: (0, 0))],
        out_shape=[jax.ShapeDtypeStruct((n, w), F32), jax.ShapeDtypeStruct((n, w), jnp.int32),
                   jax.ShapeDtypeStruct((n, w), jnp.int32), jax.ShapeDtypeStruct((1, w), F32)],
        scratch_shapes=[pltpu.VMEM((tm, tm), BF16), pltpu.VMEM((1, w), F32)],
        compiler_params=_cparams(("arbitrary",)),
        name="route",
    )(logits)


def _expert_kernel(be_ref, first_ref, par_ref, nexte_ref, qb_ref, nu_ref, ord_ref, h_hbm, nf_ref,
                   wg_hbm, wu_hbm, wd_hbm, o_ref, xbuf, vbuf, sg, su, sd, wgb, wub, wdb, gsem,
                   wsem, *, rows):
    i = pl.program_id(0)
    nused = nu_ref[0]
    nbuf = xbuf.shape[0]
    last_blk = pl.num_programs(0) - 1
    slot = lax.rem(i, nbuf)
    qmax = ord_ref.shape[0] - rows

    def first_pos(blk):
        return jnp.clip(qb_ref[blk], 0, qmax)

    def gather_row(q0, r, sl):
        tok = ord_ref[q0 + r] >> 1
        return pltpu.make_async_copy(h_hbm.at[pl.ds(tok, 1), :], xbuf.at[sl, pl.ds(r, 1), :],
                                     gsem.at[sl])

    def wait_gather(sl):
        pltpu.make_async_copy(h_hbm.at[pl.ds(0, rows), :], xbuf.at[sl], gsem.at[sl]).wait()

    @pl.when(i == 0)
    def _():
        for b in range(nbuf - 1):
            q0 = first_pos(jnp.minimum(b, last_blk))

            def body(r, _, q0=q0, b=b):
                gather_row(q0, r, b).start()
                return 0
            lax.fori_loop(0, rows, body, 0)

    def weight_copies(e, p):
        return (pltpu.make_async_copy(wg_hbm.at[e], sg.at[p], wsem.at[p]),
                pltpu.make_async_copy(wu_hbm.at[e], su.at[p], wsem.at[p]),
                pltpu.make_async_copy(wd_hbm.at[e], sd.at[p], wsem.at[p]))

    @pl.when(i == 0)
    def _():
        for cp in weight_copies(be_ref[0], 0):
            cp.start()

    @pl.when(i < nused)
    def _():
        @pl.when(first_ref[i] == 1)
        def _():
            p = par_ref[i]
            for cp in weight_copies(be_ref[i], p):
                cp.wait()
            wgb[...] = sg[p].astype(BF16)
            wub[...] = su[p].astype(BF16)
            wdb[...] = sd[p].astype(BF16)

            @pl.when(nexte_ref[i] >= 0)
            def _():
                for cp in weight_copies(nexte_ref[i], 1 - p):
                    cp.start(priority=1)

        wait_gather(slot)
        vbuf[...] = _rms(xbuf[slot], nf_ref[...]).astype(BF16)

        ahead = lax.rem(i + nbuf - 1, nbuf)
        q_next = first_pos(jnp.minimum(i + nbuf - 1, last_blk))
        for r in range(rows):
            gather_row(q_next, r, ahead).start()

        v = vbuf[...]
        hg = jnp.dot(v, wgb[...], preferred_element_type=F32)
        hu = jnp.dot(v, wub[...], preferred_element_type=F32)
        a = (hg * _sigmoid(hg) * hu).astype(BF16)
        o_ref[...] = jnp.dot(a, wdb[...], preferred_element_type=F32)

        @pl.when(i == nused - 1)
        def _():
            for b in range(1, nbuf):
                wait_gather(lax.rem(i + b, nbuf))

    @pl.when(i >= nused)
    def _():
        o_ref[...] = jnp.zeros_like(o_ref)


def experts(plan, h1, nf, w_gate, w_up, w_down, *, rows=EXPERT_ROWS):
    n, d = h1.shape
    ff = w_gate.shape[2]
    nblk = plan[0].shape[0]
    hbm = pl.BlockSpec(memory_space=pl.ANY)
    grid_spec = pltpu.PrefetchScalarGridSpec(
        num_scalar_prefetch=len(plan),
        grid=(nblk,),
        in_specs=[hbm, pl.BlockSpec((1, d), lambda i, *_: (0, 0)), hbm, hbm, hbm],
        out_specs=pl.BlockSpec((rows, d), lambda i, *_: (i, 0)),
        scratch_shapes=[
            pltpu.VMEM((EXPERT_GATHER_BUFS, rows, d), F32),
            pltpu.VMEM((rows, d), BF16),
            pltpu.VMEM((2, d, ff), F32),
            pltpu.VMEM((2, d, ff), F32),
            pltpu.VMEM((2, ff, d), F32),
            pltpu.VMEM((d, ff), BF16),
            pltpu.VMEM((d, ff), BF16),
            pltpu.VMEM((ff, d), BF16),
            pltpu.SemaphoreType.DMA((EXPERT_GATHER_BUFS,)),
            pltpu.SemaphoreType.DMA((2,)),
        ],
    )
    return pl.pallas_call(
        functools.partial(_expert_kernel, rows=rows),
        grid_spec=grid_spec,
        out_shape=jax.ShapeDtypeStruct((nblk * rows, d), F32),
        compiler_params=_cparams(("arbitrary",)),
        name="experts",
    )(*plan, h1, nf, w_gate, w_up, w_down)


def _dispatch_plan(eid, rank, counts, rows):
    n_experts = counts.shape[0]
    m = eid.size
    e_flat = eid.reshape(m)
    order = jnp.pad(jnp.argsort(e_flat).astype(jnp.int32), (0, rows))
    nblk_e = (counts + rows - 1) // rows
    bend = jnp.cumsum(nblk_e)
    bstart = bend - nblk_e
    start = jnp.cumsum(counts) - counts
    nblk = (m + n_experts * (rows - 1) + rows - 1) // rows
    bi = jnp.arange(nblk, dtype=jnp.int32)
    block_e = jnp.minimum(jnp.sum((bi[:, None] >= bend[None, :]).astype(jnp.int32), axis=1),
                          n_experts - 1)
    qbase = (start[block_e] + (bi - bstart[block_e]) * rows).astype(jnp.int32)
    nused = bend[-1:].astype(jnp.int32)
    assert nblk <= 256
    first_blk = jnp.dot(jax.nn.one_hot(e_flat, n_experts, dtype=BF16),
                        bstart.astype(F32).astype(BF16)[:, None],
                        preferred_element_type=F32)[:, 0]
    dest = (first_blk.astype(jnp.int32) * rows + rank.reshape(m)).astype(jnp.int32)
    has = counts > 0
    first = ((bi == bstart[block_e]) & (bi < nused[0])).astype(jnp.int32)
    parity = ((jnp.cumsum(has.astype(jnp.int32)) - 1)[block_e] & 1).astype(jnp.int32)
    ids = jnp.arange(n_experts, dtype=jnp.int32)
    at_or_after = lax.cummin(jnp.where(has, ids, n_experts), reverse=True)
    after = jnp.concatenate([at_or_after[1:], jnp.full((1,), n_experts, jnp.int32)])
    next_expert = jnp.where(after < n_experts, after, -1)[block_e].astype(jnp.int32)
    return (block_e, first, parity, next_expert, qbase, nused, order), dest


def _combine_kernel(dest_ref, h_ref, y_hbm, gate_ref, g_ref, o_ref, ybuf, sem, *, tm):
    i = pl.program_id(0)
    nbuf = ybuf.shape[0]
    last = pl.num_programs(0) - 1
    slot = lax.rem(i, nbuf)

    def row_copy(tile, t, k, sl):
        row = dest_ref[(tile * tm + t) * 2 + k]
        return pltpu.make_async_copy(y_hbm.at[pl.ds(row, 1), :], ybuf.at[sl, k, pl.ds(t, 1), :],
                                     sem.at[sl])

    def wait_rows(sl):
        for k in range(2):
            pltpu.make_async_copy(y_hbm.at[pl.ds(0, tm), :], ybuf.at[sl, k], sem.at[sl]).wait()

    @pl.when(i == 0)
    def _():
        for b in range(nbuf - 1):
            tile = jnp.minimum(b, last)

            def body(t, _, tile=tile, b=b):
                row_copy(tile, t, 0, b).start()
                row_copy(tile, t, 1, b).start()
                return 0
            lax.fori_loop(0, tm, body, 0)

    wait_rows(slot)
    ahead = lax.rem(i + nbuf - 1, nbuf)
    nxt = jnp.minimum(i + nbuf - 1, last)
    for t in range(tm):
        for k in range(2):
            row_copy(nxt, t, k, ahead).start()
    gate = gate_ref[...]
    h = h_ref[...] + gate[:, 0:1] * ybuf[slot, 0] + gate[:, 1:2] * ybuf[slot, 1]
    o_ref[...] = _rms(h, g_ref[...])

    @pl.when(i == last)
    def _():
        for b in range(1, nbuf):
            wait_rows(lax.rem(i + b, nbuf))


def combine(dest, h1, y, gate, g, *, tm=256):
    n, d = h1.shape
    tm = min(tm, n)
    grid_spec = pltpu.PrefetchScalarGridSpec(
        num_scalar_prefetch=1,
        grid=(n // tm,),
        in_specs=[
            pl.BlockSpec((tm, d), lambda i, ds: (i, 0)),
            pl.BlockSpec(memory_space=pl.ANY),
            pl.BlockSpec((tm, gate.shape[1]), lambda i, ds: (i, 0)),
            pl.BlockSpec((1, d), lambda i, ds: (0, 0)),
        ],
        out_specs=pl.BlockSpec((tm, d), lambda i, ds: (i, 0)),
        scratch_shapes=[pltpu.VMEM((COMBINE_GATHER_BUFS, 2, tm, d), F32),
                        pltpu.SemaphoreType.DMA((COMBINE_GATHER_BUFS,))],
    )
    return pl.pallas_call(
        functools.partial(_combine_kernel, tm=tm),
        grid_spec=grid_spec,
        out_shape=jax.ShapeDtypeStruct((n, d), F32),
        compiler_params=_cparams(("arbitrary",)),
        name="combine",
    )(dest, h1, y, gate, g)


def kernel(x, meta, norm_mix, w_in, pool_w, pool_scale, ssm_a_re, ssm_a_im, ssm_log_dt, ssm_b_re,
           ssm_b_im, ssm_c_re, ssm_c_im, ssm_d, w_glu, b_glu, w_branch_pool, w_branch_ssm, w_out,
           norm_ffn, w_router_group, b_router_group, w_router_expert, b_router_expert, w_gate,
           w_up, w_down, norm_final):
    assert w_in.shape[0] == 1, "single layer"
    bsz, seq, d = x.shape
    n = bsz * seq
    q = pool_scale.shape[1]
    assert meta.shape[0] == N_META == SSM_CHUNK and N_META >= max(POOL_WINDOWS)
    nc = seq // SSM_CHUNK
    assert nc & (nc - 1) == 0 and nc % SSM_ROWS == 0

    xf = x.reshape(n, d)
    g_mix = norm_mix[0][None, :]
    w_in_b = w_in[0].astype(BF16)

    zps_meta = rms_proj(meta.astype(F32), g_mix, w_in_b, cols=2 * q)
    zps_meta = jnp.pad(zps_meta, ((0, SSM_ROWS * SSM_CHUNK - N_META), (0, 0)))
    pool_y, z_ssm, u_ssm, gates = in_proj_pool(xf, g_mix, w_in_b, zps_meta,
                                               pool_w[0].astype(BF16), pool_scale[0][None, :],
                                               seq=seq)

    ssm_ops = _ssm_operators(ssm_a_re[0], ssm_a_im[0], ssm_log_dt[0], ssm_b_re[0], ssm_b_im[0],
                             ssm_c_re[0], ssm_c_im[0], nc)
    ssm_y = ssm_chunked(u_ssm, zps_meta, *ssm_ops, bsz=bsz, seq=seq, meta_col0=q)

    n_groups = w_router_group.shape[2]
    n_exp = w_router_expert.shape[2]
    w_r = jnp.concatenate([w_router_group[0], w_router_expert[0]], axis=1)
    w_r = jnp.pad(w_r, ((0, 0), (0, ROUTER_LANES - n_groups - n_exp))).astype(BF16)
    b_r = jnp.concatenate([b_router_group[0], b_router_expert[0]])
    b_r = jnp.pad(b_r, (0, ROUTER_LANES - n_groups - n_exp))[None, :].astype(F32)
    h1, logits = merge(pool_y, ssm_y, z_ssm, gates, xf, ssm_d[0][None, :], w_glu[0].astype(BF16),
                       b_glu[0][None, :], w_branch_pool[0].astype(BF16),
                       w_branch_ssm[0].astype(BF16), w_out[0].astype(BF16),
                       norm_ffn[0][None, :], w_r, b_r)

    gate, eid, rank, cnt = route(logits)
    counts = cnt[0, n_groups:n_groups + n_exp].astype(jnp.int32)
    plan, dest = _dispatch_plan(eid[:, :2], rank[:, :2], counts, EXPERT_ROWS)
    y = experts(plan, h1, norm_ffn[0][None, :], w_gate[0], w_up[0], w_down[0])
    out = combine(dest, h1, y, gate, norm_final[None, :])
    return out.reshape(bsz, seq, d)
```
